```python
import math
import jax, jax.numpy as jnp
from jax import lax
import numpy as np

D_MODEL = 1024
BATCH = 4
SEQ = 4096
DEPTH = 2

CHUNK = 64
QBLOCK = 128
EPS = 1e-6
ADA_STD = 0.2

D_FF = 4 * D_MODEL

MIX_A = D_MODEL // 2
POOL_WINDOWS = (2, 4, 8, 16)
POOL_GROUPS = len(POOL_WINDOWS)
POOL_CH = MIX_A // POOL_GROUPS
MIX_B = D_MODEL - MIX_A
SSM_GROUP = 16
SSM_GROUPS = MIX_B // SSM_GROUP
SSM_STATE = 64
DT_MIN = 1e-3
DT_MAX = 1e-1

HEAD_DIM = 64
C_HEADS = (D_MODEL // 2) // HEAD_DIM
D_HEADS = (D_MODEL // 2) // HEAD_DIM
C_W = C_HEADS * HEAD_DIM
D_W = D_HEADS * HEAD_DIM
IDX_HEADS = 8
IDX_DIM = 64
IDX_SCALE = (IDX_HEADS ** -0.5) * (IDX_DIM ** -0.5)
TOPK_MAX = 256
IN_CD_WIDTHS = (C_W, C_W, C_W, IDX_HEADS * IDX_DIM, IDX_DIM, IDX_HEADS, D_W, D_W, D_W)
IN_CD = sum(IN_CD_WIDTHS)

N_EVEN = (DEPTH + 1) // 2
N_ODD = DEPTH // 2

kernel_name = "hybrid_pool_s5_dsa_stickbreak_trunk"


def _rmsnorm(x, g):
    xf = x.astype(jnp.float32)
    y = xf * lax.rsqrt(jnp.mean(xf * xf, axis=-1, keepdims=True) + EPS)
    return (y * g.astype(jnp.float32)).astype(x.dtype)


def _modulate(x, c, g, w, b):
    m = (c @ w + b)[:, None, :]
    shift, scale, gate = jnp.split(m, 3, axis=-1)
    h = _rmsnorm(x, g) * (1.0 + scale) + shift
    return h, 1.0 + gate


def _to_blocks(a):
    b, l = a.shape[:2]
    return jnp.moveaxis(a.reshape(b, l // QBLOCK, QBLOCK, *a.shape[2:]), 1, 0)


def _from_blocks(a):
    a = jnp.moveaxis(a, 0, 1)
    return a.reshape(a.shape[0], a.shape[1] * a.shape[2], -1)


def _pool_mixer(u, w_pool, scale):
    bsz, L, _ = u.shape
    ug = u.astype(jnp.float32).reshape(bsz, L, POOL_GROUPS, POOL_CH)
    cs = jnp.concatenate([jnp.zeros((bsz, 1, POOL_GROUPS, POOL_CH), jnp.float32),
                          jnp.cumsum(ug, axis=1)], axis=1)
    t = jnp.arange(L)
    pooled = []
    for g, win in enumerate(POOL_WINDOWS):
        lo = jnp.maximum(t + 1 - win, 0)
        cnt = jnp.minimum(t + 1, win).astype(jnp.float32)
        pooled.append((cs[:, 1:, g] - cs[:, lo, g]) / cnt[None, :, None])
    p = jnp.stack(pooled, axis=2) - ug
    y = jnp.einsum('blgc,gcd->blgd', p, w_pool.astype(jnp.float32))
    return (y.reshape(bsz, L, MIX_A) * scale.astype(jnp.float32)).astype(u.dtype)


def _ssm_combine(e1, e2):
    a1r, a1i, b1r, b1i = e1
    a2r, a2i, b2r, b2i = e2
    return (a2r * a1r - a2i * a1i,
            a2r * a1i + a2i * a1r,
            a2r * b1r - a2i * b1i + b2r,
            a2r * b1i + a2i * b1r + b2i)


def _s5_mixer(u, lam_re, lam_im, log_dt, b_re, b_im, c_re, c_im, d_skip, w_glu, b_glu):
    bsz, L, _ = u.shape
    f32 = jnp.float32
    ug = u.astype(f32).reshape(bsz, L, SSM_GROUPS, SSM_GROUP)
    lr, li = lam_re.astype(f32), lam_im.astype(f32)
    dt = jnp.exp(log_dt.astype(f32))[:, None]
    mag = jnp.exp(lr * dt)
    ar, ai = mag * jnp.cos(li * dt), mag * jnp.sin(li * dt)
    den = lr * lr + li * li
    zr = ((ar - 1.0) * lr + ai * li) / den
    zi = (ai * lr - (ar - 1.0) * li) / den
    br, bi = b_re.astype(f32), b_im.astype(f32)
    bbr = zr[..., None] * br - zi[..., None] * bi
    bbi = zr[..., None] * bi + zi[..., None] * br
    xr = jnp.einsum('blgc,gpc->blgp', ug, bbr)
    xi = jnp.einsum('blgc,gpc->blgp', ug, bbi)
    a_r = jnp.broadcast_to(ar, (1, L, SSM_GROUPS, SSM_STATE))
    a_i = jnp.broadcast_to(ai, (1, L, SSM_GROUPS, SSM_STATE))
    _, _, hr, hi = lax.associative_scan(_ssm_combine, (a_r, a_i, xr, xi), axis=1)
    y = (jnp.einsum('gcp,blgp->blgc', c_re.astype(f32), hr)
         - jnp.einsum('gcp,blgp->blgc', c_im.astype(f32), hi)
         + d_skip.astype(f32).reshape(SSM_GROUPS, SSM_GROUP) * ug)
    y = jax.nn.gelu(y.reshape(bsz, L, MIX_B))
    y = y * jax.nn.sigmoid(y @ w_glu.astype(f32) + b_glu.astype(f32))
    return y.astype(u.dtype)


def _alibi_slopes(n_heads):
    return jnp.exp2(-8.0 * jnp.arange(1, n_heads + 1, dtype=jnp.float32) / n_heads)


def _dsa_mixer(q, k, v, qi, ki, wi):
    L = q.shape[1]
    topk = min(TOPK_MAX, L // 4)
    key_chunk = jnp.arange(L) // CHUNK
    slopes = _alibi_slopes(C_HEADS)
    nb = L // QBLOCK

    def step(args):
        qb, qib, wib, t0 = args
        t = t0 + jnp.arange(QBLOCK)
        q_chunk = t // CHUNK
        adm = key_chunk[None, :] <= q_chunk[:, None]
        rel = jax.nn.relu(jnp.einsum('bqhd,bsd->bqhs', qib, ki))
        score = jnp.einsum('bqhs,bqh->bqs', rel, wib) * IDX_SCALE
        score = jnp.where(adm[None], score, -jnp.inf)
        _, sel = lax.top_k(score, topk)
        valid = (sel // CHUNK) <= q_chunk[None, :, None]
        kg = jax.vmap(lambda kb, ib: kb[ib])(k, sel)
        vg = jax.vmap(lambda vb, ib: vb[ib])(v, sel)
        logits = jnp.einsum('bqhd,bqkhd->bhqk', qb, kg) * (HEAD_DIM ** -0.5)
        dist = jnp.abs(t[None, :, None] - sel).astype(jnp.float32)
        logits = logits - slopes[None, :, None, None] * dist[:, None]
        logits = jnp.where(valid[:, None], logits, -jnp.inf)
        p = jax.nn.softmax(logits, axis=-1)
        return jnp.einsum('bhqk,bqkhd->bqhd', p, vg)

    out = lax.map(step, (_to_blocks(q), _to_blocks(qi), _to_blocks(wi),
                         jnp.arange(nb, dtype=jnp.int32) * QBLOCK))
    return _from_blocks(out)


def _stick_breaking_mixer(q, k, v):
    L = q.shape[1]
    pos = jnp.arange(L)
    nb = L // QBLOCK

    def step(args):
        qb, t0 = args
        t = t0 + jnp.arange(QBLOCK)
        z = jnp.einsum('bqhd,bshd->bhqs', qb, k) * (HEAD_DIM ** -0.5)
        causal = (pos[None, :] < t[:, None])[None, None]
        log_keep = jnp.where(causal, jax.nn.log_sigmoid(-z), 0.0)
        after = lax.cumsum(log_keep, axis=3, reverse=True) - log_keep
        w = jnp.where(causal, jnp.exp(jax.nn.log_sigmoid(z) + after), 0.0)
        return jnp.einsum('bhqs,bshd->bqhd', w, v)

    out = lax.map(step, (_to_blocks(q), jnp.arange(nb, dtype=jnp.int32) * QBLOCK))
    return _from_blocks(out)


def _ab_mixer(h, w_in, w_out, pool_w, pool_scale, lam_re, lam_im, log_dt,
              b_re, b_im, c_re, c_im, d_skip, w_glu, b_glu):
    proj = h @ w_in
    ua, ub = proj[..., :MIX_A], proj[..., MIX_A:]
    ya = _pool_mixer(ua, pool_w, pool_scale)
    yb = _s5_mixer(ub, lam_re, lam_im, log_dt, b_re, b_im, c_re, c_im, d_skip, w_glu, b_glu)
    return jnp.concatenate([ya, yb], axis=-1) @ w_out


def _cd_mixer(h, w_in, w_out):
    bsz, L, _ = h.shape
    f32 = jnp.float32
    proj = (h @ w_in).astype(f32)
    offs = np.cumsum(IN_CD_WIDTHS)[:-1].tolist()
    qc, kc, vc, qi, ki, wi, qd, kd, vd = jnp.split(proj, offs, axis=-1)
    hd = lambda a, n: a.reshape(bsz, L, n, -1)
    oc = _dsa_mixer(hd(qc, C_HEADS), hd(kc, C_HEADS), hd(vc, C_HEADS),
                    hd(qi, IDX_HEADS), ki, wi)
    od = _stick_breaking_mixer(hd(qd, D_HEADS), hd(kd, D_HEADS), hd(vd, D_HEADS))
    y = jnp.concatenate([oc, od], axis=-1).astype(h.dtype)
    return y @ w_out


def _sq_relu_mlp(h, w1, w2):
    a = jax.nn.relu(h @ w1)
    return (a * a) @ w2


def setup_inputs(seed: int = 0) -> dict:
    key = jax.random.key(seed)
    ks = jax.random.split(key, 26)
    f32 = jnp.float32
    nrm = lambda k, shape, s: jax.random.normal(k, shape, f32) * s
    D = D_MODEL
    G, P, C = SSM_GROUPS, SSM_STATE, SSM_GROUP
    return {
        "x": nrm(ks[0], (BATCH, SEQ, D), 1.0),
        "c": nrm(ks[1], (BATCH, D), 1.0),
        "norm_g": 1.0 + nrm(ks[2], (DEPTH, 2, D), 0.1),
        "ada_w": nrm(ks[3], (DEPTH, 2, D, 3 * D), ADA_STD * D ** -0.5),
        "ada_b": nrm(ks[4], (DEPTH, 2, 3 * D), 0.01),
        "mlp_w1": nrm(ks[5], (DEPTH, D, D_FF), D ** -0.5),
        "mlp_w2": nrm(ks[6], (DEPTH, D_FF, D), D_FF ** -0.5),
        "ab_w_in": nrm(ks[7], (N_EVEN, D, MIX_A + MIX_B), D ** -0.5),
        "ab_w_out": nrm(ks[8], (N_EVEN, MIX_A + MIX_B, D), (MIX_A + MIX_B) ** -0.5),
        "pool_w": nrm(ks[9], (N_EVEN, POOL_GROUPS, POOL_CH, POOL_CH), POOL_CH ** -0.5),
        "pool_scale": 1.0 + nrm(ks[10], (N_EVEN, MIX_A), 0.1),
        "ssm_lam_re": -0.5 + nrm(ks[11], (N_EVEN, G, P), 0.01),
        "ssm_lam_im": jnp.pi * jnp.arange(P, dtype=f32) + nrm(ks[12], (N_EVEN, G, P), 0.01),
        "ssm_log_dt": jax.random.uniform(ks[13], (N_EVEN, G), f32, math.log(DT_MIN), math.log(DT_MAX)),
        "ssm_b_re": nrm(ks[14], (N_EVEN, G, P, C), (2 * C) ** -0.5),
        "ssm_b_im": nrm(ks[15], (N_EVEN, G, P, C), (2 * C) ** -0.5),
        "ssm_c_re": nrm(ks[16], (N_EVEN, G, C, P), (2 * P) ** -0.5),
        "ssm_c_im": nrm(ks[17], (N_EVEN, G, C, P), (2 * P) ** -0.5),
        "ssm_d": nrm(ks[18], (N_EVEN, MIX_B), 1.0),
        "glu_w": nrm(ks[19], (N_EVEN, MIX_B, MIX_B), MIX_B ** -0.5),
        "glu_b": nrm(ks[20], (N_EVEN, MIX_B), 0.01),
        "cd_w_in": nrm(ks[21], (N_ODD, D, IN_CD), D ** -0.5),
        "cd_w_out": nrm(ks[22], (N_ODD, C_W + D_W, D), (C_W + D_W) ** -0.5),
        "final_g": 1.0 + nrm(ks[23], (D,), 0.1),
    }


def reference(x, c, norm_g, ada_w, ada_b, mlp_w1, mlp_w2, ab_w_in, ab_w_out, pool_w, pool_scale,
              ssm_lam_re, ssm_lam_im, ssm_log_dt, ssm_b_re, ssm_b_im, ssm_c_re, ssm_c_im, ssm_d,
              glu_w, glu_b, cd_w_in, cd_w_out, final_g):
    for i in range(DEPTH):
        j = i // 2
        h, gate = _modulate(x, c, norm_g[i, 0], ada_w[i, 0], ada_b[i, 0])
        if i % 2 == 0:
            y = _ab_mixer(h, ab_w_in[j], ab_w_out[j], pool_w[j], pool_scale[j],
                          ssm_lam_re[j], ssm_lam_im[j], ssm_log_dt[j], ssm_b_re[j], ssm_b_im[j],
                          ssm_c_re[j], ssm_c_im[j], ssm_d[j], glu_w[j], glu_b[j])
        else:
            y = _cd_mixer(h, cd_w_in[j], cd_w_out[j])
        x = x + gate * y
        h, gate = _modulate(x, c, norm_g[i, 1], ada_w[i, 1], ada_b[i, 1])
        x = x + gate * _sq_relu_mlp(h, mlp_w1[i], mlp_w2[i])
    return _rmsnorm(x, final_g)
```

```python
import functools
import math

import jax
import jax.numpy as jnp
from jax import lax
from jax.experimental import pallas as pl
from jax.experimental.pallas import tpu as pltpu

F32 = jnp.float32
BF16 = jnp.bfloat16

EPS = 1e-6
CHUNK = 64
POOL_WINDOWS = (2, 4, 8, 16)
POOL_CH = 128
SSM_GROUP = 16
SSM_STATE = 64
HEAD_DIM = 64
N_HEADS = 8
IDX_HEADS = 8
IDX_DIM = 64
IDX_SCALE = (IDX_HEADS ** -0.5) * (IDX_DIM ** -0.5)
TOPK_MAX = 256

LANES = 128
SUBLANES = 8
VMEM_LIMIT = 56 * 1024 * 1024

ATT_BLOCK = 128
NEG_BIG = -1e30
COUNT_ALL = 1e9
BISECT_ITERS = 26
MAX_POPS = 4096


def _cparams(sem):
    return pltpu.CompilerParams(dimension_semantics=sem, vmem_limit_bytes=VMEM_LIMIT)


def _dot(a, b):
    return jnp.dot(a, b, preferred_element_type=F32)


def _dot_nt(a, b):
    return lax.dot_general(a, b, (((1,), (1,)), ((), ())), preferred_element_type=F32)


def _rowsum(x):
    return jnp.sum(x, axis=1, keepdims=True)


def _rowmax(x):
    return jnp.max(x, axis=1, keepdims=True)


def _norm_modulate(x, g, shift, scale1):
    ms = jnp.mean(x * x, axis=-1, keepdims=True)
    return (x * lax.rsqrt(ms + EPS)) * g * scale1 + shift


def _mod_kernel(c_ref, w_ref, b_ref, o_ref):
    j = pl.program_id(1)
    m = _dot(c_ref[...], w_ref[...]) + b_ref[...]
    o_ref[...] = m + jnp.where(j > 0, 1.0, 0.0)


def _modulation(c_pad, ada_w, ada_b):
    n_sub, d, _ = ada_w.shape
    rows = c_pad.shape[0]
    return pl.pallas_call(
        _mod_kernel,
        grid=(n_sub, 3),
        in_specs=[
            pl.BlockSpec((rows, d), lambda s, j: (0, 0)),
            pl.BlockSpec((None, d, d), lambda s, j: (s, 0, j)),
            pl.BlockSpec((None, 1, d), lambda s, j: (s, 0, j)),
        ],
        out_specs=pl.BlockSpec((None, None, rows, d), lambda s, j: (s, j, 0, 0)),
        out_shape=jax.ShapeDtypeStruct((n_sub, 3, rows, d), F32),
        compiler_params=_cparams(("arbitrary", "arbitrary")),
    )(c_pad, ada_w, ada_b)


def _norm_proj_kernel(x_ref, g_ref, shift_ref, scale_ref, w_ref, *o_refs, widths):
    h = _norm_modulate(x_ref[...], g_ref[...], shift_ref[...], scale_ref[...]).astype(BF16)
    y = _dot(h, w_ref[...])
    off = 0
    for o_ref, width in zip(o_refs, widths):
        o_ref[...] = y[:, off:off + width].astype(o_ref.dtype)
        off += width


def _norm_proj(x, g, shift, scale1, w, widths, dtypes, tm):
    bsz, seq, d = x.shape
    n = w.shape[1]
    vec = pl.BlockSpec((None, 1, d), lambda b, i: (b, 0, 0))
    return pl.pallas_call(
        functools.partial(_norm_proj_kernel, widths=widths),
        grid=(bsz, seq // tm),
        in_specs=[
            pl.BlockSpec((None, tm, d), lambda b, i: (b, i, 0)),
            pl.BlockSpec((1, d), lambda b, i: (0, 0)),
            vec, vec,
            pl.BlockSpec((d, n), lambda b, i: (0, 0)),
        ],
        out_specs=[pl.BlockSpec((None, tm, wd), lambda b, i: (b, i, 0)) for wd in widths],
        out_shape=[jax.ShapeDtypeStruct((bsz, seq, wd), dt) for wd, dt in zip(widths, dtypes)],
        compiler_params=_cparams(("parallel", "parallel")),
    )(x, g, shift, scale1, w)


def _out_proj_kernel(x_ref, gate_ref, *refs, n_y):
    y_refs, w_ref, o_ref = refs[:n_y], refs[n_y], refs[n_y + 1]
    y = jnp.concatenate([r[...] for r in y_refs], axis=1) if n_y > 1 else y_refs[0][...]
    o_ref[...] = x_ref[...] + gate_ref[...] * _dot(y, w_ref[...])


def _out_proj(x, gate1, ys, w, tm):
    bsz, seq, d = x.shape
    k = w.shape[0]
    return pl.pallas_call(
        functools.partial(_out_proj_kernel, n_y=len(ys)),
        grid=(bsz, seq // tm),
        in_specs=[
            pl.BlockSpec((None, tm, d), lambda b, i: (b, i, 0)),
            pl.BlockSpec((None, 1, d), lambda b, i: (b, 0, 0)),
            *[pl.BlockSpec((None, tm, y.shape[2]), lambda b, i: (b, i, 0)) for y in ys],
            pl.BlockSpec((k, d), lambda b, i: (0, 0)),
        ],
        out_specs=pl.BlockSpec((None, tm, d), lambda b, i: (b, i, 0)),
        out_shape=jax.ShapeDtypeStruct((bsz, seq, d), F32),
        compiler_params=_cparams(("parallel", "parallel")),
    )(x, gate1, *ys, w)


def _mlp_kernel(x_ref, g_ref, shift_ref, scale_ref, gate_ref, w1_ref, w2_ref, fg_ref, o_ref,
                h_s, acc_s, *, final_norm):
    f = pl.program_id(2)

    @pl.when(f == 0)
    def _():
        h_s[...] = _norm_modulate(x_ref[...], g_ref[...], shift_ref[...], scale_ref[...]).astype(BF16)
        acc_s[...] = jnp.zeros_like(acc_s)

    a = jnp.maximum(_dot(h_s[...], w1_ref[...]), 0.0)
    acc_s[...] += _dot((a * a).astype(BF16), w2_ref[...])

    @pl.when(f == pl.num_programs(2) - 1)
    def _():
        y = x_ref[...] + gate_ref[...] * acc_s[...]
        if final_norm:
            ms = jnp.mean(y * y, axis=-1, keepdims=True)
            y = (y * lax.rsqrt(ms + EPS)) * fg_ref[...]
        o_ref[...] = y


def _mlp(x, g, shift, scale1, gate1, w1, w2, final_g, final_norm, tm, tf):
    bsz, seq, d = x.shape
    dff = w1.shape[1]
    vec = pl.BlockSpec((None, 1, d), lambda b, i, f: (b, 0, 0))
    one = pl.BlockSpec((1, d), lambda b, i, f: (0, 0))
    return pl.pallas_call(
        functools.partial(_mlp_kernel, final_norm=final_norm),
        grid=(bsz, seq // tm, dff // tf),
        in_specs=[
            pl.BlockSpec((None, tm, d), lambda b, i, f: (b, i, 0)),
            one, vec, vec, vec,
            pl.BlockSpec((d, tf), lambda b, i, f: (0, f)),
            pl.BlockSpec((tf, d), lambda b, i, f: (f, 0)),
            one,
        ],
        out_specs=pl.BlockSpec((None, tm, d), lambda b, i, f: (b, i, 0)),
        out_shape=jax.ShapeDtypeStruct((bsz, seq, d), F32),
        scratch_shapes=[pltpu.VMEM((tm, d), BF16), pltpu.VMEM((tm, d), F32)],
        compiler_params=_cparams(("parallel", "parallel", "arbitrary")),
    )(x, g, shift, scale1, gate1, w1, w2, final_g)


def _cmul(ar, ai, br, bi):
    return ar * br - ai * bi, ar * bi + ai * br


def _s5_params_kernel(lr_ref, li_ref, ldt_ref, brt_ref, bit_ref, crt_ref, cit_ref,
                      wb_ref, wc_ref, as_re_ref, as_im_ref, pw_re_ref, pw_im_ref, a8_re_ref, a8_im_ref):
    lr, li = lr_ref[...], li_ref[...]
    dt = jnp.exp(ldt_ref[...])
    mag = jnp.exp(lr * dt)
    ar, ai = mag * jnp.cos(li * dt), mag * jnp.sin(li * dt)
    den = lr * lr + li * li
    zr = ((ar - 1.0) * lr + ai * li) / den
    zi = (ai * lr - (ar - 1.0) * li) / den

    rows, cols = brt_ref.shape
    rgrp = lax.broadcasted_iota(jnp.int32, (rows, cols), 0) // SSM_GROUP
    cgrp = lax.broadcasted_iota(jnp.int32, (rows, cols), 1) // SSM_STATE
    diag = rgrp == cgrp
    br, bi = brt_ref[...], bit_ref[...]
    wb_ref[0] = jnp.where(diag, zr * br - zi * bi, 0.0).astype(wb_ref.dtype)
    wb_ref[1] = jnp.where(diag, zr * bi + zi * br, 0.0).astype(wb_ref.dtype)
    wc_ref[0] = jnp.where(diag, crt_ref[...], 0.0).astype(wc_ref.dtype)
    wc_ref[1] = jnp.where(diag, cit_ref[...], 0.0).astype(wc_ref.dtype)

    n = lr.shape[1]
    row = lax.broadcasted_iota(jnp.int32, (SUBLANES, n), 0)
    pr, pi = jnp.broadcast_to(ar, (SUBLANES, n)), jnp.broadcast_to(ai, (SUBLANES, n))
    a1r, a1i = pr, pi
    for k in range(3):
        keep = row >= (1 << k)
        as_re_ref[k] = jnp.where(keep, pr, 0.0)
        as_im_ref[k] = jnp.where(keep, pi, 0.0)
        pr, pi = _cmul(pr, pi, pr, pi)
    a8_re_ref[...] = pr
    a8_im_ref[...] = pi
    qr, qi = a1r, a1i
    for r in range(1, SUBLANES):
        nr, ni = _cmul(qr, qi, a1r, a1i)
        qr = jnp.where(row >= r, nr, qr)
        qi = jnp.where(row >= r, ni, qi)
    pw_re_ref[...] = qr
    pw_im_ref[...] = qi


def _s5_params(lam_re, lam_im, log_dt, b_re, b_im, c_re, c_im):
    g, p = lam_re.shape
    c = b_re.shape[2]
    n = g * p
    flat = lambda a: a.reshape(1, n)
    ldt = jnp.broadcast_to(log_dt[:, None], (g, p)).reshape(1, n)
    b_t = lambda a: jnp.tile(jnp.transpose(a, (2, 0, 1)).reshape(c, n), (g, 1))
    c_t = lambda a: jnp.tile(a.reshape(g * c, p), (1, g))
    outs = pl.pallas_call(
        _s5_params_kernel,
        out_shape=[jax.ShapeDtypeStruct((2, g * c, n), BF16), jax.ShapeDtypeStruct((2, g * c, n), BF16),
                   jax.ShapeDtypeStruct((3, SUBLANES, n), F32), jax.ShapeDtypeStruct((3, SUBLANES, n), F32),
                   jax.ShapeDtypeStruct((SUBLANES, n), F32), jax.ShapeDtypeStruct((SUBLANES, n), F32),
                   jax.ShapeDtypeStruct((SUBLANES, n), F32), jax.ShapeDtypeStruct((SUBLANES, n), F32)],
        compiler_params=pltpu.CompilerParams(vmem_limit_bytes=VMEM_LIMIT),
    )(flat(lam_re), flat(lam_im), ldt, b_t(b_re), b_t(b_im), c_t(c_re), c_t(c_im))
    wb, wct = outs[0], outs[1]
    wc = jnp.transpose(wct, (0, 2, 1))
    return (wb, wc) + tuple(outs[2:])


def _gelu_tanh(x):
    return 0.5 * x * (1.0 + jnp.tanh(math.sqrt(2.0 / math.pi) * (x + 0.044715 * (x * x * x))))


def _even_kernel(p_ref, band_ref, wpool_ref, pscale_ref, wb_ref, wc_ref, dskip_ref, wglu_ref, bglu_ref,
                 as_re_ref, as_im_ref, pw_re_ref, pw_im_ref, a8_re_ref, a8_im_ref,
                 o_ref, prev_s, xr_s, xi_s, car_re, car_im, *, mix_a, scan_lanes):
    i = pl.program_id(1)
    tt = p_ref.shape[0]

    @pl.when(i == 0)
    def _():
        prev_s[...] = jnp.zeros_like(prev_s)
        car_re[...] = jnp.zeros_like(car_re)
        car_im[...] = jnp.zeros_like(car_im)

    ua = p_ref[:, :mix_a]
    ext = jnp.concatenate([prev_s[...], ua], axis=0)
    t1 = i * tt + lax.broadcasted_iota(jnp.int32, (tt, POOL_CH), 0) + 1
    for g, win in enumerate(POOL_WINDOWS):
        sl = slice(g * POOL_CH, (g + 1) * POOL_CH)
        wsum = _dot(band_ref[g], ext[:, sl])
        cnt = jnp.minimum(t1, win).astype(F32)
        pg = wsum / cnt - ua[:, sl].astype(F32)
        yg = _dot(pg.astype(BF16), wpool_ref[g]) * pscale_ref[:, sl]
        o_ref[:, sl] = yg.astype(o_ref.dtype)
    prev_s[...] = ua

    ub = p_ref[:, mix_a:]
    xr_s[...] = _dot(ub, wb_ref[0])
    xi_s[...] = _dot(ub, wb_ref[1])
    n_state = xr_s.shape[1]
    for c in range(n_state // scan_lanes):
        sl = slice(c * scan_lanes, (c + 1) * scan_lanes)

        def block(r, carry, sl=sl):
            cr, ci = carry
            rows = pl.ds(pl.multiple_of(r * SUBLANES, SUBLANES), SUBLANES)
            hr, hi = xr_s[rows, sl], xi_s[rows, sl]
            for k in range(3):
                sr = pltpu.roll(hr, 1 << k, 0)
                si = pltpu.roll(hi, 1 << k, 0)
                dr, di = _cmul(as_re_ref[k, :, sl], as_im_ref[k, :, sl], sr, si)
                hr, hi = hr + dr, hi + di
            lr = jnp.broadcast_to(hr[SUBLANES - 1:SUBLANES, :], hr.shape)
            li = jnp.broadcast_to(hi[SUBLANES - 1:SUBLANES, :], hi.shape)
            dr, di = _cmul(pw_re_ref[:, sl], pw_im_ref[:, sl], cr, ci)
            xr_s[rows, sl] = hr + dr
            xi_s[rows, sl] = hi + di
            dr, di = _cmul(a8_re_ref[:, sl], a8_im_ref[:, sl], cr, ci)
            return lr + dr, li + di

        cr, ci = lax.fori_loop(0, tt // SUBLANES, block, (car_re[:, sl], car_im[:, sl]))
        car_re[:, sl] = cr
        car_im[:, sl] = ci

    y = _dot(xr_s[...].astype(BF16), wc_ref[0]) - _dot(xi_s[...].astype(BF16), wc_ref[1])
    y = _gelu_tanh(y + dskip_ref[...] * ub.astype(F32))
    z = _dot(y.astype(BF16), wglu_ref[...]) + bglu_ref[...]
    o_ref[:, mix_a:] = (y * (1.0 / (1.0 + jnp.exp(-z)))).astype(o_ref.dtype)


def _pool_band(tt):
    t = jnp.arange(tt)[:, None] + tt
    j = jnp.arange(2 * tt)[None, :]
    return jnp.stack([((j <= t) & (j > t - w)) for w in POOL_WINDOWS]).astype(BF16)


def _even_mixer(proj, wpool, pscale, s5, dskip, wglu, bglu, tt, scan_lanes):
    bsz, seq, width = proj.shape
    wb, wc, as_re, as_im, pw_re, pw_im, a8_re, a8_im = s5
    mix_a = len(POOL_WINDOWS) * POOL_CH
    mix_b = width - mix_a
    n_state = wb.shape[2]
    const = lambda a: pl.BlockSpec(a.shape, lambda b, i, nd=a.ndim: (0,) * nd)
    band = _pool_band(tt)
    args = (band, wpool, pscale, wb, wc, dskip, wglu, bglu, as_re, as_im, pw_re, pw_im, a8_re, a8_im)
    return pl.pallas_call(
        functools.partial(_even_kernel, mix_a=mix_a, scan_lanes=scan_lanes),
        grid=(bsz, seq // tt),
        in_specs=[pl.BlockSpec((None, tt, width), lambda b, i: (b, i, 0))] + [const(a) for a in args],
        out_specs=pl.BlockSpec((None, tt, width), lambda b, i: (b, i, 0)),
        out_shape=jax.ShapeDtypeStruct((bsz, seq, width), BF16),
        scratch_shapes=[pltpu.VMEM((tt, mix_a), BF16),
                        pltpu.VMEM((tt, n_state), F32), pltpu.VMEM((tt, n_state), F32),
                        pltpu.VMEM((SUBLANES, n_state), F32), pltpu.VMEM((SUBLANES, n_state), F32)],
        compiler_params=_cparams(("parallel", "arbitrary")),
    )(proj, *args)


def _dsa_kernel(q_ref, qi_ref, wq_ref, k_ref, v_ref, kiw_ref, tri_ref, o_ref, s_s, m_s, *, topk):
    i = pl.program_id(1)
    qb, kb_size = ATT_BLOCK, ATT_BLOCK
    nkb = i + 1
    shape = (qb, kb_size)
    row = lax.broadcasted_iota(jnp.int32, shape, 0)
    col = lax.broadcasted_iota(jnp.int32, shape, 1)
    q_chunk = (i * qb + row) // CHUNK
    bcast = lambda a: jnp.broadcast_to(a, shape)
    kf = float(topk)

    wq = wq_ref[...]
    w_cols = [bcast(wq[:, IDX_DIM + h:IDX_DIM + h + 1] * IDX_SCALE) for h in range(IDX_HEADS)]

    def score_block(kb, carry):
        rmax, rmin = carry
        ki = kiw_ref[kb, :, :IDX_DIM].astype(BF16)
        acc = jnp.zeros(shape, F32)
        for h in range(IDX_HEADS):
            rel = jnp.maximum(_dot_nt(qi_ref[:, h * IDX_DIM:(h + 1) * IDX_DIM], ki), 0.0)
            acc = acc + rel * w_cols[h]
        adm = ((kb * kb_size + col) // CHUNK) <= q_chunk
        s_s[kb] = jnp.where(adm, acc, -jnp.inf)
        return jnp.maximum(rmax, jnp.where(adm, acc, -jnp.inf)), jnp.minimum(rmin, jnp.where(adm, acc, jnp.inf))

    rmax, rmin = lax.fori_loop(0, nkb, score_block,
                               (jnp.full(shape, -jnp.inf, F32), jnp.full(shape, jnp.inf, F32)))
    smax, smin = _rowmax(rmax), jnp.min(rmin, axis=1, keepdims=True)

    def count_ge(x):
        xb = bcast(x)
        cnt = lax.fori_loop(0, nkb, lambda kb, a: a + jnp.where(s_s[kb] >= xb, 1.0, 0.0),
                            jnp.zeros(shape, F32))
        return _rowsum(cnt)

    n_adm = ((q_chunk[:, :1] + 1) * CHUNK).astype(F32)
    lo0 = smin
    hi0 = 2.0 * jnp.maximum(jnp.abs(smax), jnp.abs(smin)) + 1.0

    def bisect(_, carry):
        lo, hi, clo, chi = carry
        mid = 0.5 * (lo + hi)
        c = count_ge(mid)
        ge = c >= kf
        return (jnp.where(ge, mid, lo), jnp.where(ge, hi, mid), jnp.where(ge, c, clo), jnp.where(ge, chi, c))

    lo, hi, clo, chi = lax.fori_loop(0, BISECT_ITERS, bisect, (lo0, hi0, n_adm, jnp.zeros_like(lo0)))

    done0 = jnp.where(clo <= kf, 1.0, 0.0)

    def pop_cond(carry):
        return jnp.logical_and(carry[0] > 0.0, carry[1] < MAX_POPS)

    def pop(carry):
        _, it, hi, chi, thr, need, done = carry
        hib = bcast(hi)
        v = _rowmax(lax.fori_loop(0, nkb, lambda kb, a: jnp.maximum(a, jnp.where(s_s[kb] < hib, s_s[kb], -jnp.inf)),
                                  jnp.full(shape, -jnp.inf, F32)))
        vb = bcast(v)
        cv = _rowsum(lax.fori_loop(0, nkb, lambda kb, a: a + jnp.where(s_s[kb] == vb, 1.0, 0.0),
                                   jnp.zeros(shape, F32)))
        active = done < 0.5
        fin = jnp.logical_and(active, chi + cv >= kf)
        go = jnp.logical_and(active, chi + cv < kf)
        thr = jnp.where(fin, v, thr)
        need = jnp.where(fin, kf - chi, need)
        done = jnp.where(fin, 1.0, done)
        hi = jnp.where(go, v, hi)
        chi = jnp.where(go, chi + cv, chi)
        return (jnp.max(1.0 - done), it + 1, hi, chi, thr, need, done)

    left0 = jnp.max(1.0 - done0)
    _, _, _, _, thr, need, _ = lax.while_loop(
        pop_cond, pop, (left0, jnp.int32(0), hi, chi, lo, jnp.full_like(lo, COUNT_ALL), done0))
    thrb, needb = bcast(thr), bcast(need)

    @pl.when(left0 > 0.0)
    def _():
        def tie_block(kb, run):
            s = s_s[kb]
            eq = jnp.where(s == thrb, 1.0, 0.0)
            rank = _dot(eq.astype(BF16), tri_ref[1]) + run
            take = jnp.where(s == thrb, jnp.where(rank < needb, 0.0, NEG_BIG), NEG_BIG)
            m_s[kb] = jnp.where(s > thrb, 0.0, take)
            return run + bcast(_rowsum(eq))
        lax.fori_loop(0, nkb, tie_block, jnp.zeros(shape, F32))

    @pl.when(left0 <= 0.0)
    def _():
        def sel_block(kb, carry):
            m_s[kb] = jnp.where(s_s[kb] >= thrb, 0.0, NEG_BIG)
            return carry
        lax.fori_loop(0, nkb, sel_block, 0)

    rel_pos = (row - col).astype(F32)
    for h in range(N_HEADS):
        hs = slice(h * HEAD_DIM, (h + 1) * HEAD_DIM)
        slope = 2.0 ** (-8.0 * (h + 1) / N_HEADS)
        q_h = q_ref[:, hs] * (HEAD_DIM ** -0.5)

        def attend(kb, carry, hs=hs, slope=slope, q_h=q_h):
            m, l, acc = carry
            dist = jnp.abs(rel_pos + ((i - kb) * kb_size).astype(F32))
            s = _dot_nt(q_h, k_ref[kb, :, hs]) - slope * dist + m_s[kb]
            m_new = jnp.maximum(m, _rowmax(s))
            p = jnp.exp(s - m_new)
            alpha = jnp.exp(m - m_new)
            l = alpha * l + _rowsum(p)
            acc = alpha * acc + _dot(p.astype(BF16), v_ref[kb, :, hs])
            return m_new, l, acc

        m, l, acc = lax.fori_loop(
            0, nkb, attend,
            (jnp.full((qb, 1), NEG_BIG, F32), jnp.zeros((qb, 1), F32), jnp.zeros((qb, HEAD_DIM), F32)))
        o_ref[:, hs] = (acc / l).astype(o_ref.dtype)


def _tri_consts(n):
    j = jnp.arange(n)[:, None]
    s = jnp.arange(n)[None, :]
    return jnp.stack([j > s, j < s]).astype(BF16)


def _dsa(q, k, v, qi, kiw, topk):
    bsz, seq, width = q.shape
    nb = seq // ATT_BLOCK
    blk = lambda a: a.reshape(bsz, nb, ATT_BLOCK, a.shape[-1])
    qspec = lambda wd: pl.BlockSpec((None, None, ATT_BLOCK, wd), lambda b, i: (b, i, 0, 0))
    kspec = lambda wd: pl.BlockSpec((None, nb, ATT_BLOCK, wd), lambda b, i: (b, 0, 0, 0))
    tri = _tri_consts(ATT_BLOCK)
    out = pl.pallas_call(
        functools.partial(_dsa_kernel, topk=topk),
        grid=(bsz, nb),
        in_specs=[qspec(width), qspec(width), qspec(kiw.shape[-1]),
                  kspec(width), kspec(width), kspec(kiw.shape[-1]),
                  pl.BlockSpec(tri.shape, lambda b, i: (0, 0, 0))],
        out_specs=qspec(width),
        out_shape=jax.ShapeDtypeStruct((bsz, nb, ATT_BLOCK, width), BF16),
        scratch_shapes=[pltpu.VMEM((nb, ATT_BLOCK, ATT_BLOCK), F32), pltpu.VMEM((nb, ATT_BLOCK, ATT_BLOCK), F32)],
        compiler_params=_cparams(("parallel", "arbitrary")),
    )(blk(q), blk(qi), blk(kiw), blk(k), blk(v), blk(kiw), tri)
    return out.reshape(bsz, seq, width)


def _sb_kernel(q_ref, k_ref, v_ref, tri_ref, o_ref):
    i = pl.program_id(1)
    qb = kb_size = ATT_BLOCK
    shape = (qb, kb_size)
    row = lax.broadcasted_iota(jnp.int32, shape, 0)
    col = lax.broadcasted_iota(jnp.int32, shape, 1)
    for h in range(N_HEADS):
        hs = slice(h * HEAD_DIM, (h + 1) * HEAD_DIM)
        q_h = q_ref[:, hs] * (HEAD_DIM ** -0.5)

        def step(j, carry, hs=hs, q_h=q_h):
            run, acc = carry
            kb = i - j
            z = _dot_nt(q_h, k_ref[kb, :, hs])
            sp = jnp.maximum(z, 0.0) + jnp.log1p(jnp.exp(-jnp.abs(z)))
            causal = (kb * kb_size + col) < (i * qb + row)
            log_keep = jnp.where(causal, -sp, 0.0)
            after = _dot(log_keep.astype(BF16), tri_ref[0])
            w = jnp.where(causal, jnp.exp(z - sp + after + run), 0.0)
            acc = acc + _dot(w.astype(BF16), v_ref[kb, :, hs])
            return run + _rowsum(log_keep), acc

        _, acc = lax.fori_loop(0, i + 1, step, (jnp.zeros((qb, 1), F32), jnp.zeros((qb, HEAD_DIM), F32)))
        o_ref[:, hs] = acc.astype(o_ref.dtype)


def _stick_breaking(q, k, v):
    bsz, seq, width = q.shape
    nb = seq // ATT_BLOCK
    blk = lambda a: a.reshape(bsz, nb, ATT_BLOCK, a.shape[-1])
    qspec = pl.BlockSpec((None, None, ATT_BLOCK, width), lambda b, i: (b, i, 0, 0))
    kspec = pl.BlockSpec((None, nb, ATT_BLOCK, width), lambda b, i: (b, 0, 0, 0))
    tri = _tri_consts(ATT_BLOCK)
    out = pl.pallas_call(
        _sb_kernel,
        grid=(bsz, nb),
        in_specs=[qspec, kspec, kspec, pl.BlockSpec(tri.shape, lambda b, i: (0, 0, 0))],
        out_specs=qspec,
        out_shape=jax.ShapeDtypeStruct((bsz, nb, ATT_BLOCK, width), BF16),
        compiler_params=_cparams(("parallel", "arbitrary")),
    )(blk(q), blk(k), blk(v), tri)
    return out.reshape(bsz, seq, width)


def _cd_weight(cd_w_in):
    cw = N_HEADS * HEAD_DIM
    iw = IDX_HEADS * IDX_DIM
    offs = [0, cw, 2 * cw, 3 * cw, 3 * cw + iw, 3 * cw + iw + IDX_DIM, 3 * cw + iw + IDX_DIM + IDX_HEADS]
    qkv_c = cd_w_in[:, :offs[3]]
    q_idx = cd_w_in[:, offs[3]:offs[4]]
    kw_idx = cd_w_in[:, offs[4]:offs[6]]
    qkv_d = cd_w_in[:, offs[6]:]
    pad = jnp.zeros((cd_w_in.shape[0], LANES - kw_idx.shape[1]), cd_w_in.dtype)
    return jnp.concatenate([qkv_c, q_idx, qkv_d, kw_idx, pad], axis=1)


def kernel(x, c, norm_g, ada_w, ada_b, mlp_w1, mlp_w2, ab_w_in, ab_w_out, pool_w, pool_scale, ssm_lam_re, ssm_lam_im, ssm_log_dt, ssm_b_re, ssm_b_im, ssm_c_re, ssm_c_im, ssm_d, glu_w, glu_b, cd_w_in, cd_w_out, final_g):
    bsz, seq, d = x.shape
    depth = norm_g.shape[0]
    topk = min(TOPK_MAX, seq // 4)
    tm = min(512, seq)

    c_pad = jnp.zeros((SUBLANES, d), F32).at[:bsz].set(c)
    mod = _modulation(c_pad, ada_w.reshape(depth * 2, d, 3 * d), ada_b.reshape(depth * 2, 1, 3 * d))
    mod = mod[:, :, :bsz, None, :]

    for i in range(depth):
        j = i // 2
        shift, scale1, gate1 = mod[2 * i, 0], mod[2 * i, 1], mod[2 * i, 2]
        g = norm_g[i, 0][None, :]
        if i % 2 == 0:
            width = ab_w_in.shape[2]
            (proj,) = _norm_proj(x, g, shift, scale1, ab_w_in[j].astype(BF16), (width,), (BF16,), tm)
            s5 = _s5_params(ssm_lam_re[j], ssm_lam_im[j], ssm_log_dt[j],
                            ssm_b_re[j], ssm_b_im[j], ssm_c_re[j], ssm_c_im[j])
            y = _even_mixer(proj, pool_w[j].astype(BF16), pool_scale[j][None, :], s5,
                            ssm_d[j][None, :], glu_w[j].astype(BF16), glu_b[j][None, :],
                            tt=min(256, seq), scan_lanes=512)
            x = _out_proj(x, gate1, (y,), ab_w_out[j].astype(BF16), tm)
        else:
            hw = N_HEADS * HEAD_DIM
            widths = (hw,) * 7 + (LANES,)
            dtypes = (BF16,) * 7 + (F32,)
            qc, kc, vc, qi, qd, kd, vd, kiw = _norm_proj(
                x, g, shift, scale1, _cd_weight(cd_w_in[j]).astype(BF16), widths, dtypes, tm)
            oc = _dsa(qc, kc, vc, qi, kiw, topk)
            od = _stick_breaking(qd, kd, vd)
            x = _out_proj(x, gate1, (oc, od), cd_w_out[j].astype(BF16), tm)
        shift, scale1, gate1 = mod[2 * i + 1, 0], mod[2 * i + 1, 1], mod[2 * i + 1, 2]
        x = _mlp(x, norm_g[i, 1][None, :], shift, scale1, gate1,
                 mlp_w1[i].astype(BF16), mlp_w2[i].astype(BF16), final_g[None, :],
                 final_norm=(i == depth - 1), tm=min(1024, seq), tf=512)
    return x
```

```python
import functools
import math

import jax
import jax.numpy as jnp
from jax import lax
from jax.experimental import pallas as pl
from jax.experimental.pallas import tpu as pltpu

F32 = jnp.float32
BF16 = jnp.bfloat16

EPS = 1e-6
CHUNK = 64
POOL_WINDOWS = (2, 4, 8, 16)
POOL_CH = 128
SSM_GROUP = 16
SSM_STATE = 64
HEAD_DIM = 64
N_HEADS = 8
IDX_HEADS = 8
IDX_DIM = 64
IDX_SCALE = (IDX_HEADS ** -0.5) * (IDX_DIM ** -0.5)
TOPK_MAX = 256

LANES = 128
SUBLANES = 8
VMEM_LIMIT = 56 * 1024 * 1024

ATT_BLOCK = 128
ATT_QUERY_BLOCK = 256
NEG_BIG = -1e30
COUNT_ALL = 1e9
BISECT_ITERS = 26
MAX_POPS = 4096


def _cparams(sem):
    return pltpu.CompilerParams(dimension_semantics=sem, vmem_limit_bytes=VMEM_LIMIT)


def _dot(a, b):
    return jnp.dot(a, b, preferred_element_type=F32)


def _dot_nt(a, b):
    return lax.dot_general(a, b, (((1,), (1,)), ((), ())), preferred_element_type=F32)


def _rowsum(x):
    return jnp.sum(x, axis=1, keepdims=True)


def _rowmax(x):
    return jnp.max(x, axis=1, keepdims=True)


def _norm_modulate(x, g, shift, scale1):
    ms = jnp.mean(x * x, axis=-1, keepdims=True)
    return (x * lax.rsqrt(ms + EPS)) * g * scale1 + shift


def _mod_kernel(c_ref, w_ref, b_ref, o_ref):
    j = pl.program_id(1)
    m = _dot(c_ref[...], w_ref[...]) + b_ref[...]
    o_ref[...] = m + jnp.where(j > 0, 1.0, 0.0)


def _modulation(c_pad, ada_w, ada_b):
    n_sub, d, _ = ada_w.shape
    rows = c_pad.shape[0]
    return pl.pallas_call(
        _mod_kernel,
        grid=(n_sub, 3),
        in_specs=[
            pl.BlockSpec((rows, d), lambda s, j: (0, 0)),
            pl.BlockSpec((None, d, d), lambda s, j: (s, 0, j)),
            pl.BlockSpec((None, 1, d), lambda s, j: (s, 0, j)),
        ],
        out_specs=pl.BlockSpec((None, None, rows, d), lambda s, j: (s, j, 0, 0)),
        out_shape=jax.ShapeDtypeStruct((n_sub, 3, rows, d), F32),
        compiler_params=_cparams(("arbitrary", "arbitrary")),
    )(c_pad, ada_w, ada_b)


def _norm_proj_kernel(x_ref, g_ref, shift_ref, scale_ref, w_ref, *o_refs, widths):
    h = _norm_modulate(x_ref[...], g_ref[...], shift_ref[...], scale_ref[...]).astype(BF16)
    y = _dot(h, w_ref[...])
    off = 0
    for o_ref, width in zip(o_refs, widths):
        o_ref[...] = y[:, off:off + width].astype(o_ref.dtype)
        off += width


def _norm_proj(x, g, shift, scale1, w, widths, dtypes, tm):
    bsz, seq, d = x.shape
    n = w.shape[1]
    vec = pl.BlockSpec((None, 1, d), lambda b, i: (b, 0, 0))
    return pl.pallas_call(
        functools.partial(_norm_proj_kernel, widths=widths),
        grid=(bsz, seq // tm),
        in_specs=[
            pl.BlockSpec((None, tm, d), lambda b, i: (b, i, 0)),
            pl.BlockSpec((1, d), lambda b, i: (0, 0)),
            vec, vec,
            pl.BlockSpec((d, n), lambda b, i: (0, 0)),
        ],
        out_specs=[pl.BlockSpec((None, tm, wd), lambda b, i: (b, i, 0)) for wd in widths],
        out_shape=[jax.ShapeDtypeStruct((bsz, seq, wd), dt) for wd, dt in zip(widths, dtypes)],
        compiler_params=_cparams(("parallel", "parallel")),
    )(x, g, shift, scale1, w)


def _out_proj_kernel(x_ref, gate_ref, *refs, n_y):
    y_refs, w_ref, o_ref = refs[:n_y], refs[n_y], refs[n_y + 1]
    y = jnp.concatenate([r[...] for r in y_refs], axis=1) if n_y > 1 else y_refs[0][...]
    o_ref[...] = x_ref[...] + gate_ref[...] * _dot(y, w_ref[...])


def _out_proj(x, gate1, ys, w, tm):
    bsz, seq, d = x.shape
    k = w.shape[0]
    return pl.pallas_call(
        functools.partial(_out_proj_kernel, n_y=len(ys)),
        grid=(bsz, seq // tm),
        in_specs=[
            pl.BlockSpec((None, tm, d), lambda b, i: (b, i, 0)),
            pl.BlockSpec((None, 1, d), lambda b, i: (b, 0, 0)),
            *[pl.BlockSpec((None, tm, y.shape[2]), lambda b, i: (b, i, 0)) for y in ys],
            pl.BlockSpec((k, d), lambda b, i: (0, 0)),
        ],
        out_specs=pl.BlockSpec((None, tm, d), lambda b, i: (b, i, 0)),
        out_shape=jax.ShapeDtypeStruct((bsz, seq, d), F32),
        compiler_params=_cparams(("parallel", "parallel")),
    )(x, gate1, *ys, w)


def _mlp_kernel(x_ref, g_ref, shift_ref, scale_ref, gate_ref, w1_ref, w2_ref, fg_ref, o_ref,
                h_s, acc_s, *, final_norm):
    f = pl.program_id(2)

    @pl.when(f == 0)
    def _():
        h_s[...] = _norm_modulate(x_ref[...], g_ref[...], shift_ref[...], scale_ref[...]).astype(BF16)
        acc_s[...] = jnp.zeros_like(acc_s)

    a = jnp.maximum(_dot(h_s[...], w1_ref[...]), 0.0)
    acc_s[...] += _dot((a * a).astype(BF16), w2_ref[...])

    @pl.when(f == pl.num_programs(2) - 1)
    def _():
        y = x_ref[...] + gate_ref[...] * acc_s[...]
        if final_norm:
            ms = jnp.mean(y * y, axis=-1, keepdims=True)
            y = (y * lax.rsqrt(ms + EPS)) * fg_ref[...]
        o_ref[...] = y


def _mlp(x, g, shift, scale1, gate1, w1, w2, final_g, final_norm, tm, tf):
    bsz, seq, d = x.shape
    dff = w1.shape[1]
    vec = pl.BlockSpec((None, 1, d), lambda b, i, f: (b, 0, 0))
    one = pl.BlockSpec((1, d), lambda b, i, f: (0, 0))
    return pl.pallas_call(
        functools.partial(_mlp_kernel, final_norm=final_norm),
        grid=(bsz, seq // tm, dff // tf),
        in_specs=[
            pl.BlockSpec((None, tm, d), lambda b, i, f: (b, i, 0)),
            one, vec, vec, vec,
            pl.BlockSpec((d, tf), lambda b, i, f: (0, f)),
            pl.BlockSpec((tf, d), lambda b, i, f: (f, 0)),
            one,
        ],
        out_specs=pl.BlockSpec((None, tm, d), lambda b, i, f: (b, i, 0)),
        out_shape=jax.ShapeDtypeStruct((bsz, seq, d), F32),
        scratch_shapes=[pltpu.VMEM((tm, d), BF16), pltpu.VMEM((tm, d), F32)],
        compiler_params=_cparams(("parallel", "parallel", "arbitrary")),
    )(x, g, shift, scale1, gate1, w1, w2, final_g)


def _cmul(ar, ai, br, bi):
    return ar * br - ai * bi, ar * bi + ai * br


def _s5_params_kernel(lr_ref, li_ref, ldt_ref, brt_ref, bit_ref, crt_ref, cit_ref,
                      wb_ref, wc_ref, as_re_ref, as_im_ref, pw_re_ref, pw_im_ref, a8_re_ref, a8_im_ref):
    lr, li = lr_ref[...], li_ref[...]
    dt = jnp.exp(ldt_ref[...])
    mag = jnp.exp(lr * dt)
    ar, ai = mag * jnp.cos(li * dt), mag * jnp.sin(li * dt)
    den = lr * lr + li * li
    zr = ((ar - 1.0) * lr + ai * li) / den
    zi = (ai * lr - (ar - 1.0) * li) / den

    rows, cols = brt_ref.shape
    rgrp = lax.broadcasted_iota(jnp.int32, (rows, cols), 0) // SSM_GROUP
    cgrp = lax.broadcasted_iota(jnp.int32, (rows, cols), 1) // SSM_STATE
    diag = rgrp == cgrp
    br, bi = brt_ref[...], bit_ref[...]
    wb_ref[0] = jnp.where(diag, zr * br - zi * bi, 0.0).astype(wb_ref.dtype)
    wb_ref[1] = jnp.where(diag, zr * bi + zi * br, 0.0).astype(wb_ref.dtype)
    wc_ref[0] = jnp.where(diag, crt_ref[...], 0.0).astype(wc_ref.dtype)
    wc_ref[1] = jnp.where(diag, cit_ref[...], 0.0).astype(wc_ref.dtype)

    n = lr.shape[1]
    row = lax.broadcasted_iota(jnp.int32, (SUBLANES, n), 0)
    pr, pi = jnp.broadcast_to(ar, (SUBLANES, n)), jnp.broadcast_to(ai, (SUBLANES, n))
    a1r, a1i = pr, pi
    for k in range(3):
        keep = row >= (1 << k)
        as_re_ref[k] = jnp.where(keep, pr, 0.0)
        as_im_ref[k] = jnp.where(keep, pi, 0.0)
        pr, pi = _cmul(pr, pi, pr, pi)
    a8_re_ref[...] = pr
    a8_im_ref[...] = pi
    qr, qi = a1r, a1i
    for r in range(1, SUBLANES):
        nr, ni = _cmul(qr, qi, a1r, a1i)
        qr = jnp.where(row >= r, nr, qr)
        qi = jnp.where(row >= r, ni, qi)
    pw_re_ref[...] = qr
    pw_im_ref[...] = qi


def _s5_params(lam_re, lam_im, log_dt, b_re, b_im, c_re, c_im):
    g, p = lam_re.shape
    c = b_re.shape[2]
    n = g * p
    flat = lambda a: a.reshape(1, n)
    ldt = jnp.broadcast_to(log_dt[:, None], (g, p)).reshape(1, n)
    b_t = lambda a: jnp.tile(jnp.transpose(a, (2, 0, 1)).reshape(c, n), (g, 1))
    c_t = lambda a: jnp.tile(a.reshape(g * c, p), (1, g))
    outs = pl.pallas_call(
        _s5_params_kernel,
        out_shape=[jax.ShapeDtypeStruct((2, g * c, n), BF16), jax.ShapeDtypeStruct((2, g * c, n), BF16),
                   jax.ShapeDtypeStruct((3, SUBLANES, n), F32), jax.ShapeDtypeStruct((3, SUBLANES, n), F32),
                   jax.ShapeDtypeStruct((SUBLANES, n), F32), jax.ShapeDtypeStruct((SUBLANES, n), F32),
                   jax.ShapeDtypeStruct((SUBLANES, n), F32), jax.ShapeDtypeStruct((SUBLANES, n), F32)],
        compiler_params=pltpu.CompilerParams(vmem_limit_bytes=VMEM_LIMIT),
    )(flat(lam_re), flat(lam_im), ldt, b_t(b_re), b_t(b_im), c_t(c_re), c_t(c_im))
    wb, wct = outs[0], outs[1]
    wc = jnp.transpose(wct, (0, 2, 1))
    return (wb, wc) + tuple(outs[2:])


def _gelu_tanh(x):
    return 0.5 * x * (1.0 + jnp.tanh(math.sqrt(2.0 / math.pi) * (x + 0.044715 * (x * x * x))))


def _even_kernel(p_ref, band_ref, wpool_ref, pscale_ref, wb_ref, wc_ref, dskip_ref, wglu_ref, bglu_ref,
                 as_re_ref, as_im_ref, pw_re_ref, pw_im_ref, a8_re_ref, a8_im_ref,
                 o_ref, prev_s, xr_s, xi_s, car_re, car_im, *, mix_a, scan_lanes):
    i = pl.program_id(1)
    tt = p_ref.shape[0]

    @pl.when(i == 0)
    def _():
        prev_s[...] = jnp.zeros_like(prev_s)
        car_re[...] = jnp.zeros_like(car_re)
        car_im[...] = jnp.zeros_like(car_im)

    ua = p_ref[:, :mix_a]
    ext = jnp.concatenate([prev_s[...], ua], axis=0)
    t1 = i * tt + lax.broadcasted_iota(jnp.int32, (tt, POOL_CH), 0) + 1
    for g, win in enumerate(POOL_WINDOWS):
        sl = slice(g * POOL_CH, (g + 1) * POOL_CH)
        wsum = _dot(band_ref[g], ext[:, sl])
        cnt = jnp.minimum(t1, win).astype(F32)
        pg = wsum / cnt - ua[:, sl].astype(F32)
        yg = _dot(pg.astype(BF16), wpool_ref[g]) * pscale_ref[:, sl]
        o_ref[:, sl] = yg.astype(o_ref.dtype)
    prev_s[...] = ua

    ub = p_ref[:, mix_a:]
    xr_s[...] = _dot(ub, wb_ref[0])
    xi_s[...] = _dot(ub, wb_ref[1])
    n_state = xr_s.shape[1]
    for c in range(n_state // scan_lanes):
        sl = slice(c * scan_lanes, (c + 1) * scan_lanes)

        def block(r, carry, sl=sl):
            cr, ci = carry
            rows = pl.ds(pl.multiple_of(r * SUBLANES, SUBLANES), SUBLANES)
            hr, hi = xr_s[rows, sl], xi_s[rows, sl]
            for k in range(3):
                sr = pltpu.roll(hr, 1 << k, 0)
                si = pltpu.roll(hi, 1 << k, 0)
                dr, di = _cmul(as_re_ref[k, :, sl], as_im_ref[k, :, sl], sr, si)
                hr, hi = hr + dr, hi + di
            lr = jnp.broadcast_to(hr[SUBLANES - 1:SUBLANES, :], hr.shape)
            li = jnp.broadcast_to(hi[SUBLANES - 1:SUBLANES, :], hi.shape)
            dr, di = _cmul(pw_re_ref[:, sl], pw_im_ref[:, sl], cr, ci)
            xr_s[rows, sl] = hr + dr
            xi_s[rows, sl] = hi + di
            dr, di = _cmul(a8_re_ref[:, sl], a8_im_ref[:, sl], cr, ci)
            return lr + dr, li + di

        cr, ci = lax.fori_loop(0, tt // SUBLANES, block, (car_re[:, sl], car_im[:, sl]))
        car_re[:, sl] = cr
        car_im[:, sl] = ci

    y = _dot(xr_s[...].astype(BF16), wc_ref[0]) - _dot(xi_s[...].astype(BF16), wc_ref[1])
    y = _gelu_tanh(y + dskip_ref[...] * ub.astype(F32))
    z = _dot(y.astype(BF16), wglu_ref[...]) + bglu_ref[...]
    o_ref[:, mix_a:] = (y * (1.0 / (1.0 + jnp.exp(-z)))).astype(o_ref.dtype)


def _pool_band(tt):
    t = jnp.arange(tt)[:, None] + tt
    j = jnp.arange(2 * tt)[None, :]
    return jnp.stack([((j <= t) & (j > t - w)) for w in POOL_WINDOWS]).astype(BF16)


def _even_mixer(proj, wpool, pscale, s5, dskip, wglu, bglu, tt, scan_lanes):
    bsz, seq, width = proj.shape
    wb, wc, as_re, as_im, pw_re, pw_im, a8_re, a8_im = s5
    mix_a = len(POOL_WINDOWS) * POOL_CH
    mix_b = width - mix_a
    n_state = wb.shape[2]
    const = lambda a: pl.BlockSpec(a.shape, lambda b, i, nd=a.ndim: (0,) * nd)
    band = _pool_band(tt)
    args = (band, wpool, pscale, wb, wc, dskip, wglu, bglu, as_re, as_im, pw_re, pw_im, a8_re, a8_im)
    return pl.pallas_call(
        functools.partial(_even_kernel, mix_a=mix_a, scan_lanes=scan_lanes),
        grid=(bsz, seq // tt),
        in_specs=[pl.BlockSpec((None, tt, width), lambda b, i: (b, i, 0))] + [const(a) for a in args],
        out_specs=pl.BlockSpec((None, tt, width), lambda b, i: (b, i, 0)),
        out_shape=jax.ShapeDtypeStruct((bsz, seq, width), BF16),
        scratch_shapes=[pltpu.VMEM((tt, mix_a), BF16),
                        pltpu.VMEM((tt, n_state), F32), pltpu.VMEM((tt, n_state), F32),
                        pltpu.VMEM((SUBLANES, n_state), F32), pltpu.VMEM((SUBLANES, n_state), F32)],
        compiler_params=_cparams(("parallel", "arbitrary")),
    )(proj, *args)


def _dsa_kernel(q_ref, qi_ref, wq_ref, k_ref, v_ref, kiw_ref, tri_ref, o_ref,
                s_s, m_s, qs_s, mx_s, l_s, acc_s, *, topk, qb):
    i = pl.program_id(1)
    kb_size = ATT_BLOCK
    nkb = (i + 1) * (qb // kb_size)
    shape = (qb, kb_size)
    row = lax.broadcasted_iota(jnp.int32, shape, 0)
    col = lax.broadcasted_iota(jnp.int32, shape, 1)
    q_chunk = (i * qb + row) // CHUNK
    bcast = lambda a: jnp.broadcast_to(a, shape)
    kf = float(topk)

    wq = wq_ref[...]
    w_cols = [bcast(wq[:, IDX_DIM + h:IDX_DIM + h + 1] * IDX_SCALE) for h in range(IDX_HEADS)]

    def score_block(kb, carry):
        rmax, rmin = carry
        ki = kiw_ref[kb, :, :IDX_DIM].astype(BF16)
        acc = jnp.zeros(shape, F32)
        for h in range(IDX_HEADS):
            rel = jnp.maximum(_dot_nt(qi_ref[:, h * IDX_DIM:(h + 1) * IDX_DIM], ki), 0.0)
            acc = acc + rel * w_cols[h]
        adm = ((kb * kb_size + col) // CHUNK) <= q_chunk
        s_s[kb] = jnp.where(adm, acc, -jnp.inf)
        return jnp.maximum(rmax, jnp.where(adm, acc, -jnp.inf)), jnp.minimum(rmin, jnp.where(adm, acc, jnp.inf))

    rmax, rmin = lax.fori_loop(0, nkb, score_block,
                               (jnp.full(shape, -jnp.inf, F32), jnp.full(shape, jnp.inf, F32)))
    smax, smin = _rowmax(rmax), jnp.min(rmin, axis=1, keepdims=True)

    def count_ge(x):
        xb = bcast(x)
        cnt = lax.fori_loop(0, nkb, lambda kb, a: a + jnp.where(s_s[kb] >= xb, 1.0, 0.0),
                            jnp.zeros(shape, F32))
        return _rowsum(cnt)

    n_adm = ((q_chunk[:, :1] + 1) * CHUNK).astype(F32)
    lo0 = smin
    hi0 = 2.0 * jnp.maximum(jnp.abs(smax), jnp.abs(smin)) + 1.0

    def bisect(_, carry):
        lo, hi, clo, chi = carry
        mid = 0.5 * (lo + hi)
        c = count_ge(mid)
        ge = c >= kf
        return (jnp.where(ge, mid, lo), jnp.where(ge, hi, mid), jnp.where(ge, c, clo), jnp.where(ge, chi, c))

    lo, hi, clo, chi = lax.fori_loop(0, BISECT_ITERS, bisect, (lo0, hi0, n_adm, jnp.zeros_like(lo0)))

    done0 = jnp.where(clo <= kf, 1.0, 0.0)

    def pop_cond(carry):
        return jnp.logical_and(carry[0] > 0.0, carry[1] < MAX_POPS)

    def pop(carry):
        _, it, hi, chi, thr, need, done = carry
        hib = bcast(hi)
        v = _rowmax(lax.fori_loop(0, nkb, lambda kb, a: jnp.maximum(a, jnp.where(s_s[kb] < hib, s_s[kb], -jnp.inf)),
                                  jnp.full(shape, -jnp.inf, F32)))
        vb = bcast(v)
        cv = _rowsum(lax.fori_loop(0, nkb, lambda kb, a: a + jnp.where(s_s[kb] == vb, 1.0, 0.0),
                                   jnp.zeros(shape, F32)))
        active = done < 0.5
        fin = jnp.logical_and(active, chi + cv >= kf)
        go = jnp.logical_and(active, chi + cv < kf)
        thr = jnp.where(fin, v, thr)
        need = jnp.where(fin, kf - chi, need)
        done = jnp.where(fin, 1.0, done)
        hi = jnp.where(go, v, hi)
        chi = jnp.where(go, chi + cv, chi)
        return (jnp.max(1.0 - done), it + 1, hi, chi, thr, need, done)

    left0 = jnp.max(1.0 - done0)
    _, _, _, _, thr, need, _ = lax.while_loop(
        pop_cond, pop, (left0, jnp.int32(0), hi, chi, lo, jnp.full_like(lo, COUNT_ALL), done0))
    thrb, needb = bcast(thr), bcast(need)

    @pl.when(left0 > 0.0)
    def _():
        def tie_block(kb, run):
            s = s_s[kb]
            eq = jnp.where(s == thrb, 1.0, 0.0)
            rank = _dot(eq.astype(BF16), tri_ref[1]) + run
            take = jnp.where(s == thrb, jnp.where(rank < needb, 0.0, NEG_BIG), NEG_BIG)
            m_s[kb] = jnp.where(s > thrb, 0.0, take)
            return run + bcast(_rowsum(eq))
        lax.fori_loop(0, nkb, tie_block, jnp.zeros(shape, F32))

    @pl.when(left0 <= 0.0)
    def _():
        def sel_block(kb, carry):
            m_s[kb] = jnp.where(s_s[kb] >= thrb, 0.0, NEG_BIG)
            return carry
        lax.fori_loop(0, nkb, sel_block, 0)

    rel_pos = (row - col).astype(F32)
    qs_s[...] = q_ref[...] * (HEAD_DIM ** -0.5)
    mx_s[...] = jnp.full(mx_s.shape, NEG_BIG, F32)
    l_s[...] = jnp.zeros_like(l_s)
    acc_s[...] = jnp.zeros_like(acc_s)
    ones = tri_ref[2]

    def attend(kb, carry):
        dist = jnp.abs(rel_pos + (i * qb - kb * kb_size).astype(F32))
        mask = m_s[kb]
        heads = [slice(h * HEAD_DIM, (h + 1) * HEAD_DIM) for h in range(N_HEADS)]
        slopes = [2.0 ** (-8.0 * (h + 1) / N_HEADS) for h in range(N_HEADS)]
        scores = [_dot_nt(qs_s[:, hs], k_ref[kb, :, hs]) - slope * dist + mask for hs, slope in zip(heads, slopes)]
        for h, hs in enumerate(heads):
            s = scores[h]
            m_old = mx_s[h]
            m_new = jnp.maximum(m_old, bcast(_rowmax(s)))
            p = jnp.exp(s - m_new).astype(BF16)
            alpha = jnp.exp(m_old - m_new)
            mx_s[h] = m_new
            l_s[h] = alpha * l_s[h] + _dot(p, ones)
            acc_s[h] = alpha[:, :HEAD_DIM] * acc_s[h] + _dot(p, v_ref[kb, :, hs])
        return carry

    lax.fori_loop(0, nkb, attend, 0)
    for h in range(N_HEADS):
        o_ref[:, h * HEAD_DIM:(h + 1) * HEAD_DIM] = (acc_s[h] / l_s[h][:, :HEAD_DIM]).astype(o_ref.dtype)


def _tri_consts(n):
    j = jnp.arange(n)[:, None]
    s = jnp.arange(n)[None, :]
    return jnp.stack([j > s, j < s, jnp.ones((n, n), bool)]).astype(BF16)


def _dsa(q, k, v, qi, kiw, topk, qb):
    bsz, seq, width = q.shape
    nq, nk = seq // qb, seq // ATT_BLOCK
    qblk = lambda a: a.reshape(bsz, nq, qb, a.shape[-1])
    kblk = lambda a: a.reshape(bsz, nk, ATT_BLOCK, a.shape[-1])
    qspec = lambda wd: pl.BlockSpec((None, None, qb, wd), lambda b, i: (b, i, 0, 0))
    kspec = lambda wd: pl.BlockSpec((None, nk, ATT_BLOCK, wd), lambda b, i: (b, 0, 0, 0))
    tri = _tri_consts(ATT_BLOCK)
    out = pl.pallas_call(
        functools.partial(_dsa_kernel, topk=topk, qb=qb),
        grid=(bsz, nq),
        in_specs=[qspec(width), qspec(width), qspec(kiw.shape[-1]),
                  kspec(width), kspec(width), kspec(kiw.shape[-1]),
                  pl.BlockSpec(tri.shape, lambda b, i: (0, 0, 0))],
        out_specs=qspec(width),
        out_shape=jax.ShapeDtypeStruct((bsz, nq, qb, width), BF16),
        scratch_shapes=[pltpu.VMEM((nk, qb, ATT_BLOCK), F32), pltpu.VMEM((nk, qb, ATT_BLOCK), F32),
                        pltpu.VMEM((qb, width), BF16),
                        pltpu.VMEM((N_HEADS, qb, ATT_BLOCK), F32), pltpu.VMEM((N_HEADS, qb, ATT_BLOCK), F32),
                        pltpu.VMEM((N_HEADS, qb, HEAD_DIM), F32)],
        compiler_params=_cparams(("parallel", "arbitrary")),
    )(qblk(q), qblk(qi), qblk(kiw), kblk(k), kblk(v), kblk(kiw), tri)
    return out.reshape(bsz, seq, width)


def _sb_kernel(q_ref, k_ref, v_ref, tri_ref, o_ref, qs_s, run_s, acc_s, *, qb):
    i = pl.program_id(1)
    kb_size = ATT_BLOCK
    n_diag = qb // kb_size
    shape = (qb, kb_size)
    row = lax.broadcasted_iota(jnp.int32, shape, 0)
    col = lax.broadcasted_iota(jnp.int32, shape, 1)
    qs_s[...] = q_ref[...] * (HEAD_DIM ** -0.5)
    run_s[...] = jnp.zeros_like(run_s)
    acc_s[...] = jnp.zeros_like(acc_s)

    def step(kb, masked):
        causal = (kb * kb_size + col) < (i * qb + row)
        heads = [slice(h * HEAD_DIM, (h + 1) * HEAD_DIM) for h in range(N_HEADS)]
        log_sig, log_keep = [], []
        for hs in heads:
            z = _dot_nt(qs_s[:, hs], k_ref[kb, :, hs])
            sp = jnp.maximum(z, 0.0) + jnp.log(1.0 + jnp.exp(-jnp.abs(z)))
            log_sig.append(z - sp)
            log_keep.append((jnp.where(causal, -sp, 0.0) if masked else -sp).astype(BF16))
        for h, hs in enumerate(heads):
            sums = _dot(log_keep[h], tri_ref[...])
            w = jnp.exp(log_sig[h] + sums[:, :kb_size] + run_s[h])
            if masked:
                w = jnp.where(causal, w, 0.0)
            acc_s[h] += _dot(w.astype(BF16), v_ref[kb, :, hs])
            run_s[h] += sums[:, kb_size:]

    last = (i + 1) * n_diag - 1

    def diag_step(j, carry):
        step(last - j, True)
        return carry

    def past_step(j, carry):
        step(i * n_diag - 1 - j, False)
        return carry

    lax.fori_loop(0, n_diag, diag_step, 0)
    lax.fori_loop(0, i * n_diag, past_step, 0)
    for h in range(N_HEADS):
        o_ref[:, h * HEAD_DIM:(h + 1) * HEAD_DIM] = acc_s[h].astype(o_ref.dtype)


def _stick_breaking(q, k, v, qb):
    bsz, seq, width = q.shape
    nq, nk = seq // qb, seq // ATT_BLOCK
    qspec = pl.BlockSpec((None, None, qb, width), lambda b, i: (b, i, 0, 0))
    kspec = pl.BlockSpec((None, nk, ATT_BLOCK, width), lambda b, i: (b, 0, 0, 0))
    tri = _tri_consts(ATT_BLOCK)
    sums = jnp.concatenate([tri[0], tri[2]], axis=1)
    out = pl.pallas_call(
        functools.partial(_sb_kernel, qb=qb),
        grid=(bsz, nq),
        in_specs=[qspec, kspec, kspec, pl.BlockSpec(sums.shape, lambda b, i: (0, 0))],
        out_specs=qspec,
        out_shape=jax.ShapeDtypeStruct((bsz, nq, qb, width), BF16),
        scratch_shapes=[pltpu.VMEM((qb, width), BF16),
                        pltpu.VMEM((N_HEADS, qb, ATT_BLOCK), F32),
                        pltpu.VMEM((N_HEADS, qb, HEAD_DIM), F32)],
        compiler_params=_cparams(("parallel", "arbitrary")),
    )(q.reshape(bsz, nq, qb, width), k.reshape(bsz, nk, ATT_BLOCK, width),
      v.reshape(bsz, nk, ATT_BLOCK, width), sums)
    return out.reshape(bsz, seq, width)


def _cd_weight(cd_w_in):
    cw = N_HEADS * HEAD_DIM
    iw = IDX_HEADS * IDX_DIM
    offs = [0, cw, 2 * cw, 3 * cw, 3 * cw + iw, 3 * cw + iw + IDX_DIM, 3 * cw + iw + IDX_DIM + IDX_HEADS]
    qkv_c = cd_w_in[:, :offs[3]]
    q_idx = cd_w_in[:, offs[3]:offs[4]]
    kw_idx = cd_w_in[:, offs[4]:offs[6]]
    qkv_d = cd_w_in[:, offs[6]:]
    pad = jnp.zeros((cd_w_in.shape[0], LANES - kw_idx.shape[1]), cd_w_in.dtype)
    return jnp.concatenate([qkv_c, q_idx, qkv_d, kw_idx, pad], axis=1)


def kernel(x, c, norm_g, ada_w, ada_b, mlp_w1, mlp_w2, ab_w_in, ab_w_out, pool_w, pool_scale, ssm_lam_re, ssm_lam_im, ssm_log_dt, ssm_b_re, ssm_b_im, ssm_c_re, ssm_c_im, ssm_d, glu_w, glu_b, cd_w_in, cd_w_out, final_g):
    bsz, seq, d = x.shape
    depth = norm_g.shape[0]
    topk = min(TOPK_MAX, seq // 4)
    tm = min(512, seq)

    c_pad = jnp.zeros((SUBLANES, d), F32).at[:bsz].set(c)
    mod = _modulation(c_pad, ada_w.reshape(depth * 2, d, 3 * d), ada_b.reshape(depth * 2, 1, 3 * d))
    mod = mod[:, :, :bsz, None, :]

    for i in range(depth):
        j = i // 2
        shift, scale1, gate1 = mod[2 * i, 0], mod[2 * i, 1], mod[2 * i, 2]
        g = norm_g[i, 0][None, :]
        if i % 2 == 0:
            width = ab_w_in.shape[2]
            (proj,) = _norm_proj(x, g, shift, scale1, ab_w_in[j].astype(BF16), (width,), (BF16,), tm)
            s5 = _s5_params(ssm_lam_re[j], ssm_lam_im[j], ssm_log_dt[j],
                            ssm_b_re[j], ssm_b_im[j], ssm_c_re[j], ssm_c_im[j])
            y = _even_mixer(proj, pool_w[j].astype(BF16), pool_scale[j][None, :], s5,
                            ssm_d[j][None, :], glu_w[j].astype(BF16), glu_b[j][None, :],
                            tt=min(256, seq), scan_lanes=512)
            x = _out_proj(x, gate1, (y,), ab_w_out[j].astype(BF16), tm)
        else:
            hw = N_HEADS * HEAD_DIM
            widths = (hw,) * 7 + (LANES,)
            dtypes = (BF16,) * 7 + (F32,)
            qc, kc, vc, qi, qd, kd, vd, kiw = _norm_proj(
                x, g, shift, scale1, _cd_weight(cd_w_in[j]).astype(BF16), widths, dtypes, tm)
            oc = _dsa(qc, kc, vc, qi, kiw, topk, qb=ATT_QUERY_BLOCK)
            od = _stick_breaking(qd, kd, vd, qb=ATT_QUERY_BLOCK)
            x = _out_proj(x, gate1, (oc, od), cd_w_out[j].astype(BF16), tm)
        shift, scale1, gate1 = mod[2 * i + 1, 0], mod[2 * i + 1, 1], mod[2 * i + 1, 2]
        x = _mlp(x, norm_g[i, 1][None, :], shift, scale1, gate1,
                 mlp_w1[i].astype(BF16), mlp_w2[i].astype(BF16), final_g[None, :],
                 final_norm=(i == depth - 1), tm=min(1024, seq), tf=512)
    return x
```

```python
import functools
import math

import jax
import jax.numpy as jnp
from jax import lax
from jax.experimental import pallas as pl
from jax.experimental.pallas import tpu as pltpu

F32 = jnp.float32
BF16 = jnp.bfloat16

EPS = 1e-6
CHUNK = 64
POOL_WINDOWS = (2, 4, 8, 16)
POOL_CH = 128
SSM_GROUP = 16
SSM_STATE = 64
HEAD_DIM = 64
N_HEADS = 8
IDX_HEADS = 8
IDX_DIM = 64
IDX_SCALE = (IDX_HEADS ** -0.5) * (IDX_DIM ** -0.5)
TOPK_MAX = 256

LANES = 128
SUBLANES = 8
VMEM_LIMIT = 56 * 1024 * 1024

ATT_BLOCK = 128
ATT_QUERY_BLOCK = 512
NEG_BIG = -1e30
COUNT_ALL = 1e9
BISECT_ITERS = 40
INTERP_SPAN = 256.0
MAX_POPS = 4096


def _cparams(sem):
    return pltpu.CompilerParams(dimension_semantics=sem, vmem_limit_bytes=VMEM_LIMIT)


def _dot(a, b):
    return jnp.dot(a, b, preferred_element_type=F32)


def _dot_nt(a, b):
    return lax.dot_general(a, b, (((1,), (1,)), ((), ())), preferred_element_type=F32)


def _rowsum(x):
    return jnp.sum(x, axis=1, keepdims=True)


def _rowmax(x):
    return jnp.max(x, axis=1, keepdims=True)


def _norm_modulate(x, g, shift, scale1):
    ms = jnp.mean(x * x, axis=-1, keepdims=True)
    return (x * lax.rsqrt(ms + EPS)) * g * scale1 + shift


def _mod_kernel(c_ref, w_ref, b_ref, o_ref):
    j = pl.program_id(1)
    m = _dot(c_ref[...], w_ref[...]) + b_ref[...]
    o_ref[...] = m + jnp.where(j > 0, 1.0, 0.0)


def _modulation(c_pad, ada_w, ada_b):
    n_sub, d, _ = ada_w.shape
    rows = c_pad.shape[0]
    return pl.pallas_call(
        _mod_kernel,
        grid=(n_sub, 3),
        in_specs=[
            pl.BlockSpec((rows, d), lambda s, j: (0, 0)),
            pl.BlockSpec((None, d, d), lambda s, j: (s, 0, j)),
            pl.BlockSpec((None, 1, d), lambda s, j: (s, 0, j)),
        ],
        out_specs=pl.BlockSpec((None, None, rows, d), lambda s, j: (s, j, 0, 0)),
        out_shape=jax.ShapeDtypeStruct((n_sub, 3, rows, d), F32),
        compiler_params=_cparams(("arbitrary", "arbitrary")),
    )(c_pad, ada_w, ada_b)


def _norm_proj_kernel(x_ref, g_ref, shift_ref, scale_ref, w_ref, *o_refs, widths):
    h = _norm_modulate(x_ref[...], g_ref[...], shift_ref[...], scale_ref[...]).astype(BF16)
    y = _dot(h, w_ref[...])
    off = 0
    for o_ref, width in zip(o_refs, widths):
        o_ref[...] = y[:, off:off + width].astype(o_ref.dtype)
        off += width


def _norm_proj(x, g, shift, scale1, w, widths, dtypes, tm):
    bsz, seq, d = x.shape
    n = w.shape[1]
    vec = pl.BlockSpec((None, 1, d), lambda b, i: (b, 0, 0))
    return pl.pallas_call(
        functools.partial(_norm_proj_kernel, widths=widths),
        grid=(bsz, seq // tm),
        in_specs=[
            pl.BlockSpec((None, tm, d), lambda b, i: (b, i, 0)),
            pl.BlockSpec((1, d), lambda b, i: (0, 0)),
            vec, vec,
            pl.BlockSpec((d, n), lambda b, i: (0, 0)),
        ],
        out_specs=[pl.BlockSpec((None, tm, wd), lambda b, i: (b, i, 0)) for wd in widths],
        out_shape=[jax.ShapeDtypeStruct((bsz, seq, wd), dt) for wd, dt in zip(widths, dtypes)],
        compiler_params=_cparams(("parallel", "parallel")),
    )(x, g, shift, scale1, w)


def _out_proj_kernel(x_ref, gate_ref, *refs, n_y):
    y_refs, w_ref, o_ref = refs[:n_y], refs[n_y], refs[n_y + 1]
    y = jnp.concatenate([r[...] for r in y_refs], axis=1) if n_y > 1 else y_refs[0][...]
    o_ref[...] = x_ref[...] + gate_ref[...] * _dot(y, w_ref[...])


def _out_proj(x, gate1, ys, w, tm):
    bsz, seq, d = x.shape
    k = w.shape[0]
    return pl.pallas_call(
        functools.partial(_out_proj_kernel, n_y=len(ys)),
        grid=(bsz, seq // tm),
        in_specs=[
            pl.BlockSpec((None, tm, d), lambda b, i: (b, i, 0)),
            pl.BlockSpec((None, 1, d), lambda b, i: (b, 0, 0)),
            *[pl.BlockSpec((None, tm, y.shape[2]), lambda b, i: (b, i, 0)) for y in ys],
            pl.BlockSpec((k, d), lambda b, i: (0, 0)),
        ],
        out_specs=pl.BlockSpec((None, tm, d), lambda b, i: (b, i, 0)),
        out_shape=jax.ShapeDtypeStruct((bsz, seq, d), F32),
        compiler_params=_cparams(("parallel", "parallel")),
    )(x, gate1, *ys, w)


def _mlp_kernel(x_ref, g_ref, shift_ref, scale_ref, gate_ref, w1_ref, w2_ref, fg_ref, o_ref,
                h_s, acc_s, *, final_norm):
    f = pl.program_id(2)

    @pl.when(f == 0)
    def _():
        h_s[...] = _norm_modulate(x_ref[...], g_ref[...], shift_ref[...], scale_ref[...]).astype(BF16)
        acc_s[...] = jnp.zeros_like(acc_s)

    a = jnp.maximum(_dot(h_s[...], w1_ref[...]), 0.0)
    acc_s[...] += _dot((a * a).astype(BF16), w2_ref[...])

    @pl.when(f == pl.num_programs(2) - 1)
    def _():
        y = x_ref[...] + gate_ref[...] * acc_s[...]
        if final_norm:
            ms = jnp.mean(y * y, axis=-1, keepdims=True)
            y = (y * lax.rsqrt(ms + EPS)) * fg_ref[...]
        o_ref[...] = y


def _mlp(x, g, shift, scale1, gate1, w1, w2, final_g, final_norm, tm, tf):
    bsz, seq, d = x.shape
    dff = w1.shape[1]
    vec = pl.BlockSpec((None, 1, d), lambda b, i, f: (b, 0, 0))
    one = pl.BlockSpec((1, d), lambda b, i, f: (0, 0))
    return pl.pallas_call(
        functools.partial(_mlp_kernel, final_norm=final_norm),
        grid=(bsz, seq // tm, dff // tf),
        in_specs=[
            pl.BlockSpec((None, tm, d), lambda b, i, f: (b, i, 0)),
            one, vec, vec, vec,
            pl.BlockSpec((d, tf), lambda b, i, f: (0, f)),
            pl.BlockSpec((tf, d), lambda b, i, f: (f, 0)),
            one,
        ],
        out_specs=pl.BlockSpec((None, tm, d), lambda b, i, f: (b, i, 0)),
        out_shape=jax.ShapeDtypeStruct((bsz, seq, d), F32),
        scratch_shapes=[pltpu.VMEM((tm, d), BF16), pltpu.VMEM((tm, d), F32)],
        compiler_params=_cparams(("parallel", "parallel", "arbitrary")),
    )(x, g, shift, scale1, gate1, w1, w2, final_g)


def _cmul(ar, ai, br, bi):
    return ar * br - ai * bi, ar * bi + ai * br


def _s5_params_kernel(lr_ref, li_ref, ldt_ref, brt_ref, bit_ref, crt_ref, cit_ref,
                      wb_ref, wc_ref, as_re_ref, as_im_ref, pw_re_ref, pw_im_ref, a8_re_ref, a8_im_ref):
    lr, li = lr_ref[...], li_ref[...]
    dt = jnp.exp(ldt_ref[...])
    mag = jnp.exp(lr * dt)
    ar, ai = mag * jnp.cos(li * dt), mag * jnp.sin(li * dt)
    den = lr * lr + li * li
    zr = ((ar - 1.0) * lr + ai * li) / den
    zi = (ai * lr - (ar - 1.0) * li) / den

    rows, cols = brt_ref.shape
    rgrp = lax.broadcasted_iota(jnp.int32, (rows, cols), 0) // SSM_GROUP
    cgrp = lax.broadcasted_iota(jnp.int32, (rows, cols), 1) // SSM_STATE
    diag = rgrp == cgrp
    br, bi = brt_ref[...], bit_ref[...]
    wb_ref[0] = jnp.where(diag, zr * br - zi * bi, 0.0).astype(wb_ref.dtype)
    wb_ref[1] = jnp.where(diag, zr * bi + zi * br, 0.0).astype(wb_ref.dtype)
    wc_ref[0] = jnp.where(diag, crt_ref[...], 0.0).astype(wc_ref.dtype)
    wc_ref[1] = jnp.where(diag, cit_ref[...], 0.0).astype(wc_ref.dtype)

    n = lr.shape[1]
    row = lax.broadcasted_iota(jnp.int32, (SUBLANES, n), 0)
    pr, pi = jnp.broadcast_to(ar, (SUBLANES, n)), jnp.broadcast_to(ai, (SUBLANES, n))
    a1r, a1i = pr, pi
    for k in range(3):
        keep = row >= (1 << k)
        as_re_ref[k] = jnp.where(keep, pr, 0.0)
        as_im_ref[k] = jnp.where(keep, pi, 0.0)
        pr, pi = _cmul(pr, pi, pr, pi)
    a8_re_ref[...] = pr
    a8_im_ref[...] = pi
    qr, qi = a1r, a1i
    for r in range(1, SUBLANES):
        nr, ni = _cmul(qr, qi, a1r, a1i)
        qr = jnp.where(row >= r, nr, qr)
        qi = jnp.where(row >= r, ni, qi)
    pw_re_ref[...] = qr
    pw_im_ref[...] = qi


def _s5_params(lam_re, lam_im, log_dt, b_re, b_im, c_re, c_im):
    g, p = lam_re.shape
    c = b_re.shape[2]
    n = g * p
    flat = lambda a: a.reshape(1, n)
    ldt = jnp.broadcast_to(log_dt[:, None], (g, p)).reshape(1, n)
    b_t = lambda a: jnp.tile(jnp.transpose(a, (2, 0, 1)).reshape(c, n), (g, 1))
    c_t = lambda a: jnp.tile(a.reshape(g * c, p), (1, g))
    outs = pl.pallas_call(
        _s5_params_kernel,
        out_shape=[jax.ShapeDtypeStruct((2, g * c, n), BF16), jax.ShapeDtypeStruct((2, g * c, n), BF16),
                   jax.ShapeDtypeStruct((3, SUBLANES, n), F32), jax.ShapeDtypeStruct((3, SUBLANES, n), F32),
                   jax.ShapeDtypeStruct((SUBLANES, n), F32), jax.ShapeDtypeStruct((SUBLANES, n), F32),
                   jax.ShapeDtypeStruct((SUBLANES, n), F32), jax.ShapeDtypeStruct((SUBLANES, n), F32)],
        compiler_params=pltpu.CompilerParams(vmem_limit_bytes=VMEM_LIMIT),
    )(flat(lam_re), flat(lam_im), ldt, b_t(b_re), b_t(b_im), c_t(c_re), c_t(c_im))
    wb, wct = outs[0], outs[1]
    wc = jnp.transpose(wct, (0, 2, 1))
    return (wb, wc) + tuple(outs[2:])


def _gelu_tanh(x):
    return 0.5 * x * (1.0 + jnp.tanh(math.sqrt(2.0 / math.pi) * (x + 0.044715 * (x * x * x))))


def _even_kernel(p_ref, band_ref, wpool_ref, pscale_ref, wb_ref, wc_ref, dskip_ref, wglu_ref, bglu_ref,
                 as_re_ref, as_im_ref, pw_re_ref, pw_im_ref, a8_re_ref, a8_im_ref,
                 o_ref, prev_s, xr_s, xi_s, car_re, car_im, *, mix_a, scan_lanes):
    i = pl.program_id(1)
    tt = p_ref.shape[0]

    @pl.when(i == 0)
    def _():
        prev_s[...] = jnp.zeros_like(prev_s)
        car_re[...] = jnp.zeros_like(car_re)
        car_im[...] = jnp.zeros_like(car_im)

    ua = p_ref[:, :mix_a]
    ext = jnp.concatenate([prev_s[...], ua], axis=0)
    t1 = i * tt + lax.broadcasted_iota(jnp.int32, (tt, POOL_CH), 0) + 1
    for g, win in enumerate(POOL_WINDOWS):
        sl = slice(g * POOL_CH, (g + 1) * POOL_CH)
        wsum = _dot(band_ref[g], ext[:, sl])
        cnt = jnp.minimum(t1, win).astype(F32)
        pg = wsum / cnt - ua[:, sl].astype(F32)
        yg = _dot(pg.astype(BF16), wpool_ref[g]) * pscale_ref[:, sl]
        o_ref[:, sl] = yg.astype(o_ref.dtype)
    prev_s[...] = ua

    ub = p_ref[:, mix_a:]
    xr_s[...] = _dot(ub, wb_ref[0])
    xi_s[...] = _dot(ub, wb_ref[1])
    n_state = xr_s.shape[1]
    for c in range(n_state // scan_lanes):
        sl = slice(c * scan_lanes, (c + 1) * scan_lanes)

        def block(r, carry, sl=sl):
            cr, ci = carry
            rows = pl.ds(pl.multiple_of(r * SUBLANES, SUBLANES), SUBLANES)
            hr, hi = xr_s[rows, sl], xi_s[rows, sl]
            for k in range(3):
                sr = pltpu.roll(hr, 1 << k, 0)
                si = pltpu.roll(hi, 1 << k, 0)
                dr, di = _cmul(as_re_ref[k, :, sl], as_im_ref[k, :, sl], sr, si)
                hr, hi = hr + dr, hi + di
            lr = jnp.broadcast_to(hr[SUBLANES - 1:SUBLANES, :], hr.shape)
            li = jnp.broadcast_to(hi[SUBLANES - 1:SUBLANES, :], hi.shape)
            dr, di = _cmul(pw_re_ref[:, sl], pw_im_ref[:, sl], cr, ci)
            xr_s[rows, sl] = hr + dr
            xi_s[rows, sl] = hi + di
            dr, di = _cmul(a8_re_ref[:, sl], a8_im_ref[:, sl], cr, ci)
            return lr + dr, li + di

        cr, ci = lax.fori_loop(0, tt // SUBLANES, block, (car_re[:, sl], car_im[:, sl]))
        car_re[:, sl] = cr
        car_im[:, sl] = ci

    y = _dot(xr_s[...].astype(BF16), wc_ref[0]) - _dot(xi_s[...].astype(BF16), wc_ref[1])
    y = _gelu_tanh(y + dskip_ref[...] * ub.astype(F32))
    z = _dot(y.astype(BF16), wglu_ref[...]) + bglu_ref[...]
    o_ref[:, mix_a:] = (y * (1.0 / (1.0 + jnp.exp(-z)))).astype(o_ref.dtype)


def _pool_band(tt):
    t = jnp.arange(tt)[:, None] + tt
    j = jnp.arange(2 * tt)[None, :]
    return jnp.stack([((j <= t) & (j > t - w)) for w in POOL_WINDOWS]).astype(BF16)


def _even_mixer(proj, wpool, pscale, s5, dskip, wglu, bglu, tt, scan_lanes):
    bsz, seq, width = proj.shape
    wb, wc, as_re, as_im, pw_re, pw_im, a8_re, a8_im = s5
    mix_a = len(POOL_WINDOWS) * POOL_CH
    mix_b = width - mix_a
    n_state = wb.shape[2]
    const = lambda a: pl.BlockSpec(a.shape, lambda b, i, nd=a.ndim: (0,) * nd)
    band = _pool_band(tt)
    args = (band, wpool, pscale, wb, wc, dskip, wglu, bglu, as_re, as_im, pw_re, pw_im, a8_re, a8_im)
    return pl.pallas_call(
        functools.partial(_even_kernel, mix_a=mix_a, scan_lanes=scan_lanes),
        grid=(bsz, seq // tt),
        in_specs=[pl.BlockSpec((None, tt, width), lambda b, i: (b, i, 0))] + [const(a) for a in args],
        out_specs=pl.BlockSpec((None, tt, width), lambda b, i: (b, i, 0)),
        out_shape=jax.ShapeDtypeStruct((bsz, seq, width), BF16),
        scratch_shapes=[pltpu.VMEM((tt, mix_a), BF16),
                        pltpu.VMEM((tt, n_state), F32), pltpu.VMEM((tt, n_state), F32),
                        pltpu.VMEM((SUBLANES, n_state), F32), pltpu.VMEM((SUBLANES, n_state), F32)],
        compiler_params=_cparams(("parallel", "arbitrary")),
    )(proj, *args)


def _pair_blockdiag(x, first_ref, second_ref):
    return jnp.concatenate([x * first_ref[...], x * second_ref[...]], axis=0)


def _dsa_kernel(q_ref, qi_ref, wq_ref, k_ref, v_ref, kk_ref, tri_ref, ones_ref, first_ref, second_ref, o_ref,
                s_s, qs_s, mx_s, l_s, acc_s, *, topk, qb):
    i = pl.program_id(1)
    kb_size = ATT_BLOCK
    nkb = (i + 1) * (qb // kb_size)
    shape = (qb, kb_size)
    row = lax.broadcasted_iota(jnp.int32, shape, 0)
    col = lax.broadcasted_iota(jnp.int32, shape, 1)
    q_chunk = (i * qb + row) // CHUNK
    bcast = lambda a: jnp.broadcast_to(a, shape)
    kf = float(topk)
    pairs = [slice(p * LANES, (p + 1) * LANES) for p in range(N_HEADS // 2)]

    wq = wq_ref[...]
    w_cols = [bcast(wq[:, h:h + 1] * IDX_SCALE) for h in range(IDX_HEADS)]

    def score_block(kb, carry):
        rmax, rmin = carry
        kbd = _pair_blockdiag(kk_ref[kb], first_ref, second_ref)
        acc = jnp.zeros(shape, F32)
        for p, ps in enumerate(pairs):
            rel = jnp.maximum(_dot_nt(qi_ref[:, ps], kbd), 0.0)
            acc = acc + rel[:, :kb_size] * w_cols[2 * p] + rel[:, kb_size:] * w_cols[2 * p + 1]
        adm = ((kb * kb_size + col) // CHUNK) <= q_chunk
        s_s[kb] = jnp.where(adm, acc, -jnp.inf)
        return jnp.maximum(rmax, jnp.where(adm, acc, -jnp.inf)), jnp.minimum(rmin, jnp.where(adm, acc, jnp.inf))

    rmax, rmin = lax.fori_loop(0, nkb, score_block,
                               (jnp.full(shape, -jnp.inf, F32), jnp.full(shape, jnp.inf, F32)))
    smax, smin = _rowmax(rmax), jnp.min(rmin, axis=1, keepdims=True)

    def count_ge(x):
        xb = bcast(x)
        cnt = lax.fori_loop(0, nkb, lambda kb, a: a + jnp.where(s_s[kb] >= xb, 1.0, 0.0),
                            jnp.zeros(shape, F32))
        return _rowsum(cnt)

    n_adm = ((q_chunk[:, :1] + 1) * CHUNK).astype(F32)
    lo0 = smin
    hi0 = 2.0 * jnp.maximum(jnp.abs(smax), jnp.abs(smin)) + 1.0

    def unresolved(clo):
        return jnp.max(jnp.where(clo > kf, 1.0, 0.0))

    def bisect_cond(carry):
        return jnp.logical_and(carry[0] > 0.0, carry[1] < BISECT_ITERS)

    def bisect(carry):
        _, it, lo, hi, clo, chi = carry
        span = clo - chi
        aim = (clo - (kf + 0.5)) / span
        interpolate = jnp.logical_and(span < INTERP_SPAN, it % 2 == 0)
        mid = lo + jnp.where(interpolate, aim, 0.5) * (hi - lo)
        c = count_ge(mid)
        active = clo > kf
        up = jnp.logical_and(active, c >= kf)
        down = jnp.logical_and(active, c < kf)
        clo = jnp.where(up, c, clo)
        return (unresolved(clo), it + 1, jnp.where(up, mid, lo), jnp.where(down, mid, hi), clo,
                jnp.where(down, c, chi))

    _, _, lo, hi, clo, chi = lax.while_loop(
        bisect_cond, bisect, (unresolved(n_adm), jnp.int32(0), lo0, hi0, n_adm, jnp.zeros_like(lo0)))

    done0 = jnp.where(clo <= kf, 1.0, 0.0)

    def pop_cond(carry):
        return jnp.logical_and(carry[0] > 0.0, carry[1] < MAX_POPS)

    def pop(carry):
        _, it, hi, chi, thr, need, done = carry
        hib = bcast(hi)
        v = _rowmax(lax.fori_loop(0, nkb, lambda kb, a: jnp.maximum(a, jnp.where(s_s[kb] < hib, s_s[kb], -jnp.inf)),
                                  jnp.full(shape, -jnp.inf, F32)))
        vb = bcast(v)
        cv = _rowsum(lax.fori_loop(0, nkb, lambda kb, a: a + jnp.where(s_s[kb] == vb, 1.0, 0.0),
                                   jnp.zeros(shape, F32)))
        active = done < 0.5
        fin = jnp.logical_and(active, chi + cv >= kf)
        go = jnp.logical_and(active, chi + cv < kf)
        thr = jnp.where(fin, v, thr)
        need = jnp.where(fin, kf - chi, need)
        done = jnp.where(fin, 1.0, done)
        hi = jnp.where(go, v, hi)
        chi = jnp.where(go, chi + cv, chi)
        return (jnp.max(1.0 - done), it + 1, hi, chi, thr, need, done)

    left0 = jnp.max(1.0 - done0)
    _, _, _, _, thr, need, _ = lax.while_loop(
        pop_cond, pop, (left0, jnp.int32(0), hi, chi, lo, jnp.full_like(lo, COUNT_ALL), done0))
    thrb, needb = bcast(thr), bcast(need)

    @pl.when(left0 > 0.0)
    def _():
        def tie_block(kb, run):
            s = s_s[kb]
            eq = jnp.where(s == thrb, 1.0, 0.0)
            rank = _dot(eq.astype(BF16), tri_ref[1]) + run
            take = jnp.where(s == thrb, jnp.where(rank < needb, 0.0, NEG_BIG), NEG_BIG)
            s_s[kb] = jnp.where(s > thrb, 0.0, take)
            return run + bcast(_rowsum(eq))
        lax.fori_loop(0, nkb, tie_block, jnp.zeros(shape, F32))

    @pl.when(left0 <= 0.0)
    def _():
        def sel_block(kb, carry):
            s_s[kb] = jnp.where(s_s[kb] >= thrb, 0.0, NEG_BIG)
            return carry
        lax.fori_loop(0, nkb, sel_block, 0)

    rel_pos = (row - col).astype(F32)
    first_half = col < HEAD_DIM
    qs_s[...] = q_ref[...] * (HEAD_DIM ** -0.5)
    mx_s[...] = jnp.full(mx_s.shape, NEG_BIG, F32)
    l_s[...] = jnp.zeros_like(l_s)
    acc_s[...] = jnp.zeros_like(acc_s)
    slopes = [2.0 ** (-8.0 * (h + 1) / N_HEADS) for h in range(N_HEADS)]

    def attend(kb, carry):
        dist = jnp.abs(rel_pos + (i * qb - kb * kb_size).astype(F32))
        mask = s_s[kb]
        scores = []
        for p, ps in enumerate(pairs):
            kbd = _pair_blockdiag(k_ref[kb, :, ps], first_ref, second_ref)
            bias = jnp.concatenate([mask - slopes[2 * p] * dist, mask - slopes[2 * p + 1] * dist], axis=1)
            scores.append(_dot_nt(qs_s[:, ps], kbd) + bias)
        for p, ps in enumerate(pairs):
            s = scores[p]
            m_old = mx_s[p]
            m_new = jnp.maximum(m_old, jnp.concatenate(
                [bcast(_rowmax(s[:, :kb_size])), bcast(_rowmax(s[:, kb_size:]))], axis=1))
            mx_s[p] = m_new
            pexp = jnp.exp((s - m_new).astype(BF16))
            shrink = m_old - m_new
            alpha = jnp.exp(jnp.where(first_half, shrink[:, :kb_size], shrink[:, kb_size:]))
            vbd = _pair_blockdiag(v_ref[kb, :, ps], first_ref, second_ref)
            l_s[p] = alpha * l_s[p] + _dot(pexp, ones_ref[...])
            acc_s[p] = alpha * acc_s[p] + _dot(pexp, vbd)
        return carry

    lax.fori_loop(0, nkb, attend, 0)
    for p, ps in enumerate(pairs):
        o_ref[:, ps] = (acc_s[p] / l_s[p]).astype(o_ref.dtype)


def _tri_consts(n):
    j = jnp.arange(n)[:, None]
    s = jnp.arange(n)[None, :]
    return jnp.stack([j > s, j < s]).astype(BF16)


def _pair_consts(n):
    tri = _tri_consts(n)
    zero = jnp.zeros((n, n), BF16)
    tri2 = jnp.block([[tri[0], zero], [zero, tri[0]]])
    first = (jnp.arange(n)[None, :] < n // 2) & jnp.ones((n, 1), bool)
    ones2 = jnp.concatenate([first, ~first], axis=0).astype(BF16)
    return tri2, ones2, first.astype(BF16), (~first).astype(BF16)


def _dsa(q, k, v, qi, kk, wq, topk, qb):
    bsz, seq, width = q.shape
    nq, nk = seq // qb, seq // ATT_BLOCK
    qblk = lambda a: a.reshape(bsz, nq, qb, a.shape[-1])
    kblk = lambda a: a.reshape(bsz, nk, ATT_BLOCK, a.shape[-1])
    qspec = lambda wd: pl.BlockSpec((None, None, qb, wd), lambda b, i: (b, i, 0, 0))
    kspec = lambda wd: pl.BlockSpec((None, nk, ATT_BLOCK, wd), lambda b, i: (b, 0, 0, 0))
    _, ones2, first, second = _pair_consts(ATT_BLOCK)
    consts = (_tri_consts(ATT_BLOCK), ones2, first, second)
    out = pl.pallas_call(
        functools.partial(_dsa_kernel, topk=topk, qb=qb),
        grid=(bsz, nq),
        in_specs=[qspec(width), qspec(width), qspec(LANES), kspec(width), kspec(width), kspec(LANES)]
                 + [pl.BlockSpec(a.shape, lambda b, i, nd=a.ndim: (0,) * nd) for a in consts],
        out_specs=qspec(width),
        out_shape=jax.ShapeDtypeStruct((bsz, nq, qb, width), BF16),
        scratch_shapes=[pltpu.VMEM((nk, qb, ATT_BLOCK), F32),
                        pltpu.VMEM((qb, width), BF16),
                        pltpu.VMEM((N_HEADS // 2, qb, 2 * ATT_BLOCK), F32),
                        pltpu.VMEM((N_HEADS // 2, qb, LANES), F32),
                        pltpu.VMEM((N_HEADS // 2, qb, LANES), F32)],
        compiler_params=_cparams(("parallel", "arbitrary")),
    )(qblk(q), qblk(qi), qblk(wq), kblk(k), kblk(v), kblk(kk), *consts)
    return out.reshape(bsz, seq, width)


def _sb_kernel(q_ref, k_ref, v_ref, tri_ref, ones_ref, first_ref, second_ref, o_ref, qs_s, run_s, acc_s, *, qb):
    i = pl.program_id(1)
    kb_size = ATT_BLOCK
    n_diag = qb // kb_size
    n_pairs = N_HEADS // 2
    shape = (qb, 2 * kb_size)
    row = lax.broadcasted_iota(jnp.int32, shape, 0)
    col = lax.broadcasted_iota(jnp.int32, shape, 1) % kb_size
    qs_s[...] = q_ref[...] * (HEAD_DIM ** -0.5)
    run_s[...] = jnp.zeros_like(run_s)
    acc_s[...] = jnp.zeros_like(acc_s)

    def step(kb, masked):
        causal = (kb * kb_size + col) < (i * qb + row)
        pairs = [slice(p * LANES, (p + 1) * LANES) for p in range(n_pairs)]
        log_sig, log_keep = [], []
        for ps in pairs:
            kbd = _pair_blockdiag(k_ref[kb, :, ps], first_ref, second_ref)
            z = _dot_nt(qs_s[:, ps], kbd).astype(BF16)
            soft = jnp.log(1.0 + jnp.exp(-jnp.abs(z)))
            keep = -jnp.maximum(z, 0.0) - soft
            log_sig.append(jnp.minimum(z, 0.0) - soft)
            log_keep.append(jnp.where(causal, keep, jnp.zeros_like(keep)) if masked else keep)
        for p, ps in enumerate(pairs):
            after = _dot(log_keep[p], tri_ref[...]).astype(BF16)
            w = jnp.exp(log_sig[p] + after)
            if masked:
                w = jnp.where(causal, w, jnp.zeros_like(w))
            vbd = _pair_blockdiag(v_ref[kb, :, ps], first_ref, second_ref)
            acc_s[p] += jnp.exp(run_s[p]) * _dot(w, vbd)
            run_s[p] += _dot(log_keep[p], ones_ref[...])

    last = (i + 1) * n_diag - 1

    def diag_step(j, carry):
        step(last - j, True)
        return carry

    def past_step(j, carry):
        step(i * n_diag - 1 - j, False)
        return carry

    lax.fori_loop(0, n_diag, diag_step, 0)
    lax.fori_loop(0, i * n_diag, past_step, 0)
    for p in range(n_pairs):
        o_ref[:, p * LANES:(p + 1) * LANES] = acc_s[p].astype(o_ref.dtype)


def _stick_breaking(q, k, v, qb):
    bsz, seq, width = q.shape
    nq, nk = seq // qb, seq // ATT_BLOCK
    qspec = pl.BlockSpec((None, None, qb, width), lambda b, i: (b, i, 0, 0))
    kspec = pl.BlockSpec((None, nk, ATT_BLOCK, width), lambda b, i: (b, 0, 0, 0))
    consts = _pair_consts(ATT_BLOCK)
    out = pl.pallas_call(
        functools.partial(_sb_kernel, qb=qb),
        grid=(bsz, nq),
        in_specs=[qspec, kspec, kspec] + [pl.BlockSpec(a.shape, lambda b, i: (0, 0)) for a in consts],
        out_specs=qspec,
        out_shape=jax.ShapeDtypeStruct((bsz, nq, qb, width), BF16),
        scratch_shapes=[pltpu.VMEM((qb, width), BF16),
                        pltpu.VMEM((N_HEADS // 2, qb, LANES), F32),
                        pltpu.VMEM((N_HEADS // 2, qb, LANES), F32)],
        compiler_params=_cparams(("parallel", "arbitrary")),
    )(q.reshape(bsz, nq, qb, width), k.reshape(bsz, nk, ATT_BLOCK, width),
      v.reshape(bsz, nk, ATT_BLOCK, width), *consts)
    return out.reshape(bsz, seq, width)


def _cd_weight(cd_w_in):
    cw = N_HEADS * HEAD_DIM
    iw = IDX_HEADS * IDX_DIM
    offs = [0, cw, 2 * cw, 3 * cw, 3 * cw + iw, 3 * cw + iw + IDX_DIM, 3 * cw + iw + IDX_DIM + IDX_HEADS]
    qkv_c = cd_w_in[:, :offs[3]]
    q_idx = cd_w_in[:, offs[3]:offs[4]]
    k_idx = cd_w_in[:, offs[4]:offs[5]]
    w_idx = cd_w_in[:, offs[5]:offs[6]]
    qkv_d = cd_w_in[:, offs[6]:]
    pad = jnp.zeros((cd_w_in.shape[0], LANES - w_idx.shape[1]), cd_w_in.dtype)
    return jnp.concatenate([qkv_c, q_idx, qkv_d, k_idx, k_idx, w_idx, pad], axis=1)


def kernel(x, c, norm_g, ada_w, ada_b, mlp_w1, mlp_w2, ab_w_in, ab_w_out, pool_w, pool_scale, ssm_lam_re, ssm_lam_im, ssm_log_dt, ssm_b_re, ssm_b_im, ssm_c_re, ssm_c_im, ssm_d, glu_w, glu_b, cd_w_in, cd_w_out, final_g):
    bsz, seq, d = x.shape
    depth = norm_g.shape[0]
    topk = min(TOPK_MAX, seq // 4)
    tm = min(512, seq)

    c_pad = jnp.zeros((SUBLANES, d), F32).at[:bsz].set(c)
    mod = _modulation(c_pad, ada_w.reshape(depth * 2, d, 3 * d), ada_b.reshape(depth * 2, 1, 3 * d))
    mod = mod[:, :, :bsz, None, :]

    for i in range(depth):
        j = i // 2
        shift, scale1, gate1 = mod[2 * i, 0], mod[2 * i, 1], mod[2 * i, 2]
        g = norm_g[i, 0][None, :]
        if i % 2 == 0:
            width = ab_w_in.shape[2]
            (proj,) = _norm_proj(x, g, shift, scale1, ab_w_in[j].astype(BF16), (width,), (BF16,), tm)
            s5 = _s5_params(ssm_lam_re[j], ssm_lam_im[j], ssm_log_dt[j],
                            ssm_b_re[j], ssm_b_im[j], ssm_c_re[j], ssm_c_im[j])
            y = _even_mixer(proj, pool_w[j].astype(BF16), pool_scale[j][None, :], s5,
                            ssm_d[j][None, :], glu_w[j].astype(BF16), glu_b[j][None, :],
                            tt=min(256, seq), scan_lanes=512)
            x = _out_proj(x, gate1, (y,), ab_w_out[j].astype(BF16), tm)
        else:
            hw = N_HEADS * HEAD_DIM
            widths = (hw,) * 7 + (LANES, LANES)
            dtypes = (BF16,) * 8 + (F32,)
            qc, kc, vc, qi, qd, kd, vd, kk, wq = _norm_proj(
                x, g, shift, scale1, _cd_weight(cd_w_in[j]).astype(BF16), widths, dtypes, tm)
            oc = _dsa(qc, kc, vc, qi, kk, wq, topk, qb=ATT_QUERY_BLOCK)
            od = _stick_breaking(qd, kd, vd, qb=ATT_QUERY_BLOCK)
            x = _out_proj(x, gate1, (oc, od), cd_w_out[j].astype(BF16), tm)
        shift, scale1, gate1 = mod[2 * i + 1, 0], mod[2 * i + 1, 1], mod[2 * i + 1, 2]
        x = _mlp(x, norm_g[i, 1][None, :], shift, scale1, gate1,
                 mlp_w1[i].astype(BF16), mlp_w2[i].astype(BF16), final_g[None, :],
                 final_norm=(i == depth - 1), tm=min(1024, seq), tf=512)
    return x
```

```python
import functools
import math

import jax
import jax.numpy as jnp
from jax import lax
from jax.experimental import pallas as pl
from jax.experimental.pallas import tpu as pltpu

F32 = jnp.float32
BF16 = jnp.bfloat16

EPS = 1e-6
CHUNK = 64
POOL_WINDOWS = (2, 4, 8, 16)
POOL_CH = 128
SSM_GROUP = 16
SSM_STATE = 64
HEAD_DIM = 64
N_HEADS = 8
IDX_HEADS = 8
IDX_DIM = 64
IDX_SCALE = (IDX_HEADS ** -0.5) * (IDX_DIM ** -0.5)
TOPK_MAX = 256

LANES = 128
SUBLANES = 8
VMEM_LIMIT = 56 * 1024 * 1024

ATT_BLOCK = 128
ATT_QUERY_BLOCK = 512
NEG_BIG = -1e30
COUNT_ALL = 1e9
BISECT_ITERS = 28
COUNT_ROWS = 128
INTERP_SPAN = 256.0
MAX_POPS = 4096


def _cparams(sem):
    return pltpu.CompilerParams(dimension_semantics=sem, vmem_limit_bytes=VMEM_LIMIT)


def _dot(a, b):
    return jnp.dot(a, b, preferred_element_type=F32)


def _dot_nt(a, b):
    return lax.dot_general(a, b, (((1,), (1,)), ((), ())), preferred_element_type=F32)


def _rowsum(x):
    return jnp.sum(x, axis=1, keepdims=True)


def _rowmax(x):
    return jnp.max(x, axis=1, keepdims=True)


def _norm_modulate(x, g, shift, scale1):
    ms = jnp.mean(x * x, axis=-1, keepdims=True)
    return (x * lax.rsqrt(ms + EPS)) * g * scale1 + shift


def _mod_kernel(c_ref, w_ref, b_ref, o_ref):
    j = pl.program_id(1)
    m = _dot(c_ref[...], w_ref[...]) + b_ref[...]
    o_ref[...] = m + jnp.where(j > 0, 1.0, 0.0)


def _modulation(c_pad, ada_w, ada_b):
    n_sub, d, _ = ada_w.shape
    rows = c_pad.shape[0]
    return pl.pallas_call(
        _mod_kernel,
        grid=(n_sub, 3),
        in_specs=[
            pl.BlockSpec((rows, d), lambda s, j: (0, 0)),
            pl.BlockSpec((None, d, d), lambda s, j: (s, 0, j)),
            pl.BlockSpec((None, 1, d), lambda s, j: (s, 0, j)),
        ],
        out_specs=pl.BlockSpec((None, None, rows, d), lambda s, j: (s, j, 0, 0)),
        out_shape=jax.ShapeDtypeStruct((n_sub, 3, rows, d), F32),
        compiler_params=_cparams(("arbitrary", "arbitrary")),
    )(c_pad, ada_w, ada_b)


def _norm_proj_kernel(x_ref, g_ref, shift_ref, scale_ref, w_ref, *o_refs, widths):
    h = _norm_modulate(x_ref[...], g_ref[...], shift_ref[...], scale_ref[...]).astype(BF16)
    y = _dot(h, w_ref[...])
    off = 0
    for o_ref, width in zip(o_refs, widths):
        o_ref[...] = y[:, off:off + width].astype(o_ref.dtype)
        off += width


def _norm_proj(x, g, shift, scale1, w, widths, dtypes, tm):
    bsz, seq, d = x.shape
    n = w.shape[1]
    vec = pl.BlockSpec((None, 1, d), lambda b, i: (b, 0, 0))
    return pl.pallas_call(
        functools.partial(_norm_proj_kernel, widths=widths),
        grid=(bsz, seq // tm),
        in_specs=[
            pl.BlockSpec((None, tm, d), lambda b, i: (b, i, 0)),
            pl.BlockSpec((1, d), lambda b, i: (0, 0)),
            vec, vec,
            pl.BlockSpec((d, n), lambda b, i: (0, 0)),
        ],
        out_specs=[pl.BlockSpec((None, tm, wd), lambda b, i: (b, i, 0)) for wd in widths],
        out_shape=[jax.ShapeDtypeStruct((bsz, seq, wd), dt) for wd, dt in zip(widths, dtypes)],
        compiler_params=_cparams(("parallel", "parallel")),
    )(x, g, shift, scale1, w)


def _out_proj_kernel(x_ref, gate_ref, *refs, n_y):
    y_refs, w_ref, o_ref = refs[:n_y], refs[n_y], refs[n_y + 1]
    y = jnp.concatenate([r[...] for r in y_refs], axis=1) if n_y > 1 else y_refs[0][...]
    o_ref[...] = x_ref[...] + gate_ref[...] * _dot(y, w_ref[...])


def _out_proj(x, gate1, ys, w, tm):
    bsz, seq, d = x.shape
    k = w.shape[0]
    return pl.pallas_call(
        functools.partial(_out_proj_kernel, n_y=len(ys)),
        grid=(bsz, seq // tm),
        in_specs=[
            pl.BlockSpec((None, tm, d), lambda b, i: (b, i, 0)),
            pl.BlockSpec((None, 1, d), lambda b, i: (b, 0, 0)),
            *[pl.BlockSpec((None, tm, y.shape[2]), lambda b, i: (b, i, 0)) for y in ys],
            pl.BlockSpec((k, d), lambda b, i: (0, 0)),
        ],
        out_specs=pl.BlockSpec((None, tm, d), lambda b, i: (b, i, 0)),
        out_shape=jax.ShapeDtypeStruct((bsz, seq, d), F32),
        compiler_params=_cparams(("parallel", "parallel")),
    )(x, gate1, *ys, w)


def _mlp_kernel(x_ref, g_ref, shift_ref, scale_ref, gate_ref, w1_ref, w2_ref, fg_ref, o_ref,
                h_s, acc_s, *, final_norm):
    f = pl.program_id(2)

    @pl.when(f == 0)
    def _():
        h_s[...] = _norm_modulate(x_ref[...], g_ref[...], shift_ref[...], scale_ref[...]).astype(BF16)
        acc_s[...] = jnp.zeros_like(acc_s)

    a = jnp.maximum(_dot(h_s[...], w1_ref[...]), 0.0)
    acc_s[...] += _dot((a * a).astype(BF16), w2_ref[...])

    @pl.when(f == pl.num_programs(2) - 1)
    def _():
        y = x_ref[...] + gate_ref[...] * acc_s[...]
        if final_norm:
            ms = jnp.mean(y * y, axis=-1, keepdims=True)
            y = (y * lax.rsqrt(ms + EPS)) * fg_ref[...]
        o_ref[...] = y


def _mlp(x, g, shift, scale1, gate1, w1, w2, final_g, final_norm, tm, tf):
    bsz, seq, d = x.shape
    dff = w1.shape[1]
    vec = pl.BlockSpec((None, 1, d), lambda b, i, f: (b, 0, 0))
    one = pl.BlockSpec((1, d), lambda b, i, f: (0, 0))
    return pl.pallas_call(
        functools.partial(_mlp_kernel, final_norm=final_norm),
        grid=(bsz, seq // tm, dff // tf),
        in_specs=[
            pl.BlockSpec((None, tm, d), lambda b, i, f: (b, i, 0)),
            one, vec, vec, vec,
            pl.BlockSpec((d, tf), lambda b, i, f: (0, f)),
            pl.BlockSpec((tf, d), lambda b, i, f: (f, 0)),
            one,
        ],
        out_specs=pl.BlockSpec((None, tm, d), lambda b, i, f: (b, i, 0)),
        out_shape=jax.ShapeDtypeStruct((bsz, seq, d), F32),
        scratch_shapes=[pltpu.VMEM((tm, d), BF16), pltpu.VMEM((tm, d), F32)],
        compiler_params=_cparams(("parallel", "parallel", "arbitrary")),
    )(x, g, shift, scale1, gate1, w1, w2, final_g)


def _cmul(ar, ai, br, bi):
    return ar * br - ai * bi, ar * bi + ai * br


def _s5_params_kernel(lr_ref, li_ref, ldt_ref, brt_ref, bit_ref, crt_ref, cit_ref,
                      wb_ref, wc_ref, as_re_ref, as_im_ref, pw_re_ref, pw_im_ref, a8_re_ref, a8_im_ref):
    lr, li = lr_ref[...], li_ref[...]
    dt = jnp.exp(ldt_ref[...])
    mag = jnp.exp(lr * dt)
    ar, ai = mag * jnp.cos(li * dt), mag * jnp.sin(li * dt)
    den = lr * lr + li * li
    zr = ((ar - 1.0) * lr + ai * li) / den
    zi = (ai * lr - (ar - 1.0) * li) / den

    rows, cols = brt_ref.shape
    rgrp = lax.broadcasted_iota(jnp.int32, (rows, cols), 0) // SSM_GROUP
    cgrp = lax.broadcasted_iota(jnp.int32, (rows, cols), 1) // SSM_STATE
    diag = rgrp == cgrp
    br, bi = brt_ref[...], bit_ref[...]
    wb_ref[0] = jnp.where(diag, zr * br - zi * bi, 0.0).astype(wb_ref.dtype)
    wb_ref[1] = jnp.where(diag, zr * bi + zi * br, 0.0).astype(wb_ref.dtype)
    wc_ref[0] = jnp.where(diag, crt_ref[...], 0.0).astype(wc_ref.dtype)
    wc_ref[1] = jnp.where(diag, cit_ref[...], 0.0).astype(wc_ref.dtype)

    n = lr.shape[1]
    row = lax.broadcasted_iota(jnp.int32, (SUBLANES, n), 0)
    pr, pi = jnp.broadcast_to(ar, (SUBLANES, n)), jnp.broadcast_to(ai, (SUBLANES, n))
    a1r, a1i = pr, pi
    for k in range(3):
        keep = row >= (1 << k)
        as_re_ref[k] = jnp.where(keep, pr, 0.0)
        as_im_ref[k] = jnp.where(keep, pi, 0.0)
        pr, pi = _cmul(pr, pi, pr, pi)
    a8_re_ref[...] = pr
    a8_im_ref[...] = pi
    qr, qi = a1r, a1i
    for r in range(1, SUBLANES):
        nr, ni = _cmul(qr, qi, a1r, a1i)
        qr = jnp.where(row >= r, nr, qr)
        qi = jnp.where(row >= r, ni, qi)
    pw_re_ref[...] = qr
    pw_im_ref[...] = qi


def _s5_params(lam_re, lam_im, log_dt, b_re, b_im, c_re, c_im):
    g, p = lam_re.shape
    c = b_re.shape[2]
    n = g * p
    flat = lambda a: a.reshape(1, n)
    ldt = jnp.broadcast_to(log_dt[:, None], (g, p)).reshape(1, n)
    b_t = lambda a: jnp.tile(jnp.transpose(a, (2, 0, 1)).reshape(c, n), (g, 1))
    c_t = lambda a: jnp.tile(a.reshape(g * c, p), (1, g))
    outs = pl.pallas_call(
        _s5_params_kernel,
        out_shape=[jax.ShapeDtypeStruct((2, g * c, n), BF16), jax.ShapeDtypeStruct((2, g * c, n), BF16),
                   jax.ShapeDtypeStruct((3, SUBLANES, n), F32), jax.ShapeDtypeStruct((3, SUBLANES, n), F32),
                   jax.ShapeDtypeStruct((SUBLANES, n), F32), jax.ShapeDtypeStruct((SUBLANES, n), F32),
                   jax.ShapeDtypeStruct((SUBLANES, n), F32), jax.ShapeDtypeStruct((SUBLANES, n), F32)],
        compiler_params=pltpu.CompilerParams(vmem_limit_bytes=VMEM_LIMIT),
    )(flat(lam_re), flat(lam_im), ldt, b_t(b_re), b_t(b_im), c_t(c_re), c_t(c_im))
    wb, wct = outs[0], outs[1]
    wc = jnp.transpose(wct, (0, 2, 1))
    return (wb, wc) + tuple(outs[2:])


def _gelu_tanh(x):
    return 0.5 * x * (1.0 + jnp.tanh(math.sqrt(2.0 / math.pi) * (x + 0.044715 * (x * x * x))))


def _even_kernel(p_ref, band_ref, wpool_ref, pscale_ref, wb_ref, wc_ref, dskip_ref, wglu_ref, bglu_ref,
                 as_re_ref, as_im_ref, pw_re_ref, pw_im_ref, a8_re_ref, a8_im_ref,
                 o_ref, prev_s, xr_s, xi_s, car_re, car_im, *, mix_a, scan_lanes):
    i = pl.program_id(1)
    tt = p_ref.shape[0]

    @pl.when(i == 0)
    def _():
        prev_s[...] = jnp.zeros_like(prev_s)
        car_re[...] = jnp.zeros_like(car_re)
        car_im[...] = jnp.zeros_like(car_im)

    ua = p_ref[:, :mix_a]
    ext = jnp.concatenate([prev_s[...], ua], axis=0)
    t1 = i * tt + lax.broadcasted_iota(jnp.int32, (tt, POOL_CH), 0) + 1
    for g, win in enumerate(POOL_WINDOWS):
        sl = slice(g * POOL_CH, (g + 1) * POOL_CH)
        wsum = _dot(band_ref[g], ext[:, sl])
        cnt = jnp.minimum(t1, win).astype(F32)
        pg = wsum / cnt - ua[:, sl].astype(F32)
        yg = _dot(pg.astype(BF16), wpool_ref[g]) * pscale_ref[:, sl]
        o_ref[:, sl] = yg.astype(o_ref.dtype)
    prev_s[...] = ua

    ub = p_ref[:, mix_a:]
    xr_s[...] = _dot(ub, wb_ref[0])
    xi_s[...] = _dot(ub, wb_ref[1])
    n_state = xr_s.shape[1]
    for c in range(n_state // scan_lanes):
        sl = slice(c * scan_lanes, (c + 1) * scan_lanes)

        def block(r, carry, sl=sl):
            cr, ci = carry
            rows = pl.ds(pl.multiple_of(r * SUBLANES, SUBLANES), SUBLANES)
            hr, hi = xr_s[rows, sl], xi_s[rows, sl]
            for k in range(3):
                sr = pltpu.roll(hr, 1 << k, 0)
                si = pltpu.roll(hi, 1 << k, 0)
                dr, di = _cmul(as_re_ref[k, :, sl], as_im_ref[k, :, sl], sr, si)
                hr, hi = hr + dr, hi + di
            lr = jnp.broadcast_to(hr[SUBLANES - 1:SUBLANES, :], hr.shape)
            li = jnp.broadcast_to(hi[SUBLANES - 1:SUBLANES, :], hi.shape)
            dr, di = _cmul(pw_re_ref[:, sl], pw_im_ref[:, sl], cr, ci)
            xr_s[rows, sl] = hr + dr
            xi_s[rows, sl] = hi + di
            dr, di = _cmul(a8_re_ref[:, sl], a8_im_ref[:, sl], cr, ci)
            return lr + dr, li + di

        cr, ci = lax.fori_loop(0, tt // SUBLANES, block, (car_re[:, sl], car_im[:, sl]))
        car_re[:, sl] = cr
        car_im[:, sl] = ci

    y = _dot(xr_s[...].astype(BF16), wc_ref[0]) - _dot(xi_s[...].astype(BF16), wc_ref[1])
    y = _gelu_tanh(y + dskip_ref[...] * ub.astype(F32))
    z = _dot(y.astype(BF16), wglu_ref[...]) + bglu_ref[...]
    o_ref[:, mix_a:] = (y * (1.0 / (1.0 + jnp.exp(-z)))).astype(o_ref.dtype)


def _pool_band(tt):
    t = jnp.arange(tt)[:, None] + tt
    j = jnp.arange(2 * tt)[None, :]
    return jnp.stack([((j <= t) & (j > t - w)) for w in POOL_WINDOWS]).astype(BF16)


def _even_mixer(proj, wpool, pscale, s5, dskip, wglu, bglu, tt, scan_lanes):
    bsz, seq, width = proj.shape
    wb, wc, as_re, as_im, pw_re, pw_im, a8_re, a8_im = s5
    mix_a = len(POOL_WINDOWS) * POOL_CH
    mix_b = width - mix_a
    n_state = wb.shape[2]
    const = lambda a: pl.BlockSpec(a.shape, lambda b, i, nd=a.ndim: (0,) * nd)
    band = _pool_band(tt)
    args = (band, wpool, pscale, wb, wc, dskip, wglu, bglu, as_re, as_im, pw_re, pw_im, a8_re, a8_im)
    return pl.pallas_call(
        functools.partial(_even_kernel, mix_a=mix_a, scan_lanes=scan_lanes),
        grid=(bsz, seq // tt),
        in_specs=[pl.BlockSpec((None, tt, width), lambda b, i: (b, i, 0))] + [const(a) for a in args],
        out_specs=pl.BlockSpec((None, tt, width), lambda b, i: (b, i, 0)),
        out_shape=jax.ShapeDtypeStruct((bsz, seq, width), BF16),
        scratch_shapes=[pltpu.VMEM((tt, mix_a), BF16),
                        pltpu.VMEM((tt, n_state), F32), pltpu.VMEM((tt, n_state), F32),
                        pltpu.VMEM((SUBLANES, n_state), F32), pltpu.VMEM((SUBLANES, n_state), F32)],
        compiler_params=_cparams(("parallel", "arbitrary")),
    )(proj, *args)


def _pair_blockdiag(x, first_ref, second_ref):
    return jnp.concatenate([x * first_ref[...], x * second_ref[...]], axis=0)


def _dsa_kernel(q_ref, qi_ref, wq_ref, k_ref, v_ref, kk_ref, tri_ref, ones_ref, first_ref, second_ref, o_ref,
                s_s, qs_s, mx_s, l_s, acc_s, *, topk, qb):
    i = pl.program_id(1)
    kb_size = ATT_BLOCK
    n_diag = qb // kb_size
    nkb = (i + 1) * n_diag
    shape = (qb, kb_size)
    row = lax.broadcasted_iota(jnp.int32, shape, 0)
    col = lax.broadcasted_iota(jnp.int32, shape, 1)
    q_chunk = (i * qb + row) // CHUNK
    bcast = lambda a: jnp.broadcast_to(a, shape)
    kf = float(topk)
    pairs = [slice(p * LANES, (p + 1) * LANES) for p in range(N_HEADS // 2)]

    wq = wq_ref[...]
    w_cols = [bcast(wq[:, h:h + 1] * IDX_SCALE) for h in range(IDX_HEADS)]

    def score_block(kb, carry):
        rmax, rmin = carry
        kbd = _pair_blockdiag(kk_ref[kb], first_ref, second_ref)
        acc = jnp.zeros(shape, F32)
        for p, ps in enumerate(pairs):
            rel = jnp.maximum(_dot_nt(qi_ref[:, ps], kbd), 0.0)
            acc = acc + rel[:, :kb_size] * w_cols[2 * p] + rel[:, kb_size:] * w_cols[2 * p + 1]
        adm = ((kb * kb_size + col) // CHUNK) <= q_chunk
        s_s[kb] = jnp.where(adm, acc, -jnp.inf)
        return jnp.maximum(rmax, jnp.where(adm, acc, -jnp.inf)), jnp.minimum(rmin, jnp.where(adm, acc, jnp.inf))

    rmax, rmin = lax.fori_loop(0, nkb, score_block,
                               (jnp.full(shape, -jnp.inf, F32), jnp.full(shape, jnp.inf, F32)))
    smax, smin = _rowmax(rmax), jnp.min(rmin, axis=1, keepdims=True)

    def count(x, strict):
        parts = []
        for r0 in range(0, qb, COUNT_ROWS):
            rows = slice(r0, r0 + COUNT_ROWS)
            xb = jnp.broadcast_to(x[rows], (COUNT_ROWS, kb_size))

            def add(g, a, rows=rows, xb=xb):
                for u in range(n_diag):
                    s = s_s[g * n_diag + u, rows, :]
                    a = a + jnp.where(s > xb if strict else s >= xb, 1.0, 0.0)
                return a

            parts.append(_rowsum(lax.fori_loop(0, i + 1, add, jnp.zeros((COUNT_ROWS, kb_size), F32))))
        return jnp.concatenate(parts, axis=0)

    count_ge = functools.partial(count, strict=False)
    count_gt = functools.partial(count, strict=True)

    n_adm = ((q_chunk[:, :1] + 1) * CHUNK).astype(F32)
    hi0 = 2.0 * jnp.maximum(jnp.abs(smax), jnp.abs(smin)) + 1.0
    zero = jnp.zeros_like(smin)
    cge0, cgt0 = count_ge(zero), count_gt(zero)
    wide = n_adm > kf
    zero_tie = jnp.logical_and(wide, jnp.logical_and(cgt0 < kf, cge0 >= kf))
    above = jnp.logical_and(wide, cgt0 >= kf)
    below = jnp.logical_and(wide, cge0 < kf)
    lo0 = jnp.where(jnp.logical_or(above, zero_tie), 0.0, smin)
    clo0 = jnp.where(zero_tie, kf, jnp.where(above, cge0, n_adm))
    hi0 = jnp.where(below, 0.0, hi0)
    chi0 = jnp.where(below, cge0, 0.0)
    need0 = jnp.where(zero_tie, kf - cgt0, COUNT_ALL)

    def unresolved(clo):
        return jnp.max(jnp.where(clo > kf, 1.0, 0.0))

    def bisect_cond(carry):
        return jnp.logical_and(carry[0] > 0.0, carry[1] < BISECT_ITERS)

    def bisect(carry):
        _, it, lo, hi, clo, chi = carry
        span = clo - chi
        aim = (clo - (kf + 0.5)) / span
        interpolate = jnp.logical_and(span < INTERP_SPAN, it % 2 == 0)
        mid = lo + jnp.where(interpolate, aim, 0.5) * (hi - lo)
        c = count_ge(mid)
        active = clo > kf
        up = jnp.logical_and(active, c >= kf)
        down = jnp.logical_and(active, c < kf)
        clo = jnp.where(up, c, clo)
        return (unresolved(clo), it + 1, jnp.where(up, mid, lo), jnp.where(down, mid, hi), clo,
                jnp.where(down, c, chi))

    _, _, lo, hi, clo, chi = lax.while_loop(
        bisect_cond, bisect, (unresolved(clo0), jnp.int32(0), lo0, hi0, clo0, chi0))

    done0 = jnp.where(clo <= kf, 1.0, 0.0)

    def pop_cond(carry):
        return jnp.logical_and(carry[0] > 0.0, carry[1] < MAX_POPS)

    def pop(carry):
        _, it, hi, chi, thr, need, done = carry
        hib = bcast(hi)
        v = _rowmax(lax.fori_loop(0, nkb, lambda kb, a: jnp.maximum(a, jnp.where(s_s[kb] < hib, s_s[kb], -jnp.inf)),
                                  jnp.full(shape, -jnp.inf, F32)))
        vb = bcast(v)
        cv = _rowsum(lax.fori_loop(0, nkb, lambda kb, a: a + jnp.where(s_s[kb] == vb, 1.0, 0.0),
                                   jnp.zeros(shape, F32)))
        active = done < 0.5
        fin = jnp.logical_and(active, chi + cv >= kf)
        go = jnp.logical_and(active, chi + cv < kf)
        thr = jnp.where(fin, v, thr)
        need = jnp.where(fin, kf - chi, need)
        done = jnp.where(fin, 1.0, done)
        hi = jnp.where(go, v, hi)
        chi = jnp.where(go, chi + cv, chi)
        return (jnp.max(1.0 - done), it + 1, hi, chi, thr, need, done)

    left0 = jnp.max(1.0 - done0)
    _, _, _, _, thr, need, _ = lax.while_loop(
        pop_cond, pop, (left0, jnp.int32(0), hi, chi, lo, need0, done0))
    thrb, needb = bcast(thr), bcast(need)
    partial_ties = jnp.max(jnp.where(need < COUNT_ALL, 1.0, 0.0))

    @pl.when(partial_ties > 0.0)
    def _():
        def tie_block(kb, run):
            s = s_s[kb]
            eq = jnp.where(s == thrb, 1.0, 0.0)
            rank = _dot(eq.astype(BF16), tri_ref[1]) + run
            take = jnp.where(s == thrb, jnp.where(rank < needb, 0.0, NEG_BIG), NEG_BIG)
            s_s[kb] = jnp.where(s > thrb, 0.0, take)
            return run + bcast(_rowsum(eq))
        lax.fori_loop(0, nkb, tie_block, jnp.zeros(shape, F32))

    @pl.when(partial_ties <= 0.0)
    def _():
        def sel_block(kb, carry):
            s_s[kb] = jnp.where(s_s[kb] >= thrb, 0.0, NEG_BIG)
            return carry
        lax.fori_loop(0, nkb, sel_block, 0)

    rel_pos = (row - col).astype(F32)
    first_half = col < HEAD_DIM
    qs_s[...] = q_ref[...] * (HEAD_DIM ** -0.5)
    mx_s[...] = jnp.full(mx_s.shape, NEG_BIG, F32)
    l_s[...] = jnp.zeros_like(l_s)
    acc_s[...] = jnp.zeros_like(acc_s)
    slopes = [2.0 ** (-8.0 * (h + 1) / N_HEADS) for h in range(N_HEADS)]

    def attend(kb, carry):
        dist = jnp.abs(rel_pos + (i * qb - kb * kb_size).astype(F32))
        mask = s_s[kb]
        scores = []
        for p, ps in enumerate(pairs):
            kbd = _pair_blockdiag(k_ref[kb, :, ps], first_ref, second_ref)
            bias = jnp.concatenate([mask - slopes[2 * p] * dist, mask - slopes[2 * p + 1] * dist], axis=1)
            scores.append(_dot_nt(qs_s[:, ps], kbd) + bias)
        for p, ps in enumerate(pairs):
            s = scores[p]
            m_old = mx_s[p]
            m_new = jnp.maximum(m_old, jnp.concatenate(
                [bcast(_rowmax(s[:, :kb_size])), bcast(_rowmax(s[:, kb_size:]))], axis=1))
            mx_s[p] = m_new
            pexp = jnp.exp((s - m_new).astype(BF16))
            shrink = m_old - m_new
            alpha = jnp.exp(jnp.where(first_half, shrink[:, :kb_size], shrink[:, kb_size:]))
            vbd = _pair_blockdiag(v_ref[kb, :, ps], first_ref, second_ref)
            l_s[p] = alpha * l_s[p] + _dot(pexp, ones_ref[...])
            acc_s[p] = alpha * acc_s[p] + _dot(pexp, vbd)
        return carry

    lax.fori_loop(0, nkb, attend, 0)
    for p, ps in enumerate(pairs):
        o_ref[:, ps] = (acc_s[p] / l_s[p]).astype(o_ref.dtype)


def _tri_consts(n):
    j = jnp.arange(n)[:, None]
    s = jnp.arange(n)[None, :]
    return jnp.stack([j > s, j < s]).astype(BF16)


def _pair_consts(n):
    tri = _tri_consts(n)
    zero = jnp.zeros((n, n), BF16)
    tri2 = jnp.block([[tri[0], zero], [zero, tri[0]]])
    first = (jnp.arange(n)[None, :] < n // 2) & jnp.ones((n, 1), bool)
    ones2 = jnp.concatenate([first, ~first], axis=0).astype(BF16)
    return tri2, ones2, first.astype(BF16), (~first).astype(BF16)


def _dsa(q, k, v, qi, kk, wq, topk, qb):
    bsz, seq, width = q.shape
    nq, nk = seq // qb, seq // ATT_BLOCK
    qblk = lambda a: a.reshape(bsz, nq, qb, a.shape[-1])
    kblk = lambda a: a.reshape(bsz, nk, ATT_BLOCK, a.shape[-1])
    qspec = lambda wd: pl.BlockSpec((None, None, qb, wd), lambda b, i: (b, i, 0, 0))
    kspec = lambda wd: pl.BlockSpec((None, nk, ATT_BLOCK, wd), lambda b, i: (b, 0, 0, 0))
    _, ones2, first, second = _pair_consts(ATT_BLOCK)
    consts = (_tri_consts(ATT_BLOCK), ones2, first, second)
    out = pl.pallas_call(
        functools.partial(_dsa_kernel, topk=topk, qb=qb),
        grid=(bsz, nq),
        in_specs=[qspec(width), qspec(width), qspec(LANES), kspec(width), kspec(width), kspec(LANES)]
                 + [pl.BlockSpec(a.shape, lambda b, i, nd=a.ndim: (0,) * nd) for a in consts],
        out_specs=qspec(width),
        out_shape=jax.ShapeDtypeStruct((bsz, nq, qb, width), BF16),
        scratch_shapes=[pltpu.VMEM((nk, qb, ATT_BLOCK), F32),
                        pltpu.VMEM((qb, width), BF16),
                        pltpu.VMEM((N_HEADS // 2, qb, 2 * ATT_BLOCK), F32),
                        pltpu.VMEM((N_HEADS // 2, qb, LANES), F32),
                        pltpu.VMEM((N_HEADS // 2, qb, LANES), F32)],
        compiler_params=_cparams(("parallel", "arbitrary")),
    )(qblk(q), qblk(qi), qblk(wq), kblk(k), kblk(v), kblk(kk), *consts)
    return out.reshape(bsz, seq, width)


def _sb_kernel(q_ref, k_ref, v_ref, tri_ref, ones_ref, first_ref, second_ref, o_ref, qs_s, run_s, acc_s, *, qb):
    i = pl.program_id(1)
    kb_size = ATT_BLOCK
    n_diag = qb // kb_size
    n_pairs = N_HEADS // 2
    shape = (qb, 2 * kb_size)
    row = lax.broadcasted_iota(jnp.int32, shape, 0)
    col = lax.broadcasted_iota(jnp.int32, shape, 1) % kb_size
    qs_s[...] = q_ref[...] * (HEAD_DIM ** -0.5)
    run_s[...] = jnp.zeros_like(run_s)
    acc_s[...] = jnp.zeros_like(acc_s)

    def step(kb, masked):
        causal = (kb * kb_size + col) < (i * qb + row)
        pairs = [slice(p * LANES, (p + 1) * LANES) for p in range(n_pairs)]
        log_sig, log_keep = [], []
        for ps in pairs:
            kbd = _pair_blockdiag(k_ref[kb, :, ps], first_ref, second_ref)
            z = _dot_nt(qs_s[:, ps], kbd).astype(BF16)
            soft = jnp.log(1.0 + jnp.exp(-jnp.abs(z)))
            keep = -jnp.maximum(z, 0.0) - soft
            log_sig.append(jnp.minimum(z, 0.0) - soft)
            log_keep.append(jnp.where(causal, keep, jnp.zeros_like(keep)) if masked else keep)
        for p, ps in enumerate(pairs):
            after = _dot(log_keep[p], tri_ref[...]).astype(BF16)
            w = jnp.exp(log_sig[p] + after)
            if masked:
                w = jnp.where(causal, w, jnp.zeros_like(w))
            vbd = _pair_blockdiag(v_ref[kb, :, ps], first_ref, second_ref)
            acc_s[p] += jnp.exp(run_s[p]) * _dot(w, vbd)
            run_s[p] += _dot(log_keep[p], ones_ref[...])

    last = (i + 1) * n_diag - 1

    def diag_step(j, carry):
        step(last - j, True)
        return carry

    def past_step(j, carry):
        step(i * n_diag - 1 - j, False)
        return carry

    lax.fori_loop(0, n_diag, diag_step, 0)
    lax.fori_loop(0, i * n_diag, past_step, 0)
    for p in range(n_pairs):
        o_ref[:, p * LANES:(p + 1) * LANES] = acc_s[p].astype(o_ref.dtype)


def _stick_breaking(q, k, v, qb):
    bsz, seq, width = q.shape
    nq, nk = seq // qb, seq // ATT_BLOCK
    qspec = pl.BlockSpec((None, None, qb, width), lambda b, i: (b, i, 0, 0))
    kspec = pl.BlockSpec((None, nk, ATT_BLOCK, width), lambda b, i: (b, 0, 0, 0))
    consts = _pair_consts(ATT_BLOCK)
    out = pl.pallas_call(
        functools.partial(_sb_kernel, qb=qb),
        grid=(bsz, nq),
        in_specs=[qspec, kspec, kspec] + [pl.BlockSpec(a.shape, lambda b, i: (0, 0)) for a in consts],
        out_specs=qspec,
        out_shape=jax.ShapeDtypeStruct((bsz, nq, qb, width), BF16),
        scratch_shapes=[pltpu.VMEM((qb, width), BF16),
                        pltpu.VMEM((N_HEADS // 2, qb, LANES), F32),
                        pltpu.VMEM((N_HEADS // 2, qb, LANES), F32)],
        compiler_params=_cparams(("parallel", "arbitrary")),
    )(q.reshape(bsz, nq, qb, width), k.reshape(bsz, nk, ATT_BLOCK, width),
      v.reshape(bsz, nk, ATT_BLOCK, width), *consts)
    return out.reshape(bsz, seq, width)


def _cd_weight(cd_w_in):
    cw = N_HEADS * HEAD_DIM
    iw = IDX_HEADS * IDX_DIM
    offs = [0, cw, 2 * cw, 3 * cw, 3 * cw + iw, 3 * cw + iw + IDX_DIM, 3 * cw + iw + IDX_DIM + IDX_HEADS]
    qkv_c = cd_w_in[:, :offs[3]]
    q_idx = cd_w_in[:, offs[3]:offs[4]]
    k_idx = cd_w_in[:, offs[4]:offs[5]]
    w_idx = cd_w_in[:, offs[5]:offs[6]]
    qkv_d = cd_w_in[:, offs[6]:]
    pad = jnp.zeros((cd_w_in.shape[0], LANES - w_idx.shape[1]), cd_w_in.dtype)
    return jnp.concatenate([qkv_c, q_idx, qkv_d, k_idx, k_idx, w_idx, pad], axis=1)


def kernel(x, c, norm_g, ada_w, ada_b, mlp_w1, mlp_w2, ab_w_in, ab_w_out, pool_w, pool_scale, ssm_lam_re, ssm_lam_im, ssm_log_dt, ssm_b_re, ssm_b_im, ssm_c_re, ssm_c_im, ssm_d, glu_w, glu_b, cd_w_in, cd_w_out, final_g):
    bsz, seq, d = x.shape
    depth = norm_g.shape[0]
    topk = min(TOPK_MAX, seq // 4)
    tm = min(512, seq)

    c_pad = jnp.zeros((SUBLANES, d), F32).at[:bsz].set(c)
    mod = _modulation(c_pad, ada_w.reshape(depth * 2, d, 3 * d), ada_b.reshape(depth * 2, 1, 3 * d))
    mod = mod[:, :, :bsz, None, :]

    for i in range(depth):
        j = i // 2
        shift, scale1, gate1 = mod[2 * i, 0], mod[2 * i, 1], mod[2 * i, 2]
        g = norm_g[i, 0][None, :]
        if i % 2 == 0:
            width = ab_w_in.shape[2]
            (proj,) = _norm_proj(x, g, shift, scale1, ab_w_in[j].astype(BF16), (width,), (BF16,), tm)
            s5 = _s5_params(ssm_lam_re[j], ssm_lam_im[j], ssm_log_dt[j],
                            ssm_b_re[j], ssm_b_im[j], ssm_c_re[j], ssm_c_im[j])
            y = _even_mixer(proj, pool_w[j].astype(BF16), pool_scale[j][None, :], s5,
                            ssm_d[j][None, :], glu_w[j].astype(BF16), glu_b[j][None, :],
                            tt=min(256, seq), scan_lanes=512)
            x = _out_proj(x, gate1, (y,), ab_w_out[j].astype(BF16), tm)
        else:
            hw = N_HEADS * HEAD_DIM
            widths = (hw,) * 7 + (LANES, LANES)
            dtypes = (BF16,) * 8 + (F32,)
            qc, kc, vc, qi, qd, kd, vd, kk, wq = _norm_proj(
                x, g, shift, scale1, _cd_weight(cd_w_in[j]).astype(BF16), widths, dtypes, tm)
            oc = _dsa(qc, kc, vc, qi, kk, wq, topk, qb=ATT_QUERY_BLOCK)
            od = _stick_breaking(qd, kd, vd, qb=ATT_QUERY_BLOCK)
            x = _out_proj(x, gate1, (oc, od), cd_w_out[j].astype(BF16), tm)
        shift, scale1, gate1 = mod[2 * i + 1, 0], mod[2 * i + 1, 1], mod[2 * i + 1, 2]
        x = _mlp(x, norm_g[i, 1][None, :], shift, scale1, gate1,
                 mlp_w1[i].astype(BF16), mlp_w2[i].astype(BF16), final_g[None, :],
                 final_norm=(i == depth - 1), tm=min(1024, seq), tf=512)
    return x
```

```python
import functools
import math

import jax
import jax.numpy as jnp
from jax import lax
from jax.experimental import pallas as pl
from jax.experimental.pallas import tpu as pltpu

F32 = jnp.float32
BF16 = jnp.bfloat16

EPS = 1e-6
CHUNK = 64
POOL_WINDOWS = (2, 4, 8, 16)
POOL_CH = 128
SSM_GROUP = 16
SSM_STATE = 64
HEAD_DIM = 64
N_HEADS = 8
IDX_HEADS = 8
IDX_DIM = 64
IDX_SCALE = (IDX_HEADS ** -0.5) * (IDX_DIM ** -0.5)
TOPK_MAX = 256

LANES = 128
SUBLANES = 8
VMEM_LIMIT = 56 * 1024 * 1024

ATT_BLOCK = 128
ATT_QUERY_BLOCK = 512
NEG_BIG = -1e30
COUNT_ALL = 1e9
BISECT_ITERS = 28
COUNT_ROWS = 128
INTERP_SPAN = 256.0
MAX_POPS = 4096


def _cparams(sem):
    return pltpu.CompilerParams(dimension_semantics=sem, vmem_limit_bytes=VMEM_LIMIT)


def _dot(a, b):
    return jnp.dot(a, b, preferred_element_type=F32)


def _dot_nt(a, b):
    return lax.dot_general(a, b, (((1,), (1,)), ((), ())), preferred_element_type=F32)


def _rowsum(x):
    return jnp.sum(x, axis=1, keepdims=True)


def _rowmax(x):
    return jnp.max(x, axis=1, keepdims=True)


def _norm_modulate(x, g, shift, scale1):
    ms = jnp.mean(x * x, axis=-1, keepdims=True)
    return (x * lax.rsqrt(ms + EPS)) * g * scale1 + shift


def _mod_kernel(c_ref, w_ref, b_ref, o_ref):
    j = pl.program_id(1)
    m = _dot(c_ref[...], w_ref[...]) + b_ref[...]
    o_ref[...] = m + jnp.where(j > 0, 1.0, 0.0)


def _modulation(c_pad, ada_w, ada_b):
    n_sub, d, _ = ada_w.shape
    rows = c_pad.shape[0]
    return pl.pallas_call(
        _mod_kernel,
        grid=(n_sub, 3),
        in_specs=[
            pl.BlockSpec((rows, d), lambda s, j: (0, 0)),
            pl.BlockSpec((None, d, d), lambda s, j: (s, 0, j)),
            pl.BlockSpec((None, 1, d), lambda s, j: (s, 0, j)),
        ],
        out_specs=pl.BlockSpec((None, None, rows, d), lambda s, j: (s, j, 0, 0)),
        out_shape=jax.ShapeDtypeStruct((n_sub, 3, rows, d), F32),
        compiler_params=_cparams(("arbitrary", "arbitrary")),
    )(c_pad, ada_w, ada_b)


def _norm_proj_kernel(x_ref, g_ref, shift_ref, scale_ref, w_ref, *o_refs, widths):
    h = _norm_modulate(x_ref[...], g_ref[...], shift_ref[...], scale_ref[...]).astype(BF16)
    y = _dot(h, w_ref[...])
    off = 0
    for o_ref, width in zip(o_refs, widths):
        o_ref[...] = y[:, off:off + width].astype(o_ref.dtype)
        off += width


def _norm_proj(x, g, shift, scale1, w, widths, dtypes, tm):
    bsz, seq, d = x.shape
    n = w.shape[1]
    vec = pl.BlockSpec((None, 1, d), lambda b, i: (b, 0, 0))
    return pl.pallas_call(
        functools.partial(_norm_proj_kernel, widths=widths),
        grid=(bsz, seq // tm),
        in_specs=[
            pl.BlockSpec((None, tm, d), lambda b, i: (b, i, 0)),
            pl.BlockSpec((1, d), lambda b, i: (0, 0)),
            vec, vec,
            pl.BlockSpec((d, n), lambda b, i: (0, 0)),
        ],
        out_specs=[pl.BlockSpec((None, tm, wd), lambda b, i: (b, i, 0)) for wd in widths],
        out_shape=[jax.ShapeDtypeStruct((bsz, seq, wd), dt) for wd, dt in zip(widths, dtypes)],
        compiler_params=_cparams(("parallel", "parallel")),
    )(x, g, shift, scale1, w)


def _out_proj_kernel(x_ref, gate_ref, *refs, n_y):
    y_refs, w_ref, o_ref = refs[:n_y], refs[n_y], refs[n_y + 1]
    y = jnp.concatenate([r[...] for r in y_refs], axis=1) if n_y > 1 else y_refs[0][...]
    o_ref[...] = x_ref[...] + gate_ref[...] * _dot(y, w_ref[...])


def _out_proj(x, gate1, ys, w, tm):
    bsz, seq, d = x.shape
    k = w.shape[0]
    return pl.pallas_call(
        functools.partial(_out_proj_kernel, n_y=len(ys)),
        grid=(bsz, seq // tm),
        in_specs=[
            pl.BlockSpec((None, tm, d), lambda b, i: (b, i, 0)),
            pl.BlockSpec((None, 1, d), lambda b, i: (b, 0, 0)),
            *[pl.BlockSpec((None, tm, y.shape[2]), lambda b, i: (b, i, 0)) for y in ys],
            pl.BlockSpec((k, d), lambda b, i: (0, 0)),
        ],
        out_specs=pl.BlockSpec((None, tm, d), lambda b, i: (b, i, 0)),
        out_shape=jax.ShapeDtypeStruct((bsz, seq, d), F32),
        compiler_params=_cparams(("parallel", "parallel")),
    )(x, gate1, *ys, w)


def _mlp_kernel(x_ref, g_ref, shift_ref, scale_ref, gate_ref, w1_ref, w2_ref, fg_ref, o_ref,
                h_s, acc_s, *, final_norm):
    f = pl.program_id(2)

    @pl.when(f == 0)
    def _():
        h_s[...] = _norm_modulate(x_ref[...], g_ref[...], shift_ref[...], scale_ref[...]).astype(BF16)
        acc_s[...] = jnp.zeros_like(acc_s)

    a = jnp.maximum(_dot(h_s[...], w1_ref[...]), 0.0)
    acc_s[...] += _dot((a * a).astype(BF16), w2_ref[...])

    @pl.when(f == pl.num_programs(2) - 1)
    def _():
        y = x_ref[...] + gate_ref[...] * acc_s[...]
        if final_norm:
            ms = jnp.mean(y * y, axis=-1, keepdims=True)
            y = (y * lax.rsqrt(ms + EPS)) * fg_ref[...]
        o_ref[...] = y


def _mlp(x, g, shift, scale1, gate1, w1, w2, final_g, final_norm, tm, tf):
    bsz, seq, d = x.shape
    dff = w1.shape[1]
    vec = pl.BlockSpec((None, 1, d), lambda b, i, f: (b, 0, 0))
    one = pl.BlockSpec((1, d), lambda b, i, f: (0, 0))
    return pl.pallas_call(
        functools.partial(_mlp_kernel, final_norm=final_norm),
        grid=(bsz, seq // tm, dff // tf),
        in_specs=[
            pl.BlockSpec((None, tm, d), lambda b, i, f: (b, i, 0)),
            one, vec, vec, vec,
            pl.BlockSpec((d, tf), lambda b, i, f: (0, f)),
            pl.BlockSpec((tf, d), lambda b, i, f: (f, 0)),
            one,
        ],
        out_specs=pl.BlockSpec((None, tm, d), lambda b, i, f: (b, i, 0)),
        out_shape=jax.ShapeDtypeStruct((bsz, seq, d), F32),
        scratch_shapes=[pltpu.VMEM((tm, d), BF16), pltpu.VMEM((tm, d), F32)],
        compiler_params=_cparams(("parallel", "parallel", "arbitrary")),
    )(x, g, shift, scale1, gate1, w1, w2, final_g)


def _cmul(ar, ai, br, bi):
    return ar * br - ai * bi, ar * bi + ai * br


def _s5_params_kernel(lr_ref, li_ref, ldt_ref, brt_ref, bit_ref, crt_ref, cit_ref,
                      wb_ref, wc_ref, as_re_ref, as_im_ref, pw_re_ref, pw_im_ref, a8_re_ref, a8_im_ref):
    lr, li = lr_ref[...], li_ref[...]
    dt = jnp.exp(ldt_ref[...])
    mag = jnp.exp(lr * dt)
    ar, ai = mag * jnp.cos(li * dt), mag * jnp.sin(li * dt)
    den = lr * lr + li * li
    zr = ((ar - 1.0) * lr + ai * li) / den
    zi = (ai * lr - (ar - 1.0) * li) / den

    rows, cols = brt_ref.shape
    rgrp = lax.broadcasted_iota(jnp.int32, (rows, cols), 0) // SSM_GROUP
    cgrp = lax.broadcasted_iota(jnp.int32, (rows, cols), 1) // SSM_STATE
    diag = rgrp == cgrp
    br, bi = brt_ref[...], bit_ref[...]
    wb_ref[0] = jnp.where(diag, zr * br - zi * bi, 0.0).astype(wb_ref.dtype)
    wb_ref[1] = jnp.where(diag, zr * bi + zi * br, 0.0).astype(wb_ref.dtype)
    wc_ref[0] = jnp.where(diag, crt_ref[...], 0.0).astype(wc_ref.dtype)
    wc_ref[1] = jnp.where(diag, cit_ref[...], 0.0).astype(wc_ref.dtype)

    n = lr.shape[1]
    row = lax.broadcasted_iota(jnp.int32, (SUBLANES, n), 0)
    pr, pi = jnp.broadcast_to(ar, (SUBLANES, n)), jnp.broadcast_to(ai, (SUBLANES, n))
    a1r, a1i = pr, pi
    for k in range(3):
        keep = row >= (1 << k)
        as_re_ref[k] = jnp.where(keep, pr, 0.0)
        as_im_ref[k] = jnp.where(keep, pi, 0.0)
        pr, pi = _cmul(pr, pi, pr, pi)
    a8_re_ref[...] = pr
    a8_im_ref[...] = pi
    qr, qi = a1r, a1i
    for r in range(1, SUBLANES):
        nr, ni = _cmul(qr, qi, a1r, a1i)
        qr = jnp.where(row >= r, nr, qr)
        qi = jnp.where(row >= r, ni, qi)
    pw_re_ref[...] = qr
    pw_im_ref[...] = qi


def _s5_params(lam_re, lam_im, log_dt, b_re, b_im, c_re, c_im):
    g, p = lam_re.shape
    c = b_re.shape[2]
    n = g * p
    flat = lambda a: a.reshape(1, n)
    ldt = jnp.broadcast_to(log_dt[:, None], (g, p)).reshape(1, n)
    b_t = lambda a: jnp.tile(jnp.transpose(a, (2, 0, 1)).reshape(c, n), (g, 1))
    c_t = lambda a: jnp.tile(a.reshape(g * c, p), (1, g))
    outs = pl.pallas_call(
        _s5_params_kernel,
        out_shape=[jax.ShapeDtypeStruct((2, g * c, n), BF16), jax.ShapeDtypeStruct((2, g * c, n), BF16),
                   jax.ShapeDtypeStruct((3, SUBLANES, n), F32), jax.ShapeDtypeStruct((3, SUBLANES, n), F32),
                   jax.ShapeDtypeStruct((SUBLANES, n), F32), jax.ShapeDtypeStruct((SUBLANES, n), F32),
                   jax.ShapeDtypeStruct((SUBLANES, n), F32), jax.ShapeDtypeStruct((SUBLANES, n), F32)],
        compiler_params=pltpu.CompilerParams(vmem_limit_bytes=VMEM_LIMIT),
    )(flat(lam_re), flat(lam_im), ldt, b_t(b_re), b_t(b_im), c_t(c_re), c_t(c_im))
    wb, wct = outs[0], outs[1]
    wc = jnp.transpose(wct, (0, 2, 1))
    return (wb, wc) + tuple(outs[2:])


def _gelu_tanh(x):
    return 0.5 * x * (1.0 + jnp.tanh(math.sqrt(2.0 / math.pi) * (x + 0.044715 * (x * x * x))))


def _even_kernel(p_ref, band_ref, wpool_ref, pscale_ref, wb_ref, wc_ref, dskip_ref, wglu_ref, bglu_ref,
                 as_re_ref, as_im_ref, pw_re_ref, pw_im_ref, a8_re_ref, a8_im_ref,
                 o_ref, prev_s, xr_s, xi_s, car_re, car_im, *, mix_a, scan_lanes):
    i = pl.program_id(1)
    tt = p_ref.shape[0]

    @pl.when(i == 0)
    def _():
        prev_s[...] = jnp.zeros_like(prev_s)
        car_re[...] = jnp.zeros_like(car_re)
        car_im[...] = jnp.zeros_like(car_im)

    ua = p_ref[:, :mix_a]
    ext = jnp.concatenate([prev_s[...], ua], axis=0)
    t1 = i * tt + lax.broadcasted_iota(jnp.int32, (tt, POOL_CH), 0) + 1
    for g, win in enumerate(POOL_WINDOWS):
        sl = slice(g * POOL_CH, (g + 1) * POOL_CH)
        wsum = _dot(band_ref[g], ext[:, sl])
        cnt = jnp.minimum(t1, win).astype(F32)
        pg = wsum / cnt - ua[:, sl].astype(F32)
        yg = _dot(pg.astype(BF16), wpool_ref[g]) * pscale_ref[:, sl]
        o_ref[:, sl] = yg.astype(o_ref.dtype)
    prev_s[...] = ua

    ub = p_ref[:, mix_a:]
    n_state = xr_s.shape[1]
    n_diag_blocks = ub.shape[1] // LANES
    sw = n_state // n_diag_blocks
    for j in range(n_diag_blocks):
        cs, ss = slice(j * LANES, (j + 1) * LANES), slice(j * sw, (j + 1) * sw)
        xr_s[:, ss] = _dot(ub[:, cs], wb_ref[0, cs, ss])
        xi_s[:, ss] = _dot(ub[:, cs], wb_ref[1, cs, ss])
    for c in range(n_state // scan_lanes):
        sl = slice(c * scan_lanes, (c + 1) * scan_lanes)

        def block(r, carry, sl=sl):
            cr, ci = carry
            rows = pl.ds(pl.multiple_of(r * SUBLANES, SUBLANES), SUBLANES)
            hr, hi = xr_s[rows, sl], xi_s[rows, sl]
            for k in range(3):
                sr = pltpu.roll(hr, 1 << k, 0)
                si = pltpu.roll(hi, 1 << k, 0)
                dr, di = _cmul(as_re_ref[k, :, sl], as_im_ref[k, :, sl], sr, si)
                hr, hi = hr + dr, hi + di
            lr = jnp.broadcast_to(hr[SUBLANES - 1:SUBLANES, :], hr.shape)
            li = jnp.broadcast_to(hi[SUBLANES - 1:SUBLANES, :], hi.shape)
            dr, di = _cmul(pw_re_ref[:, sl], pw_im_ref[:, sl], cr, ci)
            xr_s[rows, sl] = hr + dr
            xi_s[rows, sl] = hi + di
            dr, di = _cmul(a8_re_ref[:, sl], a8_im_ref[:, sl], cr, ci)
            return lr + dr, li + di

        cr, ci = lax.fori_loop(0, tt // SUBLANES, block, (car_re[:, sl], car_im[:, sl]))
        car_re[:, sl] = cr
        car_im[:, sl] = ci

    y = jnp.concatenate(
        [_dot(xr_s[:, j * sw:(j + 1) * sw].astype(BF16), wc_ref[0, j * sw:(j + 1) * sw, j * LANES:(j + 1) * LANES])
         - _dot(xi_s[:, j * sw:(j + 1) * sw].astype(BF16), wc_ref[1, j * sw:(j + 1) * sw, j * LANES:(j + 1) * LANES])
         for j in range(n_diag_blocks)], axis=1)
    y = _gelu_tanh(y + dskip_ref[...] * ub.astype(F32))
    z = _dot(y.astype(BF16), wglu_ref[...]) + bglu_ref[...]
    o_ref[:, mix_a:] = (y * (1.0 / (1.0 + jnp.exp(-z)))).astype(o_ref.dtype)


def _pool_band(tt):
    t = jnp.arange(tt)[:, None] + tt
    j = jnp.arange(2 * tt)[None, :]
    return jnp.stack([((j <= t) & (j > t - w)) for w in POOL_WINDOWS]).astype(BF16)


def _even_mixer(proj, wpool, pscale, s5, dskip, wglu, bglu, tt, scan_lanes):
    bsz, seq, width = proj.shape
    wb, wc, as_re, as_im, pw_re, pw_im, a8_re, a8_im = s5
    mix_a = len(POOL_WINDOWS) * POOL_CH
    mix_b = width - mix_a
    n_state = wb.shape[2]
    const = lambda a: pl.BlockSpec(a.shape, lambda b, i, nd=a.ndim: (0,) * nd)
    band = _pool_band(tt)
    args = (band, wpool, pscale, wb, wc, dskip, wglu, bglu, as_re, as_im, pw_re, pw_im, a8_re, a8_im)
    return pl.pallas_call(
        functools.partial(_even_kernel, mix_a=mix_a, scan_lanes=scan_lanes),
        grid=(bsz, seq // tt),
        in_specs=[pl.BlockSpec((None, tt, width), lambda b, i: (b, i, 0))] + [const(a) for a in args],
        out_specs=pl.BlockSpec((None, tt, width), lambda b, i: (b, i, 0)),
        out_shape=jax.ShapeDtypeStruct((bsz, seq, width), BF16),
        scratch_shapes=[pltpu.VMEM((tt, mix_a), BF16),
                        pltpu.VMEM((tt, n_state), F32), pltpu.VMEM((tt, n_state), F32),
                        pltpu.VMEM((SUBLANES, n_state), F32), pltpu.VMEM((SUBLANES, n_state), F32)],
        compiler_params=_cparams(("parallel", "arbitrary")),
    )(proj, *args)


def _pair_blockdiag(x, first_ref, second_ref):
    return jnp.concatenate([x * first_ref[...], x * second_ref[...]], axis=0)


def _dsa_kernel(q_ref, qi_ref, wq_ref, k_ref, v_ref, kk_ref, tri_ref, ones_ref, first_ref, second_ref, o_ref,
                s_s, qs_s, mx_s, al_s, *, topk, qb):
    i = pl.program_id(1)
    kb_size = ATT_BLOCK
    n_diag = qb // kb_size
    nkb = (i + 1) * n_diag
    shape = (qb, kb_size)
    row = lax.broadcasted_iota(jnp.int32, shape, 0)
    col = lax.broadcasted_iota(jnp.int32, shape, 1)
    q_chunk = (i * qb + row) // CHUNK
    bcast = lambda a: jnp.broadcast_to(a, shape)
    kf = float(topk)
    pairs = [slice(p * LANES, (p + 1) * LANES) for p in range(N_HEADS // 2)]

    wq = wq_ref[...]
    w_cols = [bcast(wq[:, h:h + 1] * IDX_SCALE) for h in range(IDX_HEADS)]

    def score_block(kb, carry):
        rmax, rmin = carry
        kbd = _pair_blockdiag(kk_ref[kb], first_ref, second_ref)
        acc = jnp.zeros(shape, F32)
        for p, ps in enumerate(pairs):
            rel = jnp.maximum(_dot_nt(qi_ref[:, ps], kbd), 0.0)
            acc = acc + rel[:, :kb_size] * w_cols[2 * p] + rel[:, kb_size:] * w_cols[2 * p + 1]
        adm = ((kb * kb_size + col) // CHUNK) <= q_chunk
        s_s[kb] = jnp.where(adm, acc, -jnp.inf)
        return jnp.maximum(rmax, jnp.where(adm, acc, -jnp.inf)), jnp.minimum(rmin, jnp.where(adm, acc, jnp.inf))

    rmax, rmin = lax.fori_loop(0, nkb, score_block,
                               (jnp.full(shape, -jnp.inf, F32), jnp.full(shape, jnp.inf, F32)))
    n_chunks = qb // kb_size
    chunk_rows = lambda c: slice(c * kb_size, (c + 1) * kb_size)
    chunk_subl = lambda c: slice(c * SUBLANES, (c + 1) * SUBLANES)
    ones8 = jnp.ones((SUBLANES, kb_size), BF16)
    tile = (kb_size, kb_size)

    def to_dense(rep):
        return rep.T[:SUBLANES]

    def to_rows(dense):
        return jnp.broadcast_to(dense[:1], tile).T

    def dense_of(fn):
        return jnp.concatenate([fn(c) for c in range(n_chunks)], axis=0)

    def fold_keys(c, fn, init):
        def body(g, a):
            for u in range(n_diag):
                a = fn(a, s_s[g * n_diag + u, chunk_rows(c), :])
            return a
        return lax.fori_loop(0, i + 1, body, init)

    def count(x, cmp):
        xbs = [to_rows(x[chunk_subl(c)]) for c in range(n_chunks)]
        cnts = [fold_keys(c, lambda a, s, xb=xbs[c]: a + jnp.where(cmp(s, xb), 1.0, 0.0), jnp.zeros(tile, F32))
                for c in range(n_chunks)]
        return jnp.concatenate([_dot_nt(ones8, cnt.astype(BF16)) for cnt in cnts], axis=0)

    count_ge = functools.partial(count, cmp=lambda s, xb: s >= xb)
    count_gt = functools.partial(count, cmp=lambda s, xb: s > xb)
    count_eq = functools.partial(count, cmp=lambda s, xb: s == xb)

    smax = dense_of(lambda c: to_dense(jnp.broadcast_to(_rowmax(rmax[chunk_rows(c)]), tile)))
    smin = dense_of(lambda c: to_dense(jnp.broadcast_to(jnp.min(rmin[chunk_rows(c)], axis=1, keepdims=True), tile)))
    dshape = (n_chunks * SUBLANES, kb_size)
    t_dense = (i * qb + (lax.broadcasted_iota(jnp.int32, dshape, 0) // SUBLANES) * kb_size
               + lax.broadcasted_iota(jnp.int32, dshape, 1))
    n_adm = ((t_dense // CHUNK + 1) * CHUNK).astype(F32)
    hi0 = 2.0 * jnp.maximum(jnp.abs(smax), jnp.abs(smin)) + 1.0
    zero = jnp.zeros_like(smin)
    cge0, cgt0 = count_ge(zero), count_gt(zero)
    wide = n_adm > kf
    zero_tie = jnp.logical_and(wide, jnp.logical_and(cgt0 < kf, cge0 >= kf))
    above = jnp.logical_and(wide, cgt0 >= kf)
    below = jnp.logical_and(wide, cge0 < kf)
    lo0 = jnp.where(jnp.logical_or(above, zero_tie), 0.0, smin)
    clo0 = jnp.where(zero_tie, kf, jnp.where(above, cge0, n_adm))
    hi0 = jnp.where(below, 0.0, hi0)
    chi0 = jnp.where(below, cge0, 0.0)
    need0 = jnp.where(zero_tie, kf - cgt0, COUNT_ALL)

    def unresolved(clo):
        return jnp.max(jnp.where(clo > kf, 1.0, 0.0))

    def bisect_cond(carry):
        return jnp.logical_and(carry[0] > 0.0, carry[1] < BISECT_ITERS)

    def bisect(carry):
        _, it, lo, hi, clo, chi = carry
        span = clo - chi
        aim = (clo - (kf + 0.5)) / span
        interpolate = jnp.logical_and(span < INTERP_SPAN, it % 2 == 0)
        mid = lo + jnp.where(interpolate, aim, 0.5) * (hi - lo)
        c = count_ge(mid)
        active = clo > kf
        up = jnp.logical_and(active, c >= kf)
        down = jnp.logical_and(active, c < kf)
        clo = jnp.where(up, c, clo)
        return (unresolved(clo), it + 1, jnp.where(up, mid, lo), jnp.where(down, mid, hi), clo,
                jnp.where(down, c, chi))

    _, _, lo, hi, clo, chi = lax.while_loop(
        bisect_cond, bisect, (unresolved(clo0), jnp.int32(0), lo0, hi0, clo0, chi0))

    done0 = jnp.where(clo <= kf, 1.0, 0.0)

    def pop_cond(carry):
        return jnp.logical_and(carry[0] > 0.0, carry[1] < MAX_POPS)

    def next_below(hi):
        def one(c):
            hib = to_rows(hi[chunk_subl(c)])
            best = fold_keys(c, lambda a, s: jnp.maximum(a, jnp.where(s < hib, s, -jnp.inf)),
                             jnp.full(tile, -jnp.inf, F32))
            return to_dense(jnp.broadcast_to(_rowmax(best), tile))
        return dense_of(one)

    def pop(carry):
        _, it, hi, chi, thr, need, done = carry
        v = next_below(hi)
        cv = count_eq(v)
        active = done < 0.5
        fin = jnp.logical_and(active, chi + cv >= kf)
        go = jnp.logical_and(active, chi + cv < kf)
        thr = jnp.where(fin, v, thr)
        need = jnp.where(fin, kf - chi, need)
        done = jnp.where(fin, 1.0, done)
        hi = jnp.where(go, v, hi)
        chi = jnp.where(go, chi + cv, chi)
        return (jnp.max(1.0 - done), it + 1, hi, chi, thr, need, done)

    left0 = jnp.max(1.0 - done0)
    _, _, _, _, thr, need, _ = lax.while_loop(
        pop_cond, pop, (left0, jnp.int32(0), hi, chi, lo, need0, done0))
    thrb = jnp.concatenate([to_rows(thr[chunk_subl(c)]) for c in range(n_chunks)], axis=0)
    needb = jnp.concatenate([to_rows(need[chunk_subl(c)]) for c in range(n_chunks)], axis=0)
    partial_ties = jnp.max(jnp.where(need < COUNT_ALL, 1.0, 0.0))

    @pl.when(partial_ties > 0.0)
    def _():
        def tie_block(kb, run):
            s = s_s[kb]
            eq = jnp.where(s == thrb, 1.0, 0.0)
            rank = _dot(eq.astype(BF16), tri_ref[1]) + run
            take = jnp.where(s == thrb, jnp.where(rank < needb, 0.0, NEG_BIG), NEG_BIG)
            s_s[kb] = jnp.where(s > thrb, 0.0, take)
            return run + bcast(_rowsum(eq))
        lax.fori_loop(0, nkb, tie_block, jnp.zeros(shape, F32))

    @pl.when(partial_ties <= 0.0)
    def _():
        def sel_block(kb, carry):
            s_s[kb] = jnp.where(s_s[kb] >= thrb, 0.0, NEG_BIG)
            return carry
        lax.fori_loop(0, nkb, sel_block, 0)

    rel_pos = (row - col).astype(F32)
    first_half = col < HEAD_DIM
    qs_s[...] = q_ref[...] * (HEAD_DIM ** -0.5)
    mx_s[...] = jnp.full(mx_s.shape, NEG_BIG, F32)
    al_s[...] = jnp.zeros_like(al_s)
    slopes = [2.0 ** (-8.0 * (h + 1) / N_HEADS) for h in range(N_HEADS)]

    def attend(g, carry):
        blocks = [g * n_diag + u for u in range(n_diag)]
        dists = [jnp.abs(rel_pos + (i * qb - kb * kb_size).astype(F32)) for kb in blocks]
        for p, ps in enumerate(pairs):
            tiles = []
            for kb, dist in zip(blocks, dists):
                mask = s_s[kb]
                kbd = _pair_blockdiag(k_ref[kb, :, ps], first_ref, second_ref)
                bias = jnp.concatenate([mask - slopes[2 * p] * dist, mask - slopes[2 * p + 1] * dist], axis=1)
                tiles.append(_dot_nt(qs_s[:, ps], kbd) + bias)
            top = functools.reduce(jnp.maximum, tiles)
            m_old = mx_s[p]
            m_new = jnp.maximum(m_old, jnp.concatenate(
                [bcast(_rowmax(top[:, :kb_size])), bcast(_rowmax(top[:, kb_size:]))], axis=1))
            mx_s[p] = m_new
            pexp = jnp.concatenate([jnp.exp((t - m_new).astype(BF16)) for t in tiles], axis=1)
            shrink = m_old - m_new
            alpha = jnp.exp(jnp.where(first_half, shrink[:, :kb_size], shrink[:, kb_size:]))
            values = jnp.concatenate(
                [jnp.concatenate([_pair_blockdiag(v_ref[kb, :, ps], first_ref, second_ref), ones_ref[...]], axis=1)
                 for kb in blocks], axis=0)
            al_s[p] = jnp.concatenate([alpha, alpha], axis=1) * al_s[p] + _dot(pexp, values)
        return carry

    lax.fori_loop(0, i + 1, attend, 0)
    for p, ps in enumerate(pairs):
        state = al_s[p]
        o_ref[:, ps] = (state[:, :LANES] / state[:, LANES:]).astype(o_ref.dtype)


def _tri_consts(n):
    j = jnp.arange(n)[:, None]
    s = jnp.arange(n)[None, :]
    return jnp.stack([j > s, j < s]).astype(BF16)


def _pair_consts(n):
    tri = _tri_consts(n)
    zero = jnp.zeros((n, n), BF16)
    tri2 = jnp.block([[tri[0], zero], [zero, tri[0]]])
    first = (jnp.arange(n)[None, :] < n // 2) & jnp.ones((n, 1), bool)
    ones2 = jnp.concatenate([first, ~first], axis=0).astype(BF16)
    return tri2, ones2, first.astype(BF16), (~first).astype(BF16)


def _dsa(q, k, v, qi, kk, wq, topk, qb):
    bsz, seq, width = q.shape
    nq, nk = seq // qb, seq // ATT_BLOCK
    qblk = lambda a: a.reshape(bsz, nq, qb, a.shape[-1])
    kblk = lambda a: a.reshape(bsz, nk, ATT_BLOCK, a.shape[-1])
    qspec = lambda wd: pl.BlockSpec((None, None, qb, wd), lambda b, i: (b, i, 0, 0))
    kspec = lambda wd: pl.BlockSpec((None, nk, ATT_BLOCK, wd), lambda b, i: (b, 0, 0, 0))
    _, ones2, first, second = _pair_consts(ATT_BLOCK)
    consts = (_tri_consts(ATT_BLOCK), ones2, first, second)
    out = pl.pallas_call(
        functools.partial(_dsa_kernel, topk=topk, qb=qb),
        grid=(bsz, nq),
        in_specs=[qspec(width), qspec(width), qspec(LANES), kspec(width), kspec(width), kspec(LANES)]
                 + [pl.BlockSpec(a.shape, lambda b, i, nd=a.ndim: (0,) * nd) for a in consts],
        out_specs=qspec(width),
        out_shape=jax.ShapeDtypeStruct((bsz, nq, qb, width), BF16),
        scratch_shapes=[pltpu.VMEM((nk, qb, ATT_BLOCK), F32),
                        pltpu.VMEM((qb, width), BF16),
                        pltpu.VMEM((N_HEADS // 2, qb, 2 * ATT_BLOCK), F32),
                        pltpu.VMEM((N_HEADS // 2, qb, 2 * LANES), F32)],
        compiler_params=_cparams(("parallel", "arbitrary")),
    )(qblk(q), qblk(qi), qblk(wq), kblk(k), kblk(v), kblk(kk), *consts)
    return out.reshape(bsz, seq, width)


def _sb_kernel(q_ref, k_ref, v_ref, tri_ref, ones_ref, first_ref, second_ref, o_ref, qs_s, run_s, acc_s, *, qb):
    i = pl.program_id(1)
    kb_size = ATT_BLOCK
    n_diag = qb // kb_size
    n_pairs = N_HEADS // 2
    shape = (qb, 2 * kb_size)
    row = lax.broadcasted_iota(jnp.int32, shape, 0)
    col = lax.broadcasted_iota(jnp.int32, shape, 1) % kb_size
    qs_s[...] = q_ref[...] * (HEAD_DIM ** -0.5)
    run_s[...] = jnp.zeros_like(run_s)
    acc_s[...] = jnp.zeros_like(acc_s)

    def step(kb, masked):
        causal = (kb * kb_size + col) < (i * qb + row)
        pairs = [slice(p * LANES, (p + 1) * LANES) for p in range(n_pairs)]
        log_sig, log_keep = [], []
        for ps in pairs:
            kbd = _pair_blockdiag(k_ref[kb, :, ps], first_ref, second_ref)
            z = _dot_nt(qs_s[:, ps], kbd).astype(BF16)
            soft = jnp.log(1.0 + jnp.exp(-jnp.abs(z)))
            keep = -jnp.maximum(z, 0.0) - soft
            log_sig.append(jnp.minimum(z, 0.0) - soft)
            log_keep.append(jnp.where(causal, keep, jnp.zeros_like(keep)) if masked else keep)
        for p, ps in enumerate(pairs):
            after = _dot(log_keep[p], tri_ref[...]).astype(BF16)
            w = jnp.exp(log_sig[p] + after)
            if masked:
                w = jnp.where(causal, w, jnp.zeros_like(w))
            vbd = _pair_blockdiag(v_ref[kb, :, ps], first_ref, second_ref)
            acc_s[p] += jnp.exp(run_s[p]) * _dot(w, vbd)
            run_s[p] += _dot(log_keep[p], ones_ref[...])

    last = (i + 1) * n_diag - 1

    def diag_step(j, carry):
        step(last - j, True)
        return carry

    def past_step(j, carry):
        step(i * n_diag - 1 - j, False)
        return carry

    lax.fori_loop(0, n_diag, diag_step, 0)
    lax.fori_loop(0, i * n_diag, past_step, 0)
    for p in range(n_pairs):
        o_ref[:, p * LANES:(p + 1) * LANES] = acc_s[p].astype(o_ref.dtype)


def _stick_breaking(q, k, v, qb):
    bsz, seq, width = q.shape
    nq, nk = seq // qb, seq // ATT_BLOCK
    qspec = pl.BlockSpec((None, None, qb, width), lambda b, i: (b, i, 0, 0))
    kspec = pl.BlockSpec((None, nk, ATT_BLOCK, width), lambda b, i: (b, 0, 0, 0))
    consts = _pair_consts(ATT_BLOCK)
    out = pl.pallas_call(
        functools.partial(_sb_kernel, qb=qb),
        grid=(bsz, nq),
        in_specs=[qspec, kspec, kspec] + [pl.BlockSpec(a.shape, lambda b, i: (0, 0)) for a in consts],
        out_specs=qspec,
        out_shape=jax.ShapeDtypeStruct((bsz, nq, qb, width), BF16),
        scratch_shapes=[pltpu.VMEM((qb, width), BF16),
                        pltpu.VMEM((N_HEADS // 2, qb, LANES), F32),
                        pltpu.VMEM((N_HEADS // 2, qb, LANES), F32)],
        compiler_params=_cparams(("parallel", "arbitrary")),
    )(q.reshape(bsz, nq, qb, width), k.reshape(bsz, nk, ATT_BLOCK, width),
      v.reshape(bsz, nk, ATT_BLOCK, width), *consts)
    return out.reshape(bsz, seq, width)


def _cd_weight(cd_w_in):
    cw = N_HEADS * HEAD_DIM
    iw = IDX_HEADS * IDX_DIM
    offs = [0, cw, 2 * cw, 3 * cw, 3 * cw + iw, 3 * cw + iw + IDX_DIM, 3 * cw + iw + IDX_DIM + IDX_HEADS]
    qkv_c = cd_w_in[:, :offs[3]]
    q_idx = cd_w_in[:, offs[3]:offs[4]]
    k_idx = cd_w_in[:, offs[4]:offs[5]]
    w_idx = cd_w_in[:, offs[5]:offs[6]]
    qkv_d = cd_w_in[:, offs[6]:]
    pad = jnp.zeros((cd_w_in.shape[0], LANES - w_idx.shape[1]), cd_w_in.dtype)
    return jnp.concatenate([qkv_c, q_idx, qkv_d, k_idx, k_idx, w_idx, pad], axis=1)


def kernel(x, c, norm_g, ada_w, ada_b, mlp_w1, mlp_w2, ab_w_in, ab_w_out, pool_w, pool_scale, ssm_lam_re, ssm_lam_im, ssm_log_dt, ssm_b_re, ssm_b_im, ssm_c_re, ssm_c_im, ssm_d, glu_w, glu_b, cd_w_in, cd_w_out, final_g):
    bsz, seq, d = x.shape
    depth = norm_g.shape[0]
    topk = min(TOPK_MAX, seq // 4)
    tm = min(512, seq)

    c_pad = jnp.zeros((SUBLANES, d), F32).at[:bsz].set(c)
    mod = _modulation(c_pad, ada_w.reshape(depth * 2, d, 3 * d), ada_b.reshape(depth * 2, 1, 3 * d))
    mod = mod[:, :, :bsz, None, :]

    for i in range(depth):
        j = i // 2
        shift, scale1, gate1 = mod[2 * i, 0], mod[2 * i, 1], mod[2 * i, 2]
        g = norm_g[i, 0][None, :]
        if i % 2 == 0:
            width = ab_w_in.shape[2]
            (proj,) = _norm_proj(x, g, shift, scale1, ab_w_in[j].astype(BF16), (width,), (BF16,), tm)
            s5 = _s5_params(ssm_lam_re[j], ssm_lam_im[j], ssm_log_dt[j],
                            ssm_b_re[j], ssm_b_im[j], ssm_c_re[j], ssm_c_im[j])
            y = _even_mixer(proj, pool_w[j].astype(BF16), pool_scale[j][None, :], s5,
                            ssm_d[j][None, :], glu_w[j].astype(BF16), glu_b[j][None, :],
                            tt=min(256, seq), scan_lanes=512)
            x = _out_proj(x, gate1, (y,), ab_w_out[j].astype(BF16), tm)
        else:
            hw = N_HEADS * HEAD_DIM
            widths = (hw,) * 7 + (LANES, LANES)
            dtypes = (BF16,) * 8 + (F32,)
            qc, kc, vc, qi, qd, kd, vd, kk, wq = _norm_proj(
                x, g, shift, scale1, _cd_weight(cd_w_in[j]).astype(BF16), widths, dtypes, tm)
            oc = _dsa(qc, kc, vc, qi, kk, wq, topk, qb=ATT_QUERY_BLOCK)
            od = _stick_breaking(qd, kd, vd, qb=ATT_QUERY_BLOCK)
            x = _out_proj(x, gate1, (oc, od), cd_w_out[j].astype(BF16), tm)
        shift, scale1, gate1 = mod[2 * i + 1, 0], mod[2 * i + 1, 1], mod[2 * i + 1, 2]
        x = _mlp(x, norm_g[i, 1][None, :], shift, scale1, gate1,
                 mlp_w1[i].astype(BF16), mlp_w2[i].astype(BF16), final_g[None, :],
                 final_norm=(i == depth - 1), tm=min(1024, seq), tf=1024)
    return x
```

```python
import functools
import math

import jax
import jax.numpy as jnp
from jax import lax
from jax.experimental import pallas as pl
from jax.experimental.pallas import tpu as pltpu

F32 = jnp.float32
BF16 = jnp.bfloat16

EPS = 1e-6
CHUNK = 64
POOL_WINDOWS = (2, 4, 8, 16)
POOL_CH = 128
SSM_GROUP = 16
SSM_STATE = 64
HEAD_DIM = 64
N_HEADS = 8
IDX_HEADS = 8
IDX_DIM = 64
IDX_SCALE = (IDX_HEADS ** -0.5) * (IDX_DIM ** -0.5)
TOPK_MAX = 256

LANES = 128
SUBLANES = 8
VMEM_LIMIT = 56 * 1024 * 1024

ATT_BLOCK = 128
ATT_QUERY_BLOCK = 512
NEG_BIG = -1e30
COUNT_ALL = 1e9
BISECT_ITERS = 28
COUNT_ROWS = 128
INTERP_SPAN = 256.0
MAX_POPS = 4096


def _cparams(sem):
    return pltpu.CompilerParams(dimension_semantics=sem, vmem_limit_bytes=VMEM_LIMIT)


def _dot(a, b):
    return jnp.dot(a, b, preferred_element_type=F32)


def _dot_nt(a, b):
    return lax.dot_general(a, b, (((1,), (1,)), ((), ())), preferred_element_type=F32)


def _rowsum(x):
    return jnp.sum(x, axis=1, keepdims=True)


def _rowmax(x):
    return jnp.max(x, axis=1, keepdims=True)


def _norm_modulate(x, g, shift, scale1):
    ms = jnp.mean(x * x, axis=-1, keepdims=True)
    return (x * lax.rsqrt(ms + EPS)) * g * scale1 + shift


def _mod_kernel(c_ref, w_ref, b_ref, o_ref):
    j = pl.program_id(1)
    m = _dot(c_ref[...], w_ref[...]) + b_ref[...]
    o_ref[...] = m + jnp.where(j > 0, 1.0, 0.0)


def _modulation(c_pad, ada_w, ada_b):
    n_sub, d, _ = ada_w.shape
    rows = c_pad.shape[0]
    return pl.pallas_call(
        _mod_kernel,
        grid=(n_sub, 3),
        in_specs=[
            pl.BlockSpec((rows, d), lambda s, j: (0, 0)),
            pl.BlockSpec((None, d, d), lambda s, j: (s, 0, j)),
            pl.BlockSpec((None, 1, d), lambda s, j: (s, 0, j)),
        ],
        out_specs=pl.BlockSpec((None, None, rows, d), lambda s, j: (s, j, 0, 0)),
        out_shape=jax.ShapeDtypeStruct((n_sub, 3, rows, d), F32),
        compiler_params=_cparams(("arbitrary", "arbitrary")),
    )(c_pad, ada_w, ada_b)


def _norm_proj_kernel(x_ref, g_ref, shift_ref, scale_ref, w_ref, *o_refs, widths):
    h = _norm_modulate(x_ref[...], g_ref[...], shift_ref[...], scale_ref[...]).astype(BF16)
    y = _dot(h, w_ref[...])
    off = 0
    for o_ref, width in zip(o_refs, widths):
        o_ref[...] = y[:, off:off + width].astype(o_ref.dtype)
        off += width


def _norm_proj(x, g, shift, scale1, w, widths, dtypes, tm):
    bsz, seq, d = x.shape
    n = w.shape[1]
    vec = pl.BlockSpec((None, 1, d), lambda b, i: (b, 0, 0))
    return pl.pallas_call(
        functools.partial(_norm_proj_kernel, widths=widths),
        grid=(bsz, seq // tm),
        in_specs=[
            pl.BlockSpec((None, tm, d), lambda b, i: (b, i, 0)),
            pl.BlockSpec((1, d), lambda b, i: (0, 0)),
            vec, vec,
            pl.BlockSpec((d, n), lambda b, i: (0, 0)),
        ],
        out_specs=[pl.BlockSpec((None, tm, wd), lambda b, i: (b, i, 0)) for wd in widths],
        out_shape=[jax.ShapeDtypeStruct((bsz, seq, wd), dt) for wd, dt in zip(widths, dtypes)],
        compiler_params=_cparams(("parallel", "parallel")),
    )(x, g, shift, scale1, w)


def _out_proj_kernel(x_ref, gate_ref, *refs, n_y):
    y_refs, w_ref, o_ref = refs[:n_y], refs[n_y], refs[n_y + 1]
    y = jnp.concatenate([r[...] for r in y_refs], axis=1) if n_y > 1 else y_refs[0][...]
    o_ref[...] = x_ref[...] + gate_ref[...] * _dot(y, w_ref[...])


def _out_proj(x, gate1, ys, w, tm):
    bsz, seq, d = x.shape
    k = w.shape[0]
    return pl.pallas_call(
        functools.partial(_out_proj_kernel, n_y=len(ys)),
        grid=(bsz, seq // tm),
        in_specs=[
            pl.BlockSpec((None, tm, d), lambda b, i: (b, i, 0)),
            pl.BlockSpec((None, 1, d), lambda b, i: (b, 0, 0)),
            *[pl.BlockSpec((None, tm, y.shape[2]), lambda b, i: (b, i, 0)) for y in ys],
            pl.BlockSpec((k, d), lambda b, i: (0, 0)),
        ],
        out_specs=pl.BlockSpec((None, tm, d), lambda b, i: (b, i, 0)),
        out_shape=jax.ShapeDtypeStruct((bsz, seq, d), F32),
        compiler_params=_cparams(("parallel", "parallel")),
    )(x, gate1, *ys, w)


def _mlp_kernel(x_ref, g_ref, shift_ref, scale_ref, gate_ref, w1_ref, w2_ref, fg_ref, o_ref,
                h_s, acc_s, *, final_norm):
    f = pl.program_id(2)

    @pl.when(f == 0)
    def _():
        h_s[...] = _norm_modulate(x_ref[...], g_ref[...], shift_ref[...], scale_ref[...]).astype(BF16)
        acc_s[...] = jnp.zeros_like(acc_s)

    a = jnp.maximum(_dot(h_s[...], w1_ref[...]), 0.0)
    acc_s[...] += _dot((a * a).astype(BF16), w2_ref[...])

    @pl.when(f == pl.num_programs(2) - 1)
    def _():
        y = x_ref[...] + gate_ref[...] * acc_s[...]
        if final_norm:
            ms = jnp.mean(y * y, axis=-1, keepdims=True)
            y = (y * lax.rsqrt(ms + EPS)) * fg_ref[...]
        o_ref[...] = y


def _mlp(x, g, shift, scale1, gate1, w1, w2, final_g, final_norm, tm, tf):
    bsz, seq, d = x.shape
    dff = w1.shape[1]
    vec = pl.BlockSpec((None, 1, d), lambda b, i, f: (b, 0, 0))
    one = pl.BlockSpec((1, d), lambda b, i, f: (0, 0))
    return pl.pallas_call(
        functools.partial(_mlp_kernel, final_norm=final_norm),
        grid=(bsz, seq // tm, dff // tf),
        in_specs=[
            pl.BlockSpec((None, tm, d), lambda b, i, f: (b, i, 0)),
            one, vec, vec, vec,
            pl.BlockSpec((d, tf), lambda b, i, f: (0, f)),
            pl.BlockSpec((tf, d), lambda b, i, f: (f, 0)),
            one,
        ],
        out_specs=pl.BlockSpec((None, tm, d), lambda b, i, f: (b, i, 0)),
        out_shape=jax.ShapeDtypeStruct((bsz, seq, d), F32),
        scratch_shapes=[pltpu.VMEM((tm, d), BF16), pltpu.VMEM((tm, d), F32)],
        compiler_params=_cparams(("parallel", "parallel", "arbitrary")),
    )(x, g, shift, scale1, gate1, w1, w2, final_g)


def _cmul(ar, ai, br, bi):
    return ar * br - ai * bi, ar * bi + ai * br


def _s5_params_kernel(lr_ref, li_ref, ldt_ref, brt_ref, bit_ref, crt_ref, cit_ref,
                      wb_ref, wc_ref, a_re_ref, a_im_ref, pk_re_ref, pk_im_ref, sg_re_ref, sg_im_ref):
    lr, li = lr_ref[...], li_ref[...]
    dt = jnp.exp(ldt_ref[...])
    mag = jnp.exp(lr * dt)
    ar, ai = mag * jnp.cos(li * dt), mag * jnp.sin(li * dt)
    den = lr * lr + li * li
    zr = ((ar - 1.0) * lr + ai * li) / den
    zi = (ai * lr - (ar - 1.0) * li) / den

    rows, cols = brt_ref.shape
    rgrp = lax.broadcasted_iota(jnp.int32, (rows, cols), 0) // SSM_GROUP
    cgrp = lax.broadcasted_iota(jnp.int32, (rows, cols), 1) // SSM_STATE
    diag = rgrp == cgrp
    br, bi = brt_ref[...], bit_ref[...]
    wb_ref[0] = jnp.where(diag, zr * br - zi * bi, 0.0).astype(wb_ref.dtype)
    wb_ref[1] = jnp.where(diag, zr * bi + zi * br, 0.0).astype(wb_ref.dtype)
    wc_ref[0] = jnp.where(diag, crt_ref[...], 0.0).astype(wc_ref.dtype)
    wc_ref[1] = jnp.where(diag, cit_ref[...], 0.0).astype(wc_ref.dtype)

    n = lr.shape[1]
    rows8 = pk_re_ref.shape[0]
    seg_len = rows8 // SUBLANES
    a_re_ref[...] = jnp.broadcast_to(ar, (SUBLANES, n))
    a_im_ref[...] = jnp.broadcast_to(ai, (SUBLANES, n))
    expo = lax.broadcasted_iota(jnp.int32, (rows8, n), 0) // SUBLANES + 1
    sq_r, sq_i = jnp.broadcast_to(ar, (rows8, n)), jnp.broadcast_to(ai, (rows8, n))
    pr, pi = jnp.ones((rows8, n), F32), jnp.zeros((rows8, n), F32)
    for b in range(seg_len.bit_length()):
        nr, ni = _cmul(pr, pi, sq_r, sq_i)
        take = ((expo >> b) & 1) == 1
        pr, pi = jnp.where(take, nr, pr), jnp.where(take, ni, pi)
        sq_r, sq_i = _cmul(sq_r, sq_i, sq_r, sq_i)
    pk_re_ref[...] = pr
    pk_im_ref[...] = pi
    row = lax.broadcasted_iota(jnp.int32, (SUBLANES, n), 0)
    gr, gi = pr[rows8 - SUBLANES:], pi[rows8 - SUBLANES:]
    for k in range(3):
        keep = row >= (1 << k)
        sg_re_ref[k] = jnp.where(keep, gr, 0.0)
        sg_im_ref[k] = jnp.where(keep, gi, 0.0)
        gr, gi = _cmul(gr, gi, gr, gi)


def _s5_params(lam_re, lam_im, log_dt, b_re, b_im, c_re, c_im, seg_len):
    g, p = lam_re.shape
    c = b_re.shape[2]
    n = g * p
    flat = lambda a: a.reshape(1, n)
    ldt = jnp.broadcast_to(log_dt[:, None], (g, p)).reshape(1, n)
    b_t = lambda a: jnp.tile(jnp.transpose(a, (2, 0, 1)).reshape(c, n), (g, 1))
    c_t = lambda a: jnp.tile(a.reshape(g * c, p), (1, g))
    outs = pl.pallas_call(
        _s5_params_kernel,
        out_shape=[jax.ShapeDtypeStruct((2, g * c, n), BF16), jax.ShapeDtypeStruct((2, g * c, n), BF16),
                   jax.ShapeDtypeStruct((SUBLANES, n), F32), jax.ShapeDtypeStruct((SUBLANES, n), F32),
                   jax.ShapeDtypeStruct((seg_len * SUBLANES, n), F32), jax.ShapeDtypeStruct((seg_len * SUBLANES, n), F32),
                   jax.ShapeDtypeStruct((3, SUBLANES, n), F32), jax.ShapeDtypeStruct((3, SUBLANES, n), F32)],
        compiler_params=pltpu.CompilerParams(vmem_limit_bytes=VMEM_LIMIT),
    )(flat(lam_re), flat(lam_im), ldt, b_t(b_re), b_t(b_im), c_t(c_re), c_t(c_im))
    wb, wct, a_re, a_im, pk_re, pk_im, sg_re, sg_im = outs
    wc = jnp.transpose(wct, (0, 2, 1))
    pk_re, pk_im = (a.reshape(seg_len, SUBLANES, n) for a in (pk_re, pk_im))
    return wb, wc, a_re, a_im, pk_re, pk_im, sg_re, sg_im


def _gelu_tanh(x):
    return 0.5 * x * (1.0 + jnp.tanh(math.sqrt(2.0 / math.pi) * (x + 0.044715 * (x * x * x))))


def _even_kernel(p_ref, band_ref, perm_ref, wpool_ref, pscale_ref, wb_ref, wc_ref, dskip_ref, wglu_ref, bglu_ref,
                 a_re_ref, a_im_ref, pk_re_ref, pk_im_ref, sg_re_ref, sg_im_ref,
                 o_ref, prev_s, xr_s, xi_s, car_re, car_im, *, mix_a, scan_lanes):
    i = pl.program_id(1)
    tt = p_ref.shape[0]

    @pl.when(i == 0)
    def _():
        prev_s[...] = jnp.zeros_like(prev_s)
        car_re[...] = jnp.zeros_like(car_re)
        car_im[...] = jnp.zeros_like(car_im)

    ua = p_ref[:, :mix_a]
    ext = jnp.concatenate([prev_s[...], ua], axis=0)
    t1 = i * tt + lax.broadcasted_iota(jnp.int32, (tt, POOL_CH), 0) + 1
    for g, win in enumerate(POOL_WINDOWS):
        sl = slice(g * POOL_CH, (g + 1) * POOL_CH)
        wsum = _dot(band_ref[g], ext[:, sl])
        cnt = jnp.minimum(t1, win).astype(F32)
        pg = wsum / cnt - ua[:, sl].astype(F32)
        yg = _dot(pg.astype(BF16), wpool_ref[g]) * pscale_ref[:, sl]
        o_ref[:, sl] = yg.astype(o_ref.dtype)
    prev_s[...] = ua

    ub = _dot(perm_ref[0], p_ref[:, mix_a:]).astype(BF16)
    n_tiles = xr_s.shape[0]
    n_diag_blocks = ub.shape[1] // LANES
    tpb = n_tiles // n_diag_blocks
    sw = tpb * LANES
    for j in range(n_diag_blocks):
        cs, ss = slice(j * LANES, (j + 1) * LANES), slice(j * sw, (j + 1) * sw)
        xr, xi = _dot(ub[:, cs], wb_ref[0, cs, ss]), _dot(ub[:, cs], wb_ref[1, cs, ss])
        for q in range(tpb):
            xr_s[j * tpb + q] = xr[:, q * LANES:(q + 1) * LANES]
            xi_s[j * tpb + q] = xi[:, q * LANES:(q + 1) * LANES]

    seg_len = tt // SUBLANES
    seg_rows = lambda k: pl.ds(pl.multiple_of(k * SUBLANES, SUBLANES), SUBLANES)
    first_seg = lax.broadcasted_iota(jnp.int32, (SUBLANES, LANES), 0) == 0
    zeros = jnp.zeros((SUBLANES, LANES), F32)
    tpc = scan_lanes // LANES
    for c in range(n_tiles // tpc):
        tiles = list(range(c * tpc, (c + 1) * tpc))
        lanes = [slice(t * LANES, (t + 1) * LANES) for t in tiles]

        def local(k, carry, tiles=tiles, lanes=lanes):
            out = []
            for t, ls, (hr, hi) in zip(tiles, lanes, carry):
                dr, di = _cmul(a_re_ref[:, ls], a_im_ref[:, ls], hr, hi)
                hr, hi = dr + xr_s[t, seg_rows(k), :], di + xi_s[t, seg_rows(k), :]
                xr_s[t, seg_rows(k), :] = hr
                xi_s[t, seg_rows(k), :] = hi
                out.append((hr, hi))
            return tuple(out)

        ends = lax.fori_loop(0, seg_len, local, tuple((zeros, zeros) for _ in tiles))
        starts = []
        for t, ls, (er, ei) in zip(tiles, lanes, ends):
            fr = jnp.where(first_seg, car_re[:, ls], pltpu.roll(er, 1, 0))
            fi = jnp.where(first_seg, car_im[:, ls], pltpu.roll(ei, 1, 0))
            for k in range(3):
                dr, di = _cmul(sg_re_ref[k, :, ls], sg_im_ref[k, :, ls],
                               pltpu.roll(fr, 1 << k, 0), pltpu.roll(fi, 1 << k, 0))
                fr, fi = fr + dr, fi + di
            starts.append((fr, fi))
            dr, di = _cmul(pk_re_ref[seg_len - 1, :, ls], pk_im_ref[seg_len - 1, :, ls], fr, fi)
            car_re[:, ls] = jnp.broadcast_to((dr + er)[SUBLANES - 1:], fr.shape)
            car_im[:, ls] = jnp.broadcast_to((di + ei)[SUBLANES - 1:], fr.shape)

        def fix(k, carry, tiles=tiles, lanes=lanes, starts=starts):
            for t, ls, (fr, fi) in zip(tiles, lanes, starts):
                dr, di = _cmul(pk_re_ref[k, :, ls], pk_im_ref[k, :, ls], fr, fi)
                xr_s[t, seg_rows(k), :] += dr
                xi_s[t, seg_rows(k), :] += di
            return carry

        lax.fori_loop(0, seg_len, fix, 0)

    state = lambda ref, j: jnp.concatenate([ref[j * tpb + q] for q in range(tpb)], axis=1).astype(BF16)
    y = jnp.concatenate(
        [_dot(state(xr_s, j), wc_ref[0, j * sw:(j + 1) * sw, j * LANES:(j + 1) * LANES])
         - _dot(state(xi_s, j), wc_ref[1, j * sw:(j + 1) * sw, j * LANES:(j + 1) * LANES])
         for j in range(n_diag_blocks)], axis=1)
    y = _gelu_tanh(y + dskip_ref[...] * ub.astype(F32))
    z = _dot(y.astype(BF16), wglu_ref[...]) + bglu_ref[...]
    out = (y * (1.0 / (1.0 + jnp.exp(-z)))).astype(o_ref.dtype)
    o_ref[:, mix_a:] = _dot(perm_ref[1], out).astype(o_ref.dtype)


def _segment_perm(tt):
    seg_len = tt // SUBLANES
    r = jnp.arange(tt)
    src = (r % SUBLANES) * seg_len + r // SUBLANES
    p = (src[:, None] == jnp.arange(tt)[None, :])
    return jnp.stack([p, p.T]).astype(BF16)


def _pool_band(tt):
    t = jnp.arange(tt)[:, None] + tt
    j = jnp.arange(2 * tt)[None, :]
    return jnp.stack([((j <= t) & (j > t - w)) for w in POOL_WINDOWS]).astype(BF16)


def _even_mixer(proj, wpool, pscale, s5, dskip, wglu, bglu, tt, scan_lanes):
    bsz, seq, width = proj.shape
    wb, wc, a_re, a_im, pk_re, pk_im, sg_re, sg_im = s5
    mix_a = len(POOL_WINDOWS) * POOL_CH
    mix_b = width - mix_a
    n_state = wb.shape[2]
    const = lambda a: pl.BlockSpec(a.shape, lambda b, i, nd=a.ndim: (0,) * nd)
    band = _pool_band(tt)
    args = (band, _segment_perm(tt), wpool, pscale, wb, wc, dskip, wglu, bglu, a_re, a_im, pk_re, pk_im, sg_re, sg_im)
    return pl.pallas_call(
        functools.partial(_even_kernel, mix_a=mix_a, scan_lanes=scan_lanes),
        grid=(bsz, seq // tt),
        in_specs=[pl.BlockSpec((None, tt, width), lambda b, i: (b, i, 0))] + [const(a) for a in args],
        out_specs=pl.BlockSpec((None, tt, width), lambda b, i: (b, i, 0)),
        out_shape=jax.ShapeDtypeStruct((bsz, seq, width), BF16),
        scratch_shapes=[pltpu.VMEM((tt, mix_a), BF16),
                        pltpu.VMEM((n_state // LANES, tt, LANES), F32), pltpu.VMEM((n_state // LANES, tt, LANES), F32),
                        pltpu.VMEM((SUBLANES, n_state), F32), pltpu.VMEM((SUBLANES, n_state), F32)],
        compiler_params=_cparams(("parallel", "arbitrary")),
    )(proj, *args)


def _pair_blockdiag(x, first_ref, second_ref):
    return jnp.concatenate([x * first_ref[...], x * second_ref[...]], axis=0)


def _dsa_kernel(q_ref, qi_ref, wq_ref, k_ref, v_ref, kk_ref, tri_ref, ones_ref, first_ref, second_ref, o_ref,
                s_s, qs_s, mx_s, al_s, *, topk, qb):
    i = pl.program_id(1)
    kb_size = ATT_BLOCK
    n_diag = qb // kb_size
    nkb = (i + 1) * n_diag
    shape = (qb, kb_size)
    row = lax.broadcasted_iota(jnp.int32, shape, 0)
    col = lax.broadcasted_iota(jnp.int32, shape, 1)
    q_chunk = (i * qb + row) // CHUNK
    bcast = lambda a: jnp.broadcast_to(a, shape)
    kf = float(topk)
    pairs = [slice(p * LANES, (p + 1) * LANES) for p in range(N_HEADS // 2)]

    wq = wq_ref[...]
    w_cols = [bcast(wq[:, h:h + 1] * IDX_SCALE) for h in range(IDX_HEADS)]

    def score_block(kb, carry):
        rmax, rmin = carry
        kbd = _pair_blockdiag(kk_ref[kb], first_ref, second_ref)
        acc = jnp.zeros(shape, F32)
        for p, ps in enumerate(pairs):
            rel = jnp.maximum(_dot_nt(qi_ref[:, ps], kbd), 0.0)
            acc = acc + rel[:, :kb_size] * w_cols[2 * p] + rel[:, kb_size:] * w_cols[2 * p + 1]
        adm = ((kb * kb_size + col) // CHUNK) <= q_chunk
        s_s[kb] = jnp.where(adm, acc, -jnp.inf)
        return jnp.maximum(rmax, jnp.where(adm, acc, -jnp.inf)), jnp.minimum(rmin, jnp.where(adm, acc, jnp.inf))

    rmax, rmin = lax.fori_loop(0, nkb, score_block,
                               (jnp.full(shape, -jnp.inf, F32), jnp.full(shape, jnp.inf, F32)))
    n_chunks = qb // kb_size
    chunk_rows = lambda c: slice(c * kb_size, (c + 1) * kb_size)
    chunk_subl = lambda c: slice(c * SUBLANES, (c + 1) * SUBLANES)
    ones8 = jnp.ones((SUBLANES, kb_size), BF16)
    tile = (kb_size, kb_size)

    def to_dense(rep):
        return rep.T[:SUBLANES]

    def to_rows(dense):
        return jnp.broadcast_to(dense[:1], tile).T

    def dense_of(fn):
        return jnp.concatenate([fn(c) for c in range(n_chunks)], axis=0)

    def fold_keys(c, fn, init):
        def body(g, a):
            for u in range(n_diag):
                a = fn(a, s_s[g * n_diag + u, chunk_rows(c), :])
            return a
        return lax.fori_loop(0, i + 1, body, init)

    def count(x, cmp, wanted=None):
        xbs = [to_rows(x[chunk_subl(c)]) for c in range(n_chunks)]
        cnts = []
        for c in range(n_chunks):
            scan = lambda c=c: fold_keys(c, lambda a, s: a + jnp.where(cmp(s, xbs[c]), 1.0, 0.0),
                                         jnp.zeros(tile, F32))
            if wanted is None:
                cnts.append(scan())
            else:
                cnts.append(lax.cond(jnp.max(wanted[chunk_subl(c)]) > 0.0, scan, lambda: jnp.zeros(tile, F32)))
        return jnp.concatenate([_dot_nt(ones8, cnt.astype(BF16)) for cnt in cnts], axis=0)

    count_ge = functools.partial(count, cmp=lambda s, xb: s >= xb)
    count_gt = functools.partial(count, cmp=lambda s, xb: s > xb)
    count_eq = functools.partial(count, cmp=lambda s, xb: s == xb)

    smax = dense_of(lambda c: to_dense(jnp.broadcast_to(_rowmax(rmax[chunk_rows(c)]), tile)))
    smin = dense_of(lambda c: to_dense(jnp.broadcast_to(jnp.min(rmin[chunk_rows(c)], axis=1, keepdims=True), tile)))
    dshape = (n_chunks * SUBLANES, kb_size)
    t_dense = (i * qb + (lax.broadcasted_iota(jnp.int32, dshape, 0) // SUBLANES) * kb_size
               + lax.broadcasted_iota(jnp.int32, dshape, 1))
    n_adm = ((t_dense // CHUNK + 1) * CHUNK).astype(F32)
    hi0 = 2.0 * jnp.maximum(jnp.abs(smax), jnp.abs(smin)) + 1.0
    zero = jnp.zeros_like(smin)
    cge0, cgt0 = count_ge(zero), count_gt(zero)
    wide = n_adm > kf
    zero_tie = jnp.logical_and(wide, jnp.logical_and(cgt0 < kf, cge0 >= kf))
    above = jnp.logical_and(wide, cgt0 >= kf)
    below = jnp.logical_and(wide, cge0 < kf)
    lo0 = jnp.where(jnp.logical_or(above, zero_tie), 0.0, smin)
    clo0 = jnp.where(zero_tie, kf, jnp.where(above, cge0, n_adm))
    hi0 = jnp.where(below, 0.0, hi0)
    chi0 = jnp.where(below, cge0, 0.0)
    need0 = jnp.where(zero_tie, kf - cgt0, COUNT_ALL)

    def unresolved(clo):
        return jnp.max(jnp.where(clo > kf, 1.0, 0.0))

    def bisect_cond(carry):
        return jnp.logical_and(carry[0] > 0.0, carry[1] < BISECT_ITERS)

    def bisect(carry):
        _, it, lo, hi, clo, chi = carry
        span = clo - chi
        aim = (clo - (kf + 0.5)) / span
        interpolate = jnp.logical_and(span < INTERP_SPAN, it % 2 == 0)
        mid = lo + jnp.where(interpolate, aim, 0.5) * (hi - lo)
        active = clo > kf
        c = count_ge(mid, wanted=jnp.where(active, 1.0, 0.0))
        up = jnp.logical_and(active, c >= kf)
        down = jnp.logical_and(active, c < kf)
        clo = jnp.where(up, c, clo)
        return (unresolved(clo), it + 1, jnp.where(up, mid, lo), jnp.where(down, mid, hi), clo,
                jnp.where(down, c, chi))

    _, _, lo, hi, clo, chi = lax.while_loop(
        bisect_cond, bisect, (unresolved(clo0), jnp.int32(0), lo0, hi0, clo0, chi0))

    done0 = jnp.where(clo <= kf, 1.0, 0.0)

    def pop_cond(carry):
        return jnp.logical_and(carry[0] > 0.0, carry[1] < MAX_POPS)

    def next_below(hi):
        def one(c):
            hib = to_rows(hi[chunk_subl(c)])
            best = fold_keys(c, lambda a, s: jnp.maximum(a, jnp.where(s < hib, s, -jnp.inf)),
                             jnp.full(tile, -jnp.inf, F32))
            return to_dense(jnp.broadcast_to(_rowmax(best), tile))
        return dense_of(one)

    def pop(carry):
        _, it, hi, chi, thr, need, done = carry
        v = next_below(hi)
        cv = count_eq(v)
        active = done < 0.5
        fin = jnp.logical_and(active, chi + cv >= kf)
        go = jnp.logical_and(active, chi + cv < kf)
        thr = jnp.where(fin, v, thr)
        need = jnp.where(fin, kf - chi, need)
        done = jnp.where(fin, 1.0, done)
        hi = jnp.where(go, v, hi)
        chi = jnp.where(go, chi + cv, chi)
        return (jnp.max(1.0 - done), it + 1, hi, chi, thr, need, done)

    left0 = jnp.max(1.0 - done0)
    _, _, _, _, thr, need, _ = lax.while_loop(
        pop_cond, pop, (left0, jnp.int32(0), hi, chi, lo, need0, done0))
    thrb = jnp.concatenate([to_rows(thr[chunk_subl(c)]) for c in range(n_chunks)], axis=0)
    needb = jnp.concatenate([to_rows(need[chunk_subl(c)]) for c in range(n_chunks)], axis=0)
    partial_ties = jnp.max(jnp.where(need < COUNT_ALL, 1.0, 0.0))

    @pl.when(partial_ties > 0.0)
    def _():
        def tie_block(kb, run):
            s = s_s[kb]
            eq = jnp.where(s == thrb, 1.0, 0.0)
            rank = _dot(eq.astype(BF16), tri_ref[1]) + run
            take = jnp.where(s == thrb, jnp.where(rank < needb, 0.0, NEG_BIG), NEG_BIG)
            s_s[kb] = jnp.where(s > thrb, 0.0, take)
            return run + bcast(_rowsum(eq))
        lax.fori_loop(0, nkb, tie_block, jnp.zeros(shape, F32))

    @pl.when(partial_ties <= 0.0)
    def _():
        def sel_block(kb, carry):
            s_s[kb] = jnp.where(s_s[kb] >= thrb, 0.0, NEG_BIG)
            return carry
        lax.fori_loop(0, nkb, sel_block, 0)

    rel_pos = (row - col).astype(F32)
    first_half = col < HEAD_DIM
    qs_s[...] = q_ref[...] * (HEAD_DIM ** -0.5)
    mx_s[...] = jnp.full(mx_s.shape, NEG_BIG, F32)
    al_s[...] = jnp.zeros_like(al_s)
    slopes = [2.0 ** (-8.0 * (h + 1) / N_HEADS) for h in range(N_HEADS)]

    def attend(g, carry):
        blocks = [g * n_diag + u for u in range(n_diag)]
        dists = [jnp.abs(rel_pos + (i * qb - kb * kb_size).astype(F32)) for kb in blocks]
        for p, ps in enumerate(pairs):
            tiles = []
            for kb, dist in zip(blocks, dists):
                mask = s_s[kb]
                kbd = _pair_blockdiag(k_ref[kb, :, ps], first_ref, second_ref)
                bias = jnp.concatenate([mask - slopes[2 * p] * dist, mask - slopes[2 * p + 1] * dist], axis=1)
                tiles.append(_dot_nt(qs_s[:, ps], kbd) + bias)
            top = functools.reduce(jnp.maximum, tiles)
            m_old = mx_s[p]
            m_new = jnp.maximum(m_old, jnp.concatenate(
                [bcast(_rowmax(top[:, :kb_size])), bcast(_rowmax(top[:, kb_size:]))], axis=1))
            mx_s[p] = m_new
            pexp = jnp.concatenate([jnp.exp((t - m_new).astype(BF16)) for t in tiles], axis=1)
            shrink = m_old - m_new
            alpha = jnp.exp(jnp.where(first_half, shrink[:, :kb_size], shrink[:, kb_size:]))
            values = jnp.concatenate(
                [jnp.concatenate([_pair_blockdiag(v_ref[kb, :, ps], first_ref, second_ref), ones_ref[...]], axis=1)
                 for kb in blocks], axis=0)
            al_s[p] = jnp.concatenate([alpha, alpha], axis=1) * al_s[p] + _dot(pexp, values)
        return carry

    lax.fori_loop(0, i + 1, attend, 0)
    for p, ps in enumerate(pairs):
        state = al_s[p]
        o_ref[:, ps] = (state[:, :LANES] / state[:, LANES:]).astype(o_ref.dtype)


def _tri_consts(n):
    j = jnp.arange(n)[:, None]
    s = jnp.arange(n)[None, :]
    return jnp.stack([j > s, j < s]).astype(BF16)


def _pair_consts(n):
    tri = _tri_consts(n)
    zero = jnp.zeros((n, n), BF16)
    tri2 = jnp.block([[tri[0], zero], [zero, tri[0]]])
    first = (jnp.arange(n)[None, :] < n // 2) & jnp.ones((n, 1), bool)
    ones2 = jnp.concatenate([first, ~first], axis=0).astype(BF16)
    return tri2, ones2, first.astype(BF16), (~first).astype(BF16)


def _dsa(q, k, v, qi, kk, wq, topk, qb):
    bsz, seq, width = q.shape
    nq, nk = seq // qb, seq // ATT_BLOCK
    qblk = lambda a: a.reshape(bsz, nq, qb, a.shape[-1])
    kblk = lambda a: a.reshape(bsz, nk, ATT_BLOCK, a.shape[-1])
    qspec = lambda wd: pl.BlockSpec((None, None, qb, wd), lambda b, i: (b, i, 0, 0))
    kspec = lambda wd: pl.BlockSpec((None, nk, ATT_BLOCK, wd), lambda b, i: (b, 0, 0, 0))
    _, ones2, first, second = _pair_consts(ATT_BLOCK)
    consts = (_tri_consts(ATT_BLOCK), ones2, first, second)
    out = pl.pallas_call(
        functools.partial(_dsa_kernel, topk=topk, qb=qb),
        grid=(bsz, nq),
        in_specs=[qspec(width), qspec(width), qspec(LANES), kspec(width), kspec(width), kspec(LANES)]
                 + [pl.BlockSpec(a.shape, lambda b, i, nd=a.ndim: (0,) * nd) for a in consts],
        out_specs=qspec(width),
        out_shape=jax.ShapeDtypeStruct((bsz, nq, qb, width), BF16),
        scratch_shapes=[pltpu.VMEM((nk, qb, ATT_BLOCK), F32),
                        pltpu.VMEM((qb, width), BF16),
                        pltpu.VMEM((N_HEADS // 2, qb, 2 * ATT_BLOCK), F32),
                        pltpu.VMEM((N_HEADS // 2, qb, 2 * LANES), F32)],
        compiler_params=_cparams(("parallel", "arbitrary")),
    )(qblk(q), qblk(qi), qblk(wq), kblk(k), kblk(v), kblk(kk), *consts)
    return out.reshape(bsz, seq, width)


def _sb_kernel(q_ref, k_ref, v_ref, tri_ref, ones_ref, first_ref, second_ref, o_ref, qs_s, run_s, acc_s, *, qb):
    i = pl.program_id(1)
    kb_size = ATT_BLOCK
    n_diag = qb // kb_size
    n_pairs = N_HEADS // 2
    shape = (qb, 2 * kb_size)
    row = lax.broadcasted_iota(jnp.int32, shape, 0)
    col = lax.broadcasted_iota(jnp.int32, shape, 1) % kb_size
    qs_s[...] = q_ref[...] * (HEAD_DIM ** -0.5)
    run_s[...] = jnp.zeros_like(run_s)
    acc_s[...] = jnp.zeros_like(acc_s)

    def step(kb, masked):
        causal = (kb * kb_size + col) < (i * qb + row)
        pairs = [slice(p * LANES, (p + 1) * LANES) for p in range(n_pairs)]
        log_sig, log_keep = [], []
        for ps in pairs:
            kbd = _pair_blockdiag(k_ref[kb, :, ps], first_ref, second_ref)
            z = _dot_nt(qs_s[:, ps], kbd).astype(BF16)
            soft = jnp.log(1.0 + jnp.exp(-jnp.abs(z)))
            keep = -jnp.maximum(z, 0.0) - soft
            log_sig.append(jnp.minimum(z, 0.0) - soft)
            log_keep.append(jnp.where(causal, keep, jnp.zeros_like(keep)) if masked else keep)
        for p, ps in enumerate(pairs):
            after = _dot(log_keep[p], tri_ref[...]).astype(BF16)
            w = jnp.exp(log_sig[p] + after)
            if masked:
                w = jnp.where(causal, w, jnp.zeros_like(w))
            vbd = _pair_blockdiag(v_ref[kb, :, ps], first_ref, second_ref)
            acc_s[p] += jnp.exp(run_s[p]) * _dot(w, vbd)
            run_s[p] += _dot(log_keep[p], ones_ref[...])

    last = (i + 1) * n_diag - 1

    def diag_step(j, carry):
        step(last - j, True)
        return carry

    def past_step(j, carry):
        step(i * n_diag - 1 - j, False)
        return carry

    lax.fori_loop(0, n_diag, diag_step, 0)
    lax.fori_loop(0, i * n_diag, past_step, 0)
    for p in range(n_pairs):
        o_ref[:, p * LANES:(p + 1) * LANES] = acc_s[p].astype(o_ref.dtype)


def _stick_breaking(q, k, v, qb):
    bsz, seq, width = q.shape
    nq, nk = seq // qb, seq // ATT_BLOCK
    qspec = pl.BlockSpec((None, None, qb, width), lambda b, i: (b, i, 0, 0))
    kspec = pl.BlockSpec((None, nk, ATT_BLOCK, width), lambda b, i: (b, 0, 0, 0))
    consts = _pair_consts(ATT_BLOCK)
    out = pl.pallas_call(
        functools.partial(_sb_kernel, qb=qb),
        grid=(bsz, nq),
        in_specs=[qspec, kspec, kspec] + [pl.BlockSpec(a.shape, lambda b, i: (0, 0)) for a in consts],
        out_specs=qspec,
        out_shape=jax.ShapeDtypeStruct((bsz, nq, qb, width), BF16),
        scratch_shapes=[pltpu.VMEM((qb, width), BF16),
                        pltpu.VMEM((N_HEADS // 2, qb, LANES), F32),
                        pltpu.VMEM((N_HEADS // 2, qb, LANES), F32)],
        compiler_params=_cparams(("parallel", "arbitrary")),
    )(q.reshape(bsz, nq, qb, width), k.reshape(bsz, nk, ATT_BLOCK, width),
      v.reshape(bsz, nk, ATT_BLOCK, width), *consts)
    return out.reshape(bsz, seq, width)


def _cd_weight(cd_w_in):
    cw = N_HEADS * HEAD_DIM
    iw = IDX_HEADS * IDX_DIM
    offs = [0, cw, 2 * cw, 3 * cw, 3 * cw + iw, 3 * cw + iw + IDX_DIM, 3 * cw + iw + IDX_DIM + IDX_HEADS]
    qkv_c = cd_w_in[:, :offs[3]]
    q_idx = cd_w_in[:, offs[3]:offs[4]]
    k_idx = cd_w_in[:, offs[4]:offs[5]]
    w_idx = cd_w_in[:, offs[5]:offs[6]]
    qkv_d = cd_w_in[:, offs[6]:]
    pad = jnp.zeros((cd_w_in.shape[0], LANES - w_idx.shape[1]), cd_w_in.dtype)
    return jnp.concatenate([qkv_c, q_idx, qkv_d, k_idx, k_idx, w_idx, pad], axis=1)


def kernel(x, c, norm_g, ada_w, ada_b, mlp_w1, mlp_w2, ab_w_in, ab_w_out, pool_w, pool_scale, ssm_lam_re, ssm_lam_im, ssm_log_dt, ssm_b_re, ssm_b_im, ssm_c_re, ssm_c_im, ssm_d, glu_w, glu_b, cd_w_in, cd_w_out, final_g):
    bsz, seq, d = x.shape
    depth = norm_g.shape[0]
    topk = min(TOPK_MAX, seq // 4)
    tm = min(512, seq)

    c_pad = jnp.zeros((SUBLANES, d), F32).at[:bsz].set(c)
    mod = _modulation(c_pad, ada_w.reshape(depth * 2, d, 3 * d), ada_b.reshape(depth * 2, 1, 3 * d))
    mod = mod[:, :, :bsz, None, :]

    for i in range(depth):
        j = i // 2
        shift, scale1, gate1 = mod[2 * i, 0], mod[2 * i, 1], mod[2 * i, 2]
        g = norm_g[i, 0][None, :]
        if i % 2 == 0:
            width = ab_w_in.shape[2]
            tt = min(256, seq)
            (proj,) = _norm_proj(x, g, shift, scale1, ab_w_in[j].astype(BF16), (width,), (BF16,), tm)
            s5 = _s5_params(ssm_lam_re[j], ssm_lam_im[j], ssm_log_dt[j],
                            ssm_b_re[j], ssm_b_im[j], ssm_c_re[j], ssm_c_im[j], seg_len=tt // SUBLANES)
            y = _even_mixer(proj, pool_w[j].astype(BF16), pool_scale[j][None, :], s5,
                            ssm_d[j][None, :], glu_w[j].astype(BF16), glu_b[j][None, :],
                            tt=tt, scan_lanes=512)
            x = _out_proj(x, gate1, (y,), ab_w_out[j].astype(BF16), tm)
        else:
            hw = N_HEADS * HEAD_DIM
            widths = (hw,) * 7 + (LANES, LANES)
            dtypes = (BF16,) * 8 + (F32,)
            qc, kc, vc, qi, qd, kd, vd, kk, wq = _norm_proj(
                x, g, shift, scale1, _cd_weight(cd_w_in[j]).astype(BF16), widths, dtypes, tm)
            oc = _dsa(qc, kc, vc, qi, kk, wq, topk, qb=ATT_QUERY_BLOCK)
            od = _stick_breaking(qd, kd, vd, qb=ATT_QUERY_BLOCK)
            x = _out_proj(x, gate1, (oc, od), cd_w_out[j].astype(BF16), tm)
        shift, scale1, gate1 = mod[2 * i + 1, 0], mod[2 * i + 1, 1], mod[2 * i + 1, 2]
        x = _mlp(x, norm_g[i, 1][None, :], shift, scale1, gate1,
                 mlp_w1[i].astype(BF16), mlp_w2[i].astype(BF16), final_g[None, :],
                 final_norm=(i == depth - 1), tm=min(1024, seq), tf=1024)
    return x
```

```python
import functools
import math

import jax
import jax.numpy as jnp
from jax import lax
from jax.experimental import pallas as pl
from jax.experimental.pallas import tpu as pltpu

F32 = jnp.float32
BF16 = jnp.bfloat16

EPS = 1e-6
CHUNK = 64
POOL_WINDOWS = (2, 4, 8, 16)
POOL_CH = 128
SSM_GROUP = 16
SSM_STATE = 64
HEAD_DIM = 64
N_HEADS = 8
IDX_HEADS = 8
IDX_DIM = 64
IDX_SCALE = (IDX_HEADS ** -0.5) * (IDX_DIM ** -0.5)
TOPK_MAX = 256

LANES = 128
SUBLANES = 8
VMEM_LIMIT = 56 * 1024 * 1024

ATT_BLOCK = 128
ATT_QUERY_BLOCK = 512
NEG_BIG = -1e30
COUNT_ALL = 1e9
BISECT_ITERS = 28
COUNT_ROWS = 128
INTERP_SPAN = 256.0
MAX_POPS = 4096


def _cparams(sem):
    return pltpu.CompilerParams(dimension_semantics=sem, vmem_limit_bytes=VMEM_LIMIT)


def _dot(a, b):
    return jnp.dot(a, b, preferred_element_type=F32)


def _dot_nt(a, b):
    return lax.dot_general(a, b, (((1,), (1,)), ((), ())), preferred_element_type=F32)


def _rowsum(x):
    return jnp.sum(x, axis=1, keepdims=True)


def _rowmax(x):
    return jnp.max(x, axis=1, keepdims=True)


def _norm_modulate(x, g, shift, scale1):
    ms = jnp.mean(x * x, axis=-1, keepdims=True)
    return (x * lax.rsqrt(ms + EPS)) * g * scale1 + shift


def _mod_kernel(c_ref, w_ref, b_ref, o_ref):
    j = pl.program_id(1)
    m = _dot(c_ref[...], w_ref[...]) + b_ref[...]
    o_ref[...] = m + jnp.where(j > 0, 1.0, 0.0)


def _modulation(c_pad, ada_w, ada_b):
    n_sub, d, _ = ada_w.shape
    rows = c_pad.shape[0]
    return pl.pallas_call(
        _mod_kernel,
        grid=(n_sub, 3),
        in_specs=[
            pl.BlockSpec((rows, d), lambda s, j: (0, 0)),
            pl.BlockSpec((None, d, d), lambda s, j: (s, 0, j)),
            pl.BlockSpec((None, 1, d), lambda s, j: (s, 0, j)),
        ],
        out_specs=pl.BlockSpec((None, None, rows, d), lambda s, j: (s, j, 0, 0)),
        out_shape=jax.ShapeDtypeStruct((n_sub, 3, rows, d), F32),
        compiler_params=_cparams(("arbitrary", "arbitrary")),
    )(c_pad, ada_w, ada_b)


def _norm_proj_kernel(x_ref, g_ref, shift_ref, scale_ref, w_ref, *o_refs, widths):
    h = _norm_modulate(x_ref[...], g_ref[...], shift_ref[...], scale_ref[...]).astype(BF16)
    y = _dot(h, w_ref[...])
    off = 0
    for o_ref, width in zip(o_refs, widths):
        o_ref[...] = y[:, off:off + width].astype(o_ref.dtype)
        off += width


def _norm_proj(x, g, shift, scale1, w, widths, dtypes, tm):
    bsz, seq, d = x.shape
    n = w.shape[1]
    vec = pl.BlockSpec((None, 1, d), lambda b, i: (b, 0, 0))
    return pl.pallas_call(
        functools.partial(_norm_proj_kernel, widths=widths),
        grid=(bsz, seq // tm),
        in_specs=[
            pl.BlockSpec((None, tm, d), lambda b, i: (b, i, 0)),
            pl.BlockSpec((1, d), lambda b, i: (0, 0)),
            vec, vec,
            pl.BlockSpec((d, n), lambda b, i: (0, 0)),
        ],
        out_specs=[pl.BlockSpec((None, tm, wd), lambda b, i: (b, i, 0)) for wd in widths],
        out_shape=[jax.ShapeDtypeStruct((bsz, seq, wd), dt) for wd, dt in zip(widths, dtypes)],
        compiler_params=_cparams(("parallel", "parallel")),
    )(x, g, shift, scale1, w)


def _out_proj_kernel(x_ref, gate_ref, *refs, n_y):
    y_refs, w_ref, o_ref = refs[:n_y], refs[n_y], refs[n_y + 1]
    y = jnp.concatenate([r[...] for r in y_refs], axis=1) if n_y > 1 else y_refs[0][...]
    o_ref[...] = x_ref[...] + gate_ref[...] * _dot(y, w_ref[...])


def _out_proj(x, gate1, ys, w, tm):
    bsz, seq, d = x.shape
    k = w.shape[0]
    return pl.pallas_call(
        functools.partial(_out_proj_kernel, n_y=len(ys)),
        grid=(bsz, seq // tm),
        in_specs=[
            pl.BlockSpec((None, tm, d), lambda b, i: (b, i, 0)),
            pl.BlockSpec((None, 1, d), lambda b, i: (b, 0, 0)),
            *[pl.BlockSpec((None, tm, y.shape[2]), lambda b, i: (b, i, 0)) for y in ys],
            pl.BlockSpec((k, d), lambda b, i: (0, 0)),
        ],
        out_specs=pl.BlockSpec((None, tm, d), lambda b, i: (b, i, 0)),
        out_shape=jax.ShapeDtypeStruct((bsz, seq, d), F32),
        compiler_params=_cparams(("parallel", "parallel")),
    )(x, gate1, *ys, w)


def _mlp_kernel(x_ref, g_ref, shift_ref, scale_ref, gate_ref, w1_ref, w2_ref, fg_ref, o_ref,
                h_s, acc_s, *, final_norm):
    f = pl.program_id(2)

    @pl.when(f == 0)
    def _():
        h_s[...] = _norm_modulate(x_ref[...], g_ref[...], shift_ref[...], scale_ref[...]).astype(BF16)
        acc_s[...] = jnp.zeros_like(acc_s)

    a = jnp.maximum(_dot(h_s[...], w1_ref[...]), 0.0)
    acc_s[...] += _dot((a * a).astype(BF16), w2_ref[...])

    @pl.when(f == pl.num_programs(2) - 1)
    def _():
        y = x_ref[...] + gate_ref[...] * acc_s[...]
        if final_norm:
            ms = jnp.mean(y * y, axis=-1, keepdims=True)
            y = (y * lax.rsqrt(ms + EPS)) * fg_ref[...]
        o_ref[...] = y


def _mlp(x, g, shift, scale1, gate1, w1, w2, final_g, final_norm, tm, tf):
    bsz, seq, d = x.shape
    dff = w1.shape[1]
    vec = pl.BlockSpec((None, 1, d), lambda b, i, f: (b, 0, 0))
    one = pl.BlockSpec((1, d), lambda b, i, f: (0, 0))
    return pl.pallas_call(
        functools.partial(_mlp_kernel, final_norm=final_norm),
        grid=(bsz, seq // tm, dff // tf),
        in_specs=[
            pl.BlockSpec((None, tm, d), lambda b, i, f: (b, i, 0)),
            one, vec, vec, vec,
            pl.BlockSpec((d, tf), lambda b, i, f: (0, f)),
            pl.BlockSpec((tf, d), lambda b, i, f: (f, 0)),
            one,
        ],
        out_specs=pl.BlockSpec((None, tm, d), lambda b, i, f: (b, i, 0)),
        out_shape=jax.ShapeDtypeStruct((bsz, seq, d), F32),
        scratch_shapes=[pltpu.VMEM((tm, d), BF16), pltpu.VMEM((tm, d), F32)],
        compiler_params=_cparams(("parallel", "parallel", "arbitrary")),
    )(x, g, shift, scale1, gate1, w1, w2, final_g)


def _cmul(ar, ai, br, bi):
    return ar * br - ai * bi, ar * bi + ai * br


def _s5_params_kernel(lr_ref, li_ref, ldt_ref, brt_ref, bit_ref, crt_ref, cit_ref,
                      wb_ref, wc_ref, a_re_ref, a_im_ref, pk_re_ref, pk_im_ref, sg_re_ref, sg_im_ref):
    lr, li = lr_ref[...], li_ref[...]
    dt = jnp.exp(ldt_ref[...])
    mag = jnp.exp(lr * dt)
    ar, ai = mag * jnp.cos(li * dt), mag * jnp.sin(li * dt)
    den = lr * lr + li * li
    zr = ((ar - 1.0) * lr + ai * li) / den
    zi = (ai * lr - (ar - 1.0) * li) / den

    rows, cols = brt_ref.shape
    rgrp = lax.broadcasted_iota(jnp.int32, (rows, cols), 0) // SSM_GROUP
    cgrp = lax.broadcasted_iota(jnp.int32, (rows, cols), 1) // SSM_STATE
    diag = rgrp == cgrp
    br, bi = brt_ref[...], bit_ref[...]
    wb_ref[0] = jnp.where(diag, zr * br - zi * bi, 0.0).astype(wb_ref.dtype)
    wb_ref[1] = jnp.where(diag, zr * bi + zi * br, 0.0).astype(wb_ref.dtype)
    wc_ref[0] = jnp.where(diag, crt_ref[...], 0.0).astype(wc_ref.dtype)
    wc_ref[1] = jnp.where(diag, cit_ref[...], 0.0).astype(wc_ref.dtype)

    n = lr.shape[1]
    rows8 = pk_re_ref.shape[0]
    seg_len = rows8 // SUBLANES
    a_re_ref[...] = jnp.broadcast_to(ar, (SUBLANES, n))
    a_im_ref[...] = jnp.broadcast_to(ai, (SUBLANES, n))
    expo = lax.broadcasted_iota(jnp.int32, (rows8, n), 0) // SUBLANES + 1
    sq_r, sq_i = jnp.broadcast_to(ar, (rows8, n)), jnp.broadcast_to(ai, (rows8, n))
    pr, pi = jnp.ones((rows8, n), F32), jnp.zeros((rows8, n), F32)
    for b in range(seg_len.bit_length()):
        nr, ni = _cmul(pr, pi, sq_r, sq_i)
        take = ((expo >> b) & 1) == 1
        pr, pi = jnp.where(take, nr, pr), jnp.where(take, ni, pi)
        sq_r, sq_i = _cmul(sq_r, sq_i, sq_r, sq_i)
    pk_re_ref[...] = pr
    pk_im_ref[...] = pi
    row = lax.broadcasted_iota(jnp.int32, (SUBLANES, n), 0)
    gr, gi = pr[rows8 - SUBLANES:], pi[rows8 - SUBLANES:]
    for k in range(3):
        keep = row >= (1 << k)
        sg_re_ref[k] = jnp.where(keep, gr, 0.0)
        sg_im_ref[k] = jnp.where(keep, gi, 0.0)
        gr, gi = _cmul(gr, gi, gr, gi)


def _s5_params(lam_re, lam_im, log_dt, b_re, b_im, c_re, c_im, seg_len):
    g, p = lam_re.shape
    c = b_re.shape[2]
    n = g * p
    flat = lambda a: a.reshape(1, n)
    ldt = jnp.broadcast_to(log_dt[:, None], (g, p)).reshape(1, n)
    b_t = lambda a: jnp.tile(jnp.transpose(a, (2, 0, 1)).reshape(c, n), (g, 1))
    c_t = lambda a: jnp.tile(a.reshape(g * c, p), (1, g))
    outs = pl.pallas_call(
        _s5_params_kernel,
        out_shape=[jax.ShapeDtypeStruct((2, g * c, n), BF16), jax.ShapeDtypeStruct((2, g * c, n), BF16),
                   jax.ShapeDtypeStruct((SUBLANES, n), F32), jax.ShapeDtypeStruct((SUBLANES, n), F32),
                   jax.ShapeDtypeStruct((seg_len * SUBLANES, n), F32), jax.ShapeDtypeStruct((seg_len * SUBLANES, n), F32),
                   jax.ShapeDtypeStruct((3, SUBLANES, n), F32), jax.ShapeDtypeStruct((3, SUBLANES, n), F32)],
        compiler_params=pltpu.CompilerParams(vmem_limit_bytes=VMEM_LIMIT),
    )(flat(lam_re), flat(lam_im), ldt, b_t(b_re), b_t(b_im), c_t(c_re), c_t(c_im))
    wb, wct, a_re, a_im, pk_re, pk_im, sg_re, sg_im = outs
    wc = jnp.transpose(wct, (0, 2, 1))
    pk_re, pk_im = (a.reshape(seg_len, SUBLANES, n) for a in (pk_re, pk_im))
    return wb, wc, a_re, a_im, pk_re, pk_im, sg_re, sg_im


def _gelu_tanh(x):
    return 0.5 * x * (1.0 + jnp.tanh(math.sqrt(2.0 / math.pi) * (x + 0.044715 * (x * x * x))))


def _even_kernel(p_ref, band_ref, perm_ref, wpool_ref, pscale_ref, wb_ref, wc_ref, dskip_ref, wglu_ref, bglu_ref,
                 a_re_ref, a_im_ref, pk_re_ref, pk_im_ref, sg_re_ref, sg_im_ref,
                 o_ref, prev_s, xr_s, xi_s, car_re, car_im, *, mix_a, scan_lanes):
    i = pl.program_id(1)
    tt = p_ref.shape[0]

    @pl.when(i == 0)
    def _():
        prev_s[...] = jnp.zeros_like(prev_s)
        car_re[...] = jnp.zeros_like(car_re)
        car_im[...] = jnp.zeros_like(car_im)

    ua = p_ref[:, :mix_a]
    ext = jnp.concatenate([prev_s[...], ua], axis=0)
    t1 = i * tt + lax.broadcasted_iota(jnp.int32, (tt, POOL_CH), 0) + 1
    for g, win in enumerate(POOL_WINDOWS):
        sl = slice(g * POOL_CH, (g + 1) * POOL_CH)
        wsum = _dot(band_ref[g], ext[:, sl])
        cnt = jnp.minimum(t1, win).astype(F32)
        pg = wsum / cnt - ua[:, sl].astype(F32)
        yg = _dot(pg.astype(BF16), wpool_ref[g]) * pscale_ref[:, sl]
        o_ref[:, sl] = yg.astype(o_ref.dtype)
    prev_s[...] = ua

    ub = _dot(perm_ref[0], p_ref[:, mix_a:]).astype(BF16)
    n_tiles = xr_s.shape[0]
    n_diag_blocks = ub.shape[1] // LANES
    tpb = n_tiles // n_diag_blocks
    sw = tpb * LANES
    for j in range(n_diag_blocks):
        cs, ss = slice(j * LANES, (j + 1) * LANES), slice(j * sw, (j + 1) * sw)
        xr, xi = _dot(ub[:, cs], wb_ref[0, cs, ss]), _dot(ub[:, cs], wb_ref[1, cs, ss])
        for q in range(tpb):
            xr_s[j * tpb + q] = xr[:, q * LANES:(q + 1) * LANES]
            xi_s[j * tpb + q] = xi[:, q * LANES:(q + 1) * LANES]

    seg_len = tt // SUBLANES
    seg_rows = lambda k: pl.ds(pl.multiple_of(k * SUBLANES, SUBLANES), SUBLANES)
    first_seg = lax.broadcasted_iota(jnp.int32, (SUBLANES, LANES), 0) == 0
    zeros = jnp.zeros((SUBLANES, LANES), F32)
    tpc = scan_lanes // LANES
    for c in range(n_tiles // tpc):
        tiles = list(range(c * tpc, (c + 1) * tpc))
        lanes = [slice(t * LANES, (t + 1) * LANES) for t in tiles]

        def local(k, carry, tiles=tiles, lanes=lanes):
            out = []
            for t, ls, (hr, hi) in zip(tiles, lanes, carry):
                dr, di = _cmul(a_re_ref[:, ls], a_im_ref[:, ls], hr, hi)
                hr, hi = dr + xr_s[t, seg_rows(k), :], di + xi_s[t, seg_rows(k), :]
                xr_s[t, seg_rows(k), :] = hr
                xi_s[t, seg_rows(k), :] = hi
                out.append((hr, hi))
            return tuple(out)

        ends = lax.fori_loop(0, seg_len, local, tuple((zeros, zeros) for _ in tiles))
        starts = []
        for t, ls, (er, ei) in zip(tiles, lanes, ends):
            fr = jnp.where(first_seg, car_re[:, ls], pltpu.roll(er, 1, 0))
            fi = jnp.where(first_seg, car_im[:, ls], pltpu.roll(ei, 1, 0))
            for k in range(3):
                dr, di = _cmul(sg_re_ref[k, :, ls], sg_im_ref[k, :, ls],
                               pltpu.roll(fr, 1 << k, 0), pltpu.roll(fi, 1 << k, 0))
                fr, fi = fr + dr, fi + di
            starts.append((fr, fi))
            dr, di = _cmul(pk_re_ref[seg_len - 1, :, ls], pk_im_ref[seg_len - 1, :, ls], fr, fi)
            car_re[:, ls] = jnp.broadcast_to((dr + er)[SUBLANES - 1:], fr.shape)
            car_im[:, ls] = jnp.broadcast_to((di + ei)[SUBLANES - 1:], fr.shape)

        def fix(k, carry, tiles=tiles, lanes=lanes, starts=starts):
            for t, ls, (fr, fi) in zip(tiles, lanes, starts):
                dr, di = _cmul(pk_re_ref[k, :, ls], pk_im_ref[k, :, ls], fr, fi)
                xr_s[t, seg_rows(k), :] += dr
                xi_s[t, seg_rows(k), :] += di
            return carry

        lax.fori_loop(0, seg_len, fix, 0)

    state = lambda ref, j: jnp.concatenate([ref[j * tpb + q] for q in range(tpb)], axis=1).astype(BF16)
    y = jnp.concatenate(
        [_dot(state(xr_s, j), wc_ref[0, j * sw:(j + 1) * sw, j * LANES:(j + 1) * LANES])
         - _dot(state(xi_s, j), wc_ref[1, j * sw:(j + 1) * sw, j * LANES:(j + 1) * LANES])
         for j in range(n_diag_blocks)], axis=1)
    y = _gelu_tanh(y + dskip_ref[...] * ub.astype(F32))
    z = _dot(y.astype(BF16), wglu_ref[...]) + bglu_ref[...]
    out = (y * (1.0 / (1.0 + jnp.exp(-z)))).astype(o_ref.dtype)
    o_ref[:, mix_a:] = _dot(perm_ref[1], out).astype(o_ref.dtype)


def _segment_perm(tt):
    seg_len = tt // SUBLANES
    r = jnp.arange(tt)
    src = (r % SUBLANES) * seg_len + r // SUBLANES
    p = (src[:, None] == jnp.arange(tt)[None, :])
    return jnp.stack([p, p.T]).astype(BF16)


def _pool_band(tt):
    t = jnp.arange(tt)[:, None] + tt
    j = jnp.arange(2 * tt)[None, :]
    return jnp.stack([((j <= t) & (j > t - w)) for w in POOL_WINDOWS]).astype(BF16)


def _even_mixer(proj, wpool, pscale, s5, dskip, wglu, bglu, tt, scan_lanes):
    bsz, seq, width = proj.shape
    wb, wc, a_re, a_im, pk_re, pk_im, sg_re, sg_im = s5
    mix_a = len(POOL_WINDOWS) * POOL_CH
    mix_b = width - mix_a
    n_state = wb.shape[2]
    const = lambda a: pl.BlockSpec(a.shape, lambda b, i, nd=a.ndim: (0,) * nd)
    band = _pool_band(tt)
    args = (band, _segment_perm(tt), wpool, pscale, wb, wc, dskip, wglu, bglu, a_re, a_im, pk_re, pk_im, sg_re, sg_im)
    return pl.pallas_call(
        functools.partial(_even_kernel, mix_a=mix_a, scan_lanes=scan_lanes),
        grid=(bsz, seq // tt),
        in_specs=[pl.BlockSpec((None, tt, width), lambda b, i: (b, i, 0))] + [const(a) for a in args],
        out_specs=pl.BlockSpec((None, tt, width), lambda b, i: (b, i, 0)),
        out_shape=jax.ShapeDtypeStruct((bsz, seq, width), BF16),
        scratch_shapes=[pltpu.VMEM((tt, mix_a), BF16),
                        pltpu.VMEM((n_state // LANES, tt, LANES), F32), pltpu.VMEM((n_state // LANES, tt, LANES), F32),
                        pltpu.VMEM((SUBLANES, n_state), F32), pltpu.VMEM((SUBLANES, n_state), F32)],
        compiler_params=_cparams(("parallel", "arbitrary")),
    )(proj, *args)


def _pair_blockdiag(x, first_ref, second_ref):
    return jnp.concatenate([x * first_ref[...], x * second_ref[...]], axis=0)


def _dsa_kernel(q_ref, qi_ref, wq_ref, k_ref, v_ref, kk_ref, tri_ref, ones_ref, first_ref, second_ref, o_ref,
                s_s, qs_s, mx_s, al_s, *, topk, qb):
    i = pl.program_id(1)
    kb_size = ATT_BLOCK
    n_diag = qb // kb_size
    nkb = (i + 1) * n_diag
    shape = (qb, kb_size)
    row = lax.broadcasted_iota(jnp.int32, shape, 0)
    col = lax.broadcasted_iota(jnp.int32, shape, 1)
    q_chunk = (i * qb + row) // CHUNK
    bcast = lambda a: jnp.broadcast_to(a, shape)
    kf = float(topk)
    pairs = [slice(p * LANES, (p + 1) * LANES) for p in range(N_HEADS // 2)]

    wq = wq_ref[...]
    w_cols = [bcast(wq[:, h:h + 1] * IDX_SCALE) for h in range(IDX_HEADS)]

    def score_block(kb, carry):
        rmax, rmin = carry
        kbd = _pair_blockdiag(kk_ref[kb], first_ref, second_ref)
        acc = jnp.zeros(shape, F32)
        for p, ps in enumerate(pairs):
            rel = jnp.maximum(_dot_nt(qi_ref[:, ps], kbd), 0.0)
            acc = acc + rel[:, :kb_size] * w_cols[2 * p] + rel[:, kb_size:] * w_cols[2 * p + 1]
        adm = ((kb * kb_size + col) // CHUNK) <= q_chunk
        s_s[kb] = jnp.where(adm, acc, -jnp.inf)
        return jnp.maximum(rmax, jnp.where(adm, acc, -jnp.inf)), jnp.minimum(rmin, jnp.where(adm, acc, jnp.inf))

    rmax, rmin = lax.fori_loop(0, nkb, score_block,
                               (jnp.full(shape, -jnp.inf, F32), jnp.full(shape, jnp.inf, F32)))
    n_chunks = qb // kb_size
    chunk_rows = lambda c: slice(c * kb_size, (c + 1) * kb_size)
    chunk_subl = lambda c: slice(c * SUBLANES, (c + 1) * SUBLANES)
    ones8 = jnp.ones((SUBLANES, kb_size), BF16)
    tile = (kb_size, kb_size)

    def to_dense(rep):
        return rep.T[:SUBLANES]

    def to_rows(dense):
        return jnp.broadcast_to(dense[:1], tile).T

    def dense_of(fn):
        return jnp.concatenate([fn(c) for c in range(n_chunks)], axis=0)

    def fold_keys(c, fn, init):
        def body(g, a):
            for u in range(n_diag):
                a = fn(a, s_s[g * n_diag + u, chunk_rows(c), :])
            return a
        return lax.fori_loop(0, i + 1, body, init)

    def count(x, cmp):
        xbs = [to_rows(x[chunk_subl(c)]) for c in range(n_chunks)]
        cnts = [fold_keys(c, lambda a, s, xb=xbs[c]: a + jnp.where(cmp(s, xb), 1.0, 0.0), jnp.zeros(tile, F32))
                for c in range(n_chunks)]
        return jnp.concatenate([_dot_nt(ones8, cnt.astype(BF16)) for cnt in cnts], axis=0)

    count_ge = functools.partial(count, cmp=lambda s, xb: s >= xb)
    count_gt = functools.partial(count, cmp=lambda s, xb: s > xb)
    count_eq = functools.partial(count, cmp=lambda s, xb: s == xb)

    smax = dense_of(lambda c: to_dense(jnp.broadcast_to(_rowmax(rmax[chunk_rows(c)]), tile)))
    smin = dense_of(lambda c: to_dense(jnp.broadcast_to(jnp.min(rmin[chunk_rows(c)], axis=1, keepdims=True), tile)))
    dshape = (n_chunks * SUBLANES, kb_size)
    t_dense = (i * qb + (lax.broadcasted_iota(jnp.int32, dshape, 0) // SUBLANES) * kb_size
               + lax.broadcasted_iota(jnp.int32, dshape, 1))
    n_adm = ((t_dense // CHUNK + 1) * CHUNK).astype(F32)
    hi0 = 2.0 * jnp.maximum(jnp.abs(smax), jnp.abs(smin)) + 1.0
    zero = jnp.zeros_like(smin)
    cge0, cgt0 = count_ge(zero), count_gt(zero)
    wide = n_adm > kf
    zero_tie = jnp.logical_and(wide, jnp.logical_and(cgt0 < kf, cge0 >= kf))
    above = jnp.logical_and(wide, cgt0 >= kf)
    below = jnp.logical_and(wide, cge0 < kf)
    lo0 = jnp.where(jnp.logical_or(above, zero_tie), 0.0, smin)
    clo0 = jnp.where(zero_tie, kf, jnp.where(above, cge0, n_adm))
    hi0 = jnp.where(below, 0.0, hi0)
    chi0 = jnp.where(below, cge0, 0.0)
    need0 = jnp.where(zero_tie, kf - cgt0, COUNT_ALL)

    def unresolved(clo):
        return jnp.max(jnp.where(clo > kf, 1.0, 0.0))

    def bisect_cond(carry):
        return jnp.logical_and(carry[0] > 0.0, carry[1] < BISECT_ITERS)

    def bisect(carry):
        _, it, lo, hi, clo, chi = carry
        span = clo - chi
        aim = (clo - (kf + 0.5)) / span
        interpolate = jnp.logical_and(span < INTERP_SPAN, it % 2 == 0)
        mid = lo + jnp.where(interpolate, aim, 0.5) * (hi - lo)
        c = count_ge(mid)
        active = clo > kf
        up = jnp.logical_and(active, c >= kf)
        down = jnp.logical_and(active, c < kf)
        clo = jnp.where(up, c, clo)
        return (unresolved(clo), it + 1, jnp.where(up, mid, lo), jnp.where(down, mid, hi), clo,
                jnp.where(down, c, chi))

    _, _, lo, hi, clo, chi = lax.while_loop(
        bisect_cond, bisect, (unresolved(clo0), jnp.int32(0), lo0, hi0, clo0, chi0))

    done0 = jnp.where(clo <= kf, 1.0, 0.0)

    def pop_cond(carry):
        return jnp.logical_and(carry[0] > 0.0, carry[1] < MAX_POPS)

    def next_below(hi):
        def one(c):
            hib = to_rows(hi[chunk_subl(c)])
            best = fold_keys(c, lambda a, s: jnp.maximum(a, jnp.where(s < hib, s, -jnp.inf)),
                             jnp.full(tile, -jnp.inf, F32))
            return to_dense(jnp.broadcast_to(_rowmax(best), tile))
        return dense_of(one)

    def pop(carry):
        _, it, hi, chi, thr, need, done = carry
        v = next_below(hi)
        cv = count_eq(v)
        active = done < 0.5
        fin = jnp.logical_and(active, chi + cv >= kf)
        go = jnp.logical_and(active, chi + cv < kf)
        thr = jnp.where(fin, v, thr)
        need = jnp.where(fin, kf - chi, need)
        done = jnp.where(fin, 1.0, done)
        hi = jnp.where(go, v, hi)
        chi = jnp.where(go, chi + cv, chi)
        return (jnp.max(1.0 - done), it + 1, hi, chi, thr, need, done)

    left0 = jnp.max(1.0 - done0)
    _, _, _, _, thr, need, _ = lax.while_loop(
        pop_cond, pop, (left0, jnp.int32(0), hi, chi, lo, need0, done0))
    thrb = jnp.concatenate([to_rows(thr[chunk_subl(c)]) for c in range(n_chunks)], axis=0)
    needb = jnp.concatenate([to_rows(need[chunk_subl(c)]) for c in range(n_chunks)], axis=0)
    partial_ties = jnp.max(jnp.where(need < COUNT_ALL, 1.0, 0.0))

    @pl.when(partial_ties > 0.0)
    def _():
        def tie_block(kb, run):
            s = s_s[kb]
            eq = jnp.where(s == thrb, 1.0, 0.0)
            rank = _dot(eq.astype(BF16), tri_ref[1]) + run
            take = jnp.where(s == thrb, jnp.where(rank < needb, 0.0, NEG_BIG), NEG_BIG)
            s_s[kb] = jnp.where(s > thrb, 0.0, take)
            return run + bcast(_rowsum(eq))
        lax.fori_loop(0, nkb, tie_block, jnp.zeros(shape, F32))

    @pl.when(partial_ties <= 0.0)
    def _():
        def sel_block(kb, carry):
            s_s[kb] = jnp.where(s_s[kb] >= thrb, 0.0, NEG_BIG)
            return carry
        lax.fori_loop(0, nkb, sel_block, 0)

    rel_pos = (row - col).astype(F32)
    first_half = col < HEAD_DIM
    qs_s[...] = q_ref[...] * (HEAD_DIM ** -0.5)
    mx_s[...] = jnp.full(mx_s.shape, NEG_BIG, F32)
    al_s[...] = jnp.zeros_like(al_s)
    slopes = [2.0 ** (-8.0 * (h + 1) / N_HEADS) for h in range(N_HEADS)]

    def attend(g, carry):
        blocks = [g * n_diag + u for u in range(n_diag)]
        dists = [jnp.abs(rel_pos + (i * qb - kb * kb_size).astype(F32)) for kb in blocks]
        for p, ps in enumerate(pairs):
            tiles = []
            for kb, dist in zip(blocks, dists):
                mask = s_s[kb]
                kbd = _pair_blockdiag(k_ref[kb, :, ps], first_ref, second_ref)
                bias = jnp.concatenate([mask - slopes[2 * p] * dist, mask - slopes[2 * p + 1] * dist], axis=1)
                tiles.append(_dot_nt(qs_s[:, ps], kbd) + bias)
            top = functools.reduce(jnp.maximum, tiles)
            m_old = mx_s[p]
            m_new = jnp.maximum(m_old, jnp.concatenate(
                [bcast(_rowmax(top[:, :kb_size])), bcast(_rowmax(top[:, kb_size:]))], axis=1))
            mx_s[p] = m_new
            pexp = jnp.concatenate([jnp.exp((t - m_new).astype(BF16)) for t in tiles], axis=1)
            shrink = m_old - m_new
            alpha = jnp.exp(jnp.where(first_half, shrink[:, :kb_size], shrink[:, kb_size:]))
            values = jnp.concatenate(
                [jnp.concatenate([_pair_blockdiag(v_ref[kb, :, ps], first_ref, second_ref), ones_ref[...]], axis=1)
                 for kb in blocks], axis=0)
            al_s[p] = jnp.concatenate([alpha, alpha], axis=1) * al_s[p] + _dot(pexp, values)
        return carry

    lax.fori_loop(0, i + 1, attend, 0)
    for p, ps in enumerate(pairs):
        state = al_s[p]
        o_ref[:, ps] = (state[:, :LANES] / state[:, LANES:]).astype(o_ref.dtype)


def _tri_consts(n):
    j = jnp.arange(n)[:, None]
    s = jnp.arange(n)[None, :]
    return jnp.stack([j > s, j < s]).astype(BF16)


def _pair_consts(n):
    tri = _tri_consts(n)
    zero = jnp.zeros((n, n), BF16)
    tri2 = jnp.block([[tri[0], zero], [zero, tri[0]]])
    first = (jnp.arange(n)[None, :] < n // 2) & jnp.ones((n, 1), bool)
    ones2 = jnp.concatenate([first, ~first], axis=0).astype(BF16)
    return tri2, ones2, first.astype(BF16), (~first).astype(BF16)


def _dsa(q, k, v, qi, kk, wq, topk, qb):
    bsz, seq, width = q.shape
    nq, nk = seq // qb, seq // ATT_BLOCK
    qblk = lambda a: a.reshape(bsz, nq, qb, a.shape[-1])
    kblk = lambda a: a.reshape(bsz, nk, ATT_BLOCK, a.shape[-1])
    qspec = lambda wd: pl.BlockSpec((None, None, qb, wd), lambda b, i: (b, i, 0, 0))
    kspec = lambda wd: pl.BlockSpec((None, nk, ATT_BLOCK, wd), lambda b, i: (b, 0, 0, 0))
    _, ones2, first, second = _pair_consts(ATT_BLOCK)
    consts = (_tri_consts(ATT_BLOCK), ones2, first, second)
    out = pl.pallas_call(
        functools.partial(_dsa_kernel, topk=topk, qb=qb),
        grid=(bsz, nq),
        in_specs=[qspec(width), qspec(width), qspec(LANES), kspec(width), kspec(width), kspec(LANES)]
                 + [pl.BlockSpec(a.shape, lambda b, i, nd=a.ndim: (0,) * nd) for a in consts],
        out_specs=qspec(width),
        out_shape=jax.ShapeDtypeStruct((bsz, nq, qb, width), BF16),
        scratch_shapes=[pltpu.VMEM((nk, qb, ATT_BLOCK), F32),
                        pltpu.VMEM((qb, width), BF16),
                        pltpu.VMEM((N_HEADS // 2, qb, 2 * ATT_BLOCK), F32),
                        pltpu.VMEM((N_HEADS // 2, qb, 2 * LANES), F32)],
        compiler_params=_cparams(("parallel", "arbitrary")),
    )(qblk(q), qblk(qi), qblk(wq), kblk(k), kblk(v), kblk(kk), *consts)
    return out.reshape(bsz, seq, width)


def _sb_kernel(q_ref, k_ref, v_ref, tri_ref, ones_ref, first_ref, second_ref, o_ref, qs_s, run_s, acc_s, *, qb):
    i = pl.program_id(1)
    kb_size = ATT_BLOCK
    n_diag = qb // kb_size
    n_pairs = N_HEADS // 2
    shape = (qb, 2 * kb_size)
    row = lax.broadcasted_iota(jnp.int32, shape, 0)
    col = lax.broadcasted_iota(jnp.int32, shape, 1) % kb_size
    qs_s[...] = q_ref[...] * (HEAD_DIM ** -0.5)
    run_s[...] = jnp.zeros_like(run_s)
    acc_s[...] = jnp.zeros_like(acc_s)

    def step(kb, masked):
        causal = (kb * kb_size + col) < (i * qb + row)
        pairs = [slice(p * LANES, (p + 1) * LANES) for p in range(n_pairs)]
        log_sig, log_keep = [], []
        for ps in pairs:
            kbd = _pair_blockdiag(k_ref[kb, :, ps], first_ref, second_ref)
            z = _dot_nt(qs_s[:, ps], kbd).astype(BF16)
            soft = jnp.log(1.0 + jnp.exp(-jnp.abs(z)))
            keep = -jnp.maximum(z, 0.0) - soft
            log_sig.append(jnp.minimum(z, 0.0) - soft)
            log_keep.append(jnp.where(causal, keep, jnp.zeros_like(keep)) if masked else keep)
        for p, ps in enumerate(pairs):
            after = _dot(log_keep[p], tri_ref[...]).astype(BF16)
            w = jnp.exp(log_sig[p] + after)
            if masked:
                w = jnp.where(causal, w, jnp.zeros_like(w))
            vbd = _pair_blockdiag(v_ref[kb, :, ps], first_ref, second_ref)
            acc_s[p] += jnp.exp(run_s[p]) * _dot(w, vbd)
            run_s[p] += _dot(log_keep[p], ones_ref[...])

    last = (i + 1) * n_diag - 1

    def diag_step(j, carry):
        step(last - j, True)
        return carry

    def past_step(j, carry):
        step(i * n_diag - 1 - j, False)
        return carry

    lax.fori_loop(0, n_diag, diag_step, 0)
    lax.fori_loop(0, i * n_diag, past_step, 0)
    for p in range(n_pairs):
        o_ref[:, p * LANES:(p + 1) * LANES] = acc_s[p].astype(o_ref.dtype)


def _stick_breaking(q, k, v, qb):
    bsz, seq, width = q.shape
    nq, nk = seq // qb, seq // ATT_BLOCK
    qspec = pl.BlockSpec((None, None, qb, width), lambda b, i: (b, i, 0, 0))
    kspec = pl.BlockSpec((None, nk, ATT_BLOCK, width), lambda b, i: (b, 0, 0, 0))
    consts = _pair_consts(ATT_BLOCK)
    out = pl.pallas_call(
        functools.partial(_sb_kernel, qb=qb),
        grid=(bsz, nq),
        in_specs=[qspec, kspec, kspec] + [pl.BlockSpec(a.shape, lambda b, i: (0, 0)) for a in consts],
        out_specs=qspec,
        out_shape=jax.ShapeDtypeStruct((bsz, nq, qb, width), BF16),
        scratch_shapes=[pltpu.VMEM((qb, width), BF16),
                        pltpu.VMEM((N_HEADS // 2, qb, LANES), F32),
                        pltpu.VMEM((N_HEADS // 2, qb, LANES), F32)],
        compiler_params=_cparams(("parallel", "arbitrary")),
    )(q.reshape(bsz, nq, qb, width), k.reshape(bsz, nk, ATT_BLOCK, width),
      v.reshape(bsz, nk, ATT_BLOCK, width), *consts)
    return out.reshape(bsz, seq, width)


def _cd_weight(cd_w_in):
    cw = N_HEADS * HEAD_DIM
    iw = IDX_HEADS * IDX_DIM
    offs = [0, cw, 2 * cw, 3 * cw, 3 * cw + iw, 3 * cw + iw + IDX_DIM, 3 * cw + iw + IDX_DIM + IDX_HEADS]
    qkv_c = cd_w_in[:, :offs[3]]
    q_idx = cd_w_in[:, offs[3]:offs[4]]
    k_idx = cd_w_in[:, offs[4]:offs[5]]
    w_idx = cd_w_in[:, offs[5]:offs[6]]
    qkv_d = cd_w_in[:, offs[6]:]
    pad = jnp.zeros((cd_w_in.shape[0], LANES - w_idx.shape[1]), cd_w_in.dtype)
    return jnp.concatenate([qkv_c, q_idx, qkv_d, k_idx, k_idx, w_idx, pad], axis=1)


def kernel(x, c, norm_g, ada_w, ada_b, mlp_w1, mlp_w2, ab_w_in, ab_w_out, pool_w, pool_scale, ssm_lam_re, ssm_lam_im, ssm_log_dt, ssm_b_re, ssm_b_im, ssm_c_re, ssm_c_im, ssm_d, glu_w, glu_b, cd_w_in, cd_w_out, final_g):
    bsz, seq, d = x.shape
    depth = norm_g.shape[0]
    topk = min(TOPK_MAX, seq // 4)
    tm = min(512, seq)

    c_pad = jnp.zeros((SUBLANES, d), F32).at[:bsz].set(c)
    mod = _modulation(c_pad, ada_w.reshape(depth * 2, d, 3 * d), ada_b.reshape(depth * 2, 1, 3 * d))
    mod = mod[:, :, :bsz, None, :]

    for i in range(depth):
        j = i // 2
        shift, scale1, gate1 = mod[2 * i, 0], mod[2 * i, 1], mod[2 * i, 2]
        g = norm_g[i, 0][None, :]
        if i % 2 == 0:
            width = ab_w_in.shape[2]
            tt = min(256, seq)
            (proj,) = _norm_proj(x, g, shift, scale1, ab_w_in[j].astype(BF16), (width,), (BF16,), tm)
            s5 = _s5_params(ssm_lam_re[j], ssm_lam_im[j], ssm_log_dt[j],
                            ssm_b_re[j], ssm_b_im[j], ssm_c_re[j], ssm_c_im[j], seg_len=tt // SUBLANES)
            y = _even_mixer(proj, pool_w[j].astype(BF16), pool_scale[j][None, :], s5,
                            ssm_d[j][None, :], glu_w[j].astype(BF16), glu_b[j][None, :],
                            tt=tt, scan_lanes=512)
            x = _out_proj(x, gate1, (y,), ab_w_out[j].astype(BF16), tm)
        else:
            hw = N_HEADS * HEAD_DIM
            widths = (hw,) * 7 + (LANES, LANES)
            dtypes = (BF16,) * 8 + (F32,)
            qc, kc, vc, qi, qd, kd, vd, kk, wq = _norm_proj(
                x, g, shift, scale1, _cd_weight(cd_w_in[j]).astype(BF16), widths, dtypes, tm)
            oc = _dsa(qc, kc, vc, qi, kk, wq, topk, qb=ATT_QUERY_BLOCK)
            od = _stick_breaking(qd, kd, vd, qb=ATT_QUERY_BLOCK)
            x = _out_proj(x, gate1, (oc, od), cd_w_out[j].astype(BF16), tm)
        shift, scale1, gate1 = mod[2 * i + 1, 0], mod[2 * i + 1, 1], mod[2 * i + 1, 2]
        x = _mlp(x, norm_g[i, 1][None, :], shift, scale1, gate1,
                 mlp_w1[i].astype(BF16), mlp_w2[i].astype(BF16), final_g[None, :],
                 final_norm=(i == depth - 1), tm=min(1024, seq), tf=1024)
    return x
```

```python
import functools
import math

import jax
import jax.numpy as jnp
from jax import lax
from jax.experimental import pallas as pl
from jax.experimental.pallas import tpu as pltpu

F32 = jnp.float32
BF16 = jnp.bfloat16

EPS = 1e-6
CHUNK = 64
POOL_WINDOWS = (2, 4, 8, 16)
POOL_CH = 128
SSM_GROUP = 16
SSM_STATE = 64
HEAD_DIM = 64
N_HEADS = 8
IDX_HEADS = 8
IDX_DIM = 64
IDX_SCALE = (IDX_HEADS ** -0.5) * (IDX_DIM ** -0.5)
TOPK_MAX = 256

LANES = 128
SUBLANES = 8
VMEM_LIMIT = 56 * 1024 * 1024

ATT_BLOCK = 128
ATT_QUERY_BLOCK = 512
NEG_BIG = -1e30
COUNT_ALL = 1e9
BISECT_ITERS = 28
COUNT_ROWS = 128
POP_SPAN = 2.0
INTERP_SPAN = 256.0
MAX_POPS = 4096


def _cparams(sem):
    return pltpu.CompilerParams(dimension_semantics=sem, vmem_limit_bytes=VMEM_LIMIT)


def _dot(a, b):
    return jnp.dot(a, b, preferred_element_type=F32)


def _dot_nt(a, b):
    return lax.dot_general(a, b, (((1,), (1,)), ((), ())), preferred_element_type=F32)


def _rowsum(x):
    return jnp.sum(x, axis=1, keepdims=True)


def _rowmax(x):
    return jnp.max(x, axis=1, keepdims=True)


def _norm_modulate(x, g, shift, scale1):
    ms = jnp.mean(x * x, axis=-1, keepdims=True)
    return (x * lax.rsqrt(ms + EPS)) * g * scale1 + shift


def _mod_kernel(c_ref, w_ref, b_ref, o_ref):
    j = pl.program_id(1)
    m = _dot(c_ref[...], w_ref[...]) + b_ref[...]
    o_ref[...] = m + jnp.where(j > 0, 1.0, 0.0)


def _modulation(c_pad, ada_w, ada_b):
    n_sub, d, _ = ada_w.shape
    rows = c_pad.shape[0]
    return pl.pallas_call(
        _mod_kernel,
        grid=(n_sub, 3),
        in_specs=[
            pl.BlockSpec((rows, d), lambda s, j: (0, 0)),
            pl.BlockSpec((None, d, d), lambda s, j: (s, 0, j)),
            pl.BlockSpec((None, 1, d), lambda s, j: (s, 0, j)),
        ],
        out_specs=pl.BlockSpec((None, None, rows, d), lambda s, j: (s, j, 0, 0)),
        out_shape=jax.ShapeDtypeStruct((n_sub, 3, rows, d), F32),
        compiler_params=_cparams(("arbitrary", "arbitrary")),
    )(c_pad, ada_w, ada_b)


def _norm_proj_kernel(x_ref, g_ref, shift_ref, scale_ref, w_ref, *o_refs, widths):
    h = _norm_modulate(x_ref[...], g_ref[...], shift_ref[...], scale_ref[...]).astype(BF16)
    y = _dot(h, w_ref[...])
    off = 0
    for o_ref, width in zip(o_refs, widths):
        o_ref[...] = y[:, off:off + width].astype(o_ref.dtype)
        off += width


def _norm_proj(x, g, shift, scale1, w, widths, dtypes, tm):
    bsz, seq, d = x.shape
    n = w.shape[1]
    vec = pl.BlockSpec((None, 1, d), lambda b, i: (b, 0, 0))
    return pl.pallas_call(
        functools.partial(_norm_proj_kernel, widths=widths),
        grid=(bsz, seq // tm),
        in_specs=[
            pl.BlockSpec((None, tm, d), lambda b, i: (b, i, 0)),
            pl.BlockSpec((1, d), lambda b, i: (0, 0)),
            vec, vec,
            pl.BlockSpec((d, n), lambda b, i: (0, 0)),
        ],
        out_specs=[pl.BlockSpec((None, tm, wd), lambda b, i: (b, i, 0)) for wd in widths],
        out_shape=[jax.ShapeDtypeStruct((bsz, seq, wd), dt) for wd, dt in zip(widths, dtypes)],
        compiler_params=_cparams(("parallel", "parallel")),
    )(x, g, shift, scale1, w)


def _out_proj_kernel(x_ref, gate_ref, *refs, n_y):
    y_refs, w_ref, o_ref = refs[:n_y], refs[n_y], refs[n_y + 1]
    y = jnp.concatenate([r[...] for r in y_refs], axis=1) if n_y > 1 else y_refs[0][...]
    o_ref[...] = x_ref[...] + gate_ref[...] * _dot(y, w_ref[...])


def _out_proj(x, gate1, ys, w, tm):
    bsz, seq, d = x.shape
    k = w.shape[0]
    return pl.pallas_call(
        functools.partial(_out_proj_kernel, n_y=len(ys)),
        grid=(bsz, seq // tm),
        in_specs=[
            pl.BlockSpec((None, tm, d), lambda b, i: (b, i, 0)),
            pl.BlockSpec((None, 1, d), lambda b, i: (b, 0, 0)),
            *[pl.BlockSpec((None, tm, y.shape[2]), lambda b, i: (b, i, 0)) for y in ys],
            pl.BlockSpec((k, d), lambda b, i: (0, 0)),
        ],
        out_specs=pl.BlockSpec((None, tm, d), lambda b, i: (b, i, 0)),
        out_shape=jax.ShapeDtypeStruct((bsz, seq, d), F32),
        compiler_params=_cparams(("parallel", "parallel")),
    )(x, gate1, *ys, w)


def _mlp_kernel(x_ref, g_ref, shift_ref, scale_ref, gate_ref, w1_ref, w2_ref, fg_ref, o_ref,
                h_s, acc_s, *, final_norm):
    f = pl.program_id(2)

    @pl.when(f == 0)
    def _():
        h_s[...] = _norm_modulate(x_ref[...], g_ref[...], shift_ref[...], scale_ref[...]).astype(BF16)
        acc_s[...] = jnp.zeros_like(acc_s)

    a = jnp.maximum(_dot(h_s[...], w1_ref[...]), 0.0)
    acc_s[...] += _dot((a * a).astype(BF16), w2_ref[...])

    @pl.when(f == pl.num_programs(2) - 1)
    def _():
        y = x_ref[...] + gate_ref[...] * acc_s[...]
        if final_norm:
            ms = jnp.mean(y * y, axis=-1, keepdims=True)
            y = (y * lax.rsqrt(ms + EPS)) * fg_ref[...]
        o_ref[...] = y


def _mlp(x, g, shift, scale1, gate1, w1, w2, final_g, final_norm, tm, tf):
    bsz, seq, d = x.shape
    dff = w1.shape[1]
    vec = pl.BlockSpec((None, 1, d), lambda b, i, f: (b, 0, 0))
    one = pl.BlockSpec((1, d), lambda b, i, f: (0, 0))
    return pl.pallas_call(
        functools.partial(_mlp_kernel, final_norm=final_norm),
        grid=(bsz, seq // tm, dff // tf),
        in_specs=[
            pl.BlockSpec((None, tm, d), lambda b, i, f: (b, i, 0)),
            one, vec, vec, vec,
            pl.BlockSpec((d, tf), lambda b, i, f: (0, f)),
            pl.BlockSpec((tf, d), lambda b, i, f: (f, 0)),
            one,
        ],
        out_specs=pl.BlockSpec((None, tm, d), lambda b, i, f: (b, i, 0)),
        out_shape=jax.ShapeDtypeStruct((bsz, seq, d), F32),
        scratch_shapes=[pltpu.VMEM((tm, d), BF16), pltpu.VMEM((tm, d), F32)],
        compiler_params=_cparams(("parallel", "parallel", "arbitrary")),
    )(x, g, shift, scale1, gate1, w1, w2, final_g)


def _cmul(ar, ai, br, bi):
    return ar * br - ai * bi, ar * bi + ai * br


def _s5_params_kernel(lr_ref, li_ref, ldt_ref, brt_ref, bit_ref, crt_ref, cit_ref,
                      wb_ref, wc_ref, a_re_ref, a_im_ref, pk_re_ref, pk_im_ref, sg_re_ref, sg_im_ref):
    lr, li = lr_ref[...], li_ref[...]
    dt = jnp.exp(ldt_ref[...])
    mag = jnp.exp(lr * dt)
    ar, ai = mag * jnp.cos(li * dt), mag * jnp.sin(li * dt)
    den = lr * lr + li * li
    zr = ((ar - 1.0) * lr + ai * li) / den
    zi = (ai * lr - (ar - 1.0) * li) / den

    rows, cols = brt_ref.shape
    rgrp = lax.broadcasted_iota(jnp.int32, (rows, cols), 0) // SSM_GROUP
    cgrp = lax.broadcasted_iota(jnp.int32, (rows, cols), 1) // SSM_STATE
    diag = rgrp == cgrp
    br, bi = brt_ref[...], bit_ref[...]
    wb_ref[0] = jnp.where(diag, zr * br - zi * bi, 0.0).astype(wb_ref.dtype)
    wb_ref[1] = jnp.where(diag, zr * bi + zi * br, 0.0).astype(wb_ref.dtype)
    wc_ref[0] = jnp.where(diag, crt_ref[...], 0.0).astype(wc_ref.dtype)
    wc_ref[1] = jnp.where(diag, cit_ref[...], 0.0).astype(wc_ref.dtype)

    n = lr.shape[1]
    rows8 = pk_re_ref.shape[0]
    seg_len = rows8 // SUBLANES
    a_re_ref[...] = jnp.broadcast_to(ar, (SUBLANES, n))
    a_im_ref[...] = jnp.broadcast_to(ai, (SUBLANES, n))
    expo = lax.broadcasted_iota(jnp.int32, (rows8, n), 0) // SUBLANES + 1
    sq_r, sq_i = jnp.broadcast_to(ar, (rows8, n)), jnp.broadcast_to(ai, (rows8, n))
    pr, pi = jnp.ones((rows8, n), F32), jnp.zeros((rows8, n), F32)
    for b in range(seg_len.bit_length()):
        nr, ni = _cmul(pr, pi, sq_r, sq_i)
        take = ((expo >> b) & 1) == 1
        pr, pi = jnp.where(take, nr, pr), jnp.where(take, ni, pi)
        sq_r, sq_i = _cmul(sq_r, sq_i, sq_r, sq_i)
    pk_re_ref[...] = pr
    pk_im_ref[...] = pi
    row = lax.broadcasted_iota(jnp.int32, (SUBLANES, n), 0)
    gr, gi = pr[rows8 - SUBLANES:], pi[rows8 - SUBLANES:]
    for k in range(3):
        keep = row >= (1 << k)
        sg_re_ref[k] = jnp.where(keep, gr, 0.0)
        sg_im_ref[k] = jnp.where(keep, gi, 0.0)
        gr, gi = _cmul(gr, gi, gr, gi)


def _s5_params(lam_re, lam_im, log_dt, b_re, b_im, c_re, c_im, seg_len):
    g, p = lam_re.shape
    c = b_re.shape[2]
    n = g * p
    flat = lambda a: a.reshape(1, n)
    ldt = jnp.broadcast_to(log_dt[:, None], (g, p)).reshape(1, n)
    b_t = lambda a: jnp.tile(jnp.transpose(a, (2, 0, 1)).reshape(c, n), (g, 1))
    c_t = lambda a: jnp.tile(a.reshape(g * c, p), (1, g))
    outs = pl.pallas_call(
        _s5_params_kernel,
        out_shape=[jax.ShapeDtypeStruct((2, g * c, n), BF16), jax.ShapeDtypeStruct((2, g * c, n), BF16),
                   jax.ShapeDtypeStruct((SUBLANES, n), F32), jax.ShapeDtypeStruct((SUBLANES, n), F32),
                   jax.ShapeDtypeStruct((seg_len * SUBLANES, n), F32), jax.ShapeDtypeStruct((seg_len * SUBLANES, n), F32),
                   jax.ShapeDtypeStruct((3, SUBLANES, n), F32), jax.ShapeDtypeStruct((3, SUBLANES, n), F32)],
        compiler_params=pltpu.CompilerParams(vmem_limit_bytes=VMEM_LIMIT),
    )(flat(lam_re), flat(lam_im), ldt, b_t(b_re), b_t(b_im), c_t(c_re), c_t(c_im))
    wb, wct, a_re, a_im, pk_re, pk_im, sg_re, sg_im = outs
    wc = jnp.transpose(wct, (0, 2, 1))
    pk_re, pk_im = (a.reshape(seg_len, SUBLANES, n) for a in (pk_re, pk_im))
    return wb, wc, a_re, a_im, pk_re, pk_im, sg_re, sg_im


def _gelu_tanh(x):
    return 0.5 * x * (1.0 + jnp.tanh(math.sqrt(2.0 / math.pi) * (x + 0.044715 * (x * x * x))))


def _even_kernel(p_ref, band_ref, perm_ref, wpool_ref, pscale_ref, wb_ref, wc_ref, dskip_ref, wglu_ref, bglu_ref,
                 a_re_ref, a_im_ref, pk_re_ref, pk_im_ref, sg_re_ref, sg_im_ref,
                 o_ref, prev_s, xr_s, xi_s, car_re, car_im, *, mix_a, scan_lanes):
    i = pl.program_id(1)
    tt = p_ref.shape[0]

    @pl.when(i == 0)
    def _():
        prev_s[...] = jnp.zeros_like(prev_s)
        car_re[...] = jnp.zeros_like(car_re)
        car_im[...] = jnp.zeros_like(car_im)

    ua = p_ref[:, :mix_a]
    ext = jnp.concatenate([prev_s[...], ua], axis=0)
    t1 = i * tt + lax.broadcasted_iota(jnp.int32, (tt, POOL_CH), 0) + 1
    for g, win in enumerate(POOL_WINDOWS):
        sl = slice(g * POOL_CH, (g + 1) * POOL_CH)
        wsum = _dot(band_ref[g], ext[:, sl])
        cnt = jnp.minimum(t1, win).astype(F32)
        pg = wsum / cnt - ua[:, sl].astype(F32)
        yg = _dot(pg.astype(BF16), wpool_ref[g]) * pscale_ref[:, sl]
        o_ref[:, sl] = yg.astype(o_ref.dtype)
    prev_s[...] = ua

    ub = _dot(perm_ref[0], p_ref[:, mix_a:]).astype(BF16)
    n_tiles = xr_s.shape[0]
    n_diag_blocks = ub.shape[1] // LANES
    tpb = n_tiles // n_diag_blocks
    sw = tpb * LANES
    for j in range(n_diag_blocks):
        cs, ss = slice(j * LANES, (j + 1) * LANES), slice(j * sw, (j + 1) * sw)
        xr, xi = _dot(ub[:, cs], wb_ref[0, cs, ss]), _dot(ub[:, cs], wb_ref[1, cs, ss])
        for q in range(tpb):
            xr_s[j * tpb + q] = xr[:, q * LANES:(q + 1) * LANES]
            xi_s[j * tpb + q] = xi[:, q * LANES:(q + 1) * LANES]

    seg_len = tt // SUBLANES
    seg_rows = lambda k: pl.ds(pl.multiple_of(k * SUBLANES, SUBLANES), SUBLANES)
    first_seg = lax.broadcasted_iota(jnp.int32, (SUBLANES, LANES), 0) == 0
    zeros = jnp.zeros((SUBLANES, LANES), F32)
    tpc = scan_lanes // LANES
    for c in range(n_tiles // tpc):
        tiles = list(range(c * tpc, (c + 1) * tpc))
        lanes = [slice(t * LANES, (t + 1) * LANES) for t in tiles]

        def local(k, carry, tiles=tiles, lanes=lanes):
            out = []
            for t, ls, (hr, hi) in zip(tiles, lanes, carry):
                dr, di = _cmul(a_re_ref[:, ls], a_im_ref[:, ls], hr, hi)
                hr, hi = dr + xr_s[t, seg_rows(k), :], di + xi_s[t, seg_rows(k), :]
                xr_s[t, seg_rows(k), :] = hr
                xi_s[t, seg_rows(k), :] = hi
                out.append((hr, hi))
            return tuple(out)

        ends = lax.fori_loop(0, seg_len, local, tuple((zeros, zeros) for _ in tiles))
        starts = []
        for t, ls, (er, ei) in zip(tiles, lanes, ends):
            fr = jnp.where(first_seg, car_re[:, ls], pltpu.roll(er, 1, 0))
            fi = jnp.where(first_seg, car_im[:, ls], pltpu.roll(ei, 1, 0))
            for k in range(3):
                dr, di = _cmul(sg_re_ref[k, :, ls], sg_im_ref[k, :, ls],
                               pltpu.roll(fr, 1 << k, 0), pltpu.roll(fi, 1 << k, 0))
                fr, fi = fr + dr, fi + di
            starts.append((fr, fi))
            dr, di = _cmul(pk_re_ref[seg_len - 1, :, ls], pk_im_ref[seg_len - 1, :, ls], fr, fi)
            car_re[:, ls] = jnp.broadcast_to((dr + er)[SUBLANES - 1:], fr.shape)
            car_im[:, ls] = jnp.broadcast_to((di + ei)[SUBLANES - 1:], fr.shape)

        def fix(k, carry, tiles=tiles, lanes=lanes, starts=starts):
            for t, ls, (fr, fi) in zip(tiles, lanes, starts):
                dr, di = _cmul(pk_re_ref[k, :, ls], pk_im_ref[k, :, ls], fr, fi)
                xr_s[t, seg_rows(k), :] += dr
                xi_s[t, seg_rows(k), :] += di
            return carry

        lax.fori_loop(0, seg_len, fix, 0)

    state = lambda ref, j: jnp.concatenate([ref[j * tpb + q] for q in range(tpb)], axis=1).astype(BF16)
    y = jnp.concatenate(
        [_dot(state(xr_s, j), wc_ref[0, j * sw:(j + 1) * sw, j * LANES:(j + 1) * LANES])
         - _dot(state(xi_s, j), wc_ref[1, j * sw:(j + 1) * sw, j * LANES:(j + 1) * LANES])
         for j in range(n_diag_blocks)], axis=1)
    y = _gelu_tanh(y + dskip_ref[...] * ub.astype(F32))
    z = _dot(y.astype(BF16), wglu_ref[...]) + bglu_ref[...]
    out = (y * (1.0 / (1.0 + jnp.exp(-z)))).astype(o_ref.dtype)
    o_ref[:, mix_a:] = _dot(perm_ref[1], out).astype(o_ref.dtype)


def _segment_perm(tt):
    seg_len = tt // SUBLANES
    r = jnp.arange(tt)
    src = (r % SUBLANES) * seg_len + r // SUBLANES
    p = (src[:, None] == jnp.arange(tt)[None, :])
    return jnp.stack([p, p.T]).astype(BF16)


def _pool_band(tt):
    t = jnp.arange(tt)[:, None] + tt
    j = jnp.arange(2 * tt)[None, :]
    return jnp.stack([((j <= t) & (j > t - w)) for w in POOL_WINDOWS]).astype(BF16)


def _even_mixer(proj, wpool, pscale, s5, dskip, wglu, bglu, tt, scan_lanes):
    bsz, seq, width = proj.shape
    wb, wc, a_re, a_im, pk_re, pk_im, sg_re, sg_im = s5
    mix_a = len(POOL_WINDOWS) * POOL_CH
    mix_b = width - mix_a
    n_state = wb.shape[2]
    const = lambda a: pl.BlockSpec(a.shape, lambda b, i, nd=a.ndim: (0,) * nd)
    band = _pool_band(tt)
    args = (band, _segment_perm(tt), wpool, pscale, wb, wc, dskip, wglu, bglu, a_re, a_im, pk_re, pk_im, sg_re, sg_im)
    return pl.pallas_call(
        functools.partial(_even_kernel, mix_a=mix_a, scan_lanes=scan_lanes),
        grid=(bsz, seq // tt),
        in_specs=[pl.BlockSpec((None, tt, width), lambda b, i: (b, i, 0))] + [const(a) for a in args],
        out_specs=pl.BlockSpec((None, tt, width), lambda b, i: (b, i, 0)),
        out_shape=jax.ShapeDtypeStruct((bsz, seq, width), BF16),
        scratch_shapes=[pltpu.VMEM((tt, mix_a), BF16),
                        pltpu.VMEM((n_state // LANES, tt, LANES), F32), pltpu.VMEM((n_state // LANES, tt, LANES), F32),
                        pltpu.VMEM((SUBLANES, n_state), F32), pltpu.VMEM((SUBLANES, n_state), F32)],
        compiler_params=_cparams(("parallel", "arbitrary")),
    )(proj, *args)


def _pair_blockdiag(x, first_ref, second_ref):
    return jnp.concatenate([x * first_ref[...], x * second_ref[...]], axis=0)


def _dsa_kernel(q_ref, qi_ref, wq_ref, k_ref, v_ref, kk_ref, tri_ref, ones_ref, first_ref, second_ref, o_ref,
                s_s, qs_s, mx_s, al_s, *, topk, qb):
    i = pl.program_id(1)
    kb_size = ATT_BLOCK
    n_diag = qb // kb_size
    nkb = (i + 1) * n_diag
    shape = (qb, kb_size)
    row = lax.broadcasted_iota(jnp.int32, shape, 0)
    col = lax.broadcasted_iota(jnp.int32, shape, 1)
    q_chunk = (i * qb + row) // CHUNK
    bcast = lambda a: jnp.broadcast_to(a, shape)
    kf = float(topk)
    pairs = [slice(p * LANES, (p + 1) * LANES) for p in range(N_HEADS // 2)]

    wq = wq_ref[...]
    w_cols = [bcast(wq[:, h:h + 1] * IDX_SCALE) for h in range(IDX_HEADS)]

    def score_block(kb, carry):
        rmax, rmin = carry
        kbd = _pair_blockdiag(kk_ref[kb], first_ref, second_ref)
        acc = jnp.zeros(shape, F32)
        for p, ps in enumerate(pairs):
            rel = jnp.maximum(_dot_nt(qi_ref[:, ps], kbd), 0.0)
            acc = acc + rel[:, :kb_size] * w_cols[2 * p] + rel[:, kb_size:] * w_cols[2 * p + 1]
        adm = ((kb * kb_size + col) // CHUNK) <= q_chunk
        s_s[kb] = jnp.where(adm, acc, -jnp.inf)
        return jnp.maximum(rmax, jnp.where(adm, acc, -jnp.inf)), jnp.minimum(rmin, jnp.where(adm, acc, jnp.inf))

    rmax, rmin = lax.fori_loop(0, nkb, score_block,
                               (jnp.full(shape, -jnp.inf, F32), jnp.full(shape, jnp.inf, F32)))
    n_chunks = qb // kb_size
    chunk_rows = lambda c: slice(c * kb_size, (c + 1) * kb_size)
    chunk_subl = lambda c: slice(c * SUBLANES, (c + 1) * SUBLANES)
    ones8 = jnp.ones((SUBLANES, kb_size), BF16)
    tile = (kb_size, kb_size)

    def to_dense(rep):
        return rep.T[:SUBLANES]

    def to_rows(dense):
        return jnp.broadcast_to(dense[:1], tile).T

    def dense_of(fn):
        return jnp.concatenate([fn(c) for c in range(n_chunks)], axis=0)

    def fold_keys(c, fn, init):
        def body(g, a):
            for u in range(n_diag):
                a = fn(a, s_s[g * n_diag + u, chunk_rows(c), :])
            return a
        return lax.fori_loop(0, i + 1, body, init)

    def count(x, cmp):
        xbs = [to_rows(x[chunk_subl(c)]) for c in range(n_chunks)]
        cnts = [fold_keys(c, lambda a, s, xb=xbs[c]: a + jnp.where(cmp(s, xb), 1.0, 0.0), jnp.zeros(tile, F32))
                for c in range(n_chunks)]
        return jnp.concatenate([_dot_nt(ones8, cnt.astype(BF16)) for cnt in cnts], axis=0)

    count_ge = functools.partial(count, cmp=lambda s, xb: s >= xb)
    count_gt = functools.partial(count, cmp=lambda s, xb: s > xb)
    count_eq = functools.partial(count, cmp=lambda s, xb: s == xb)

    smax = dense_of(lambda c: to_dense(jnp.broadcast_to(_rowmax(rmax[chunk_rows(c)]), tile)))
    smin = dense_of(lambda c: to_dense(jnp.broadcast_to(jnp.min(rmin[chunk_rows(c)], axis=1, keepdims=True), tile)))
    dshape = (n_chunks * SUBLANES, kb_size)
    t_dense = (i * qb + (lax.broadcasted_iota(jnp.int32, dshape, 0) // SUBLANES) * kb_size
               + lax.broadcasted_iota(jnp.int32, dshape, 1))
    n_adm = ((t_dense // CHUNK + 1) * CHUNK).astype(F32)
    hi0 = smax + jnp.maximum(jnp.abs(smax) * 1e-6, 1e-30)
    zero = jnp.zeros_like(smin)
    cge0, cgt0 = count_ge(zero), count_gt(zero)
    wide = n_adm > kf
    zero_tie = jnp.logical_and(wide, jnp.logical_and(cgt0 < kf, cge0 >= kf))
    above = jnp.logical_and(wide, cgt0 >= kf)
    below = jnp.logical_and(wide, cge0 < kf)
    lo0 = jnp.where(jnp.logical_or(above, zero_tie), 0.0, smin)
    clo0 = jnp.where(zero_tie, kf, jnp.where(above, cge0, n_adm))
    hi0 = jnp.where(below, 0.0, hi0)
    chi0 = jnp.where(below, cge0, 0.0)
    need0 = jnp.where(zero_tie, kf - cgt0, COUNT_ALL)

    def unresolved(clo, chi):
        return jnp.max(jnp.where(jnp.logical_and(clo > kf, clo - chi > POP_SPAN), 1.0, 0.0))

    def bisect_cond(carry):
        return jnp.logical_and(carry[0] > 0.0, carry[1] < BISECT_ITERS)

    def bisect(carry):
        _, it, lo, hi, clo, chi = carry
        span = clo - chi
        aim = (clo - (kf + 0.5)) / span
        interpolate = jnp.logical_and(span < INTERP_SPAN, it % 2 == 0)
        mid = lo + jnp.where(interpolate, aim, 0.5) * (hi - lo)
        c = count_ge(mid)
        active = clo > kf
        up = jnp.logical_and(active, c >= kf)
        down = jnp.logical_and(active, c < kf)
        clo = jnp.where(up, c, clo)
        chi = jnp.where(down, c, chi)
        return (unresolved(clo, chi), it + 1, jnp.where(up, mid, lo), jnp.where(down, mid, hi), clo, chi)

    _, _, lo, hi, clo, chi = lax.while_loop(
        bisect_cond, bisect, (unresolved(clo0, chi0), jnp.int32(0), lo0, hi0, clo0, chi0))

    done0 = jnp.where(clo <= kf, 1.0, 0.0)

    def pop_cond(carry):
        return jnp.logical_and(carry[0] > 0.0, carry[1] < MAX_POPS)

    def next_below(hi):
        def one(c):
            hib = to_rows(hi[chunk_subl(c)])
            best = fold_keys(c, lambda a, s: jnp.maximum(a, jnp.where(s < hib, s, -jnp.inf)),
                             jnp.full(tile, -jnp.inf, F32))
            return to_dense(jnp.broadcast_to(_rowmax(best), tile))
        return dense_of(one)

    def pop(carry):
        _, it, hi, chi, thr, need, done = carry
        v = next_below(hi)
        cv = count_eq(v)
        active = done < 0.5
        fin = jnp.logical_and(active, chi + cv >= kf)
        go = jnp.logical_and(active, chi + cv < kf)
        thr = jnp.where(fin, v, thr)
        need = jnp.where(fin, kf - chi, need)
        done = jnp.where(fin, 1.0, done)
        hi = jnp.where(go, v, hi)
        chi = jnp.where(go, chi + cv, chi)
        return (jnp.max(1.0 - done), it + 1, hi, chi, thr, need, done)

    left0 = jnp.max(1.0 - done0)
    _, _, _, _, thr, need, _ = lax.while_loop(
        pop_cond, pop, (left0, jnp.int32(0), hi, chi, lo, need0, done0))
    thrb = jnp.concatenate([to_rows(thr[chunk_subl(c)]) for c in range(n_chunks)], axis=0)
    needb = jnp.concatenate([to_rows(need[chunk_subl(c)]) for c in range(n_chunks)], axis=0)
    partial_ties = jnp.max(jnp.where(need < COUNT_ALL, 1.0, 0.0))

    @pl.when(partial_ties > 0.0)
    def _():
        def tie_block(kb, run):
            s = s_s[kb]
            eq = jnp.where(s == thrb, 1.0, 0.0)
            rank = _dot(eq.astype(BF16), tri_ref[1]) + run
            take = jnp.where(s == thrb, jnp.where(rank < needb, 0.0, NEG_BIG), NEG_BIG)
            s_s[kb] = jnp.where(s > thrb, 0.0, take)
            return run + bcast(_rowsum(eq))
        lax.fori_loop(0, nkb, tie_block, jnp.zeros(shape, F32))

    @pl.when(partial_ties <= 0.0)
    def _():
        def sel_block(kb, carry):
            s_s[kb] = jnp.where(s_s[kb] >= thrb, 0.0, NEG_BIG)
            return carry
        lax.fori_loop(0, nkb, sel_block, 0)

    rel_pos = (row - col).astype(F32)
    first_half = col < HEAD_DIM
    qs_s[...] = q_ref[...] * (HEAD_DIM ** -0.5)
    mx_s[...] = jnp.full(mx_s.shape, NEG_BIG, F32)
    al_s[...] = jnp.zeros_like(al_s)
    slopes = [2.0 ** (-8.0 * (h + 1) / N_HEADS) for h in range(N_HEADS)]

    def attend(g, carry):
        blocks = [g * n_diag + u for u in range(n_diag)]
        dists = [jnp.abs(rel_pos + (i * qb - kb * kb_size).astype(F32)) for kb in blocks]
        for p, ps in enumerate(pairs):
            tiles = []
            for kb, dist in zip(blocks, dists):
                mask = s_s[kb]
                kbd = _pair_blockdiag(k_ref[kb, :, ps], first_ref, second_ref)
                bias = jnp.concatenate([mask - slopes[2 * p] * dist, mask - slopes[2 * p + 1] * dist], axis=1)
                tiles.append(_dot_nt(qs_s[:, ps], kbd) + bias)
            top = functools.reduce(jnp.maximum, tiles)
            m_old = mx_s[p]
            m_new = jnp.maximum(m_old, jnp.concatenate(
                [bcast(_rowmax(top[:, :kb_size])), bcast(_rowmax(top[:, kb_size:]))], axis=1))
            mx_s[p] = m_new
            pexp = jnp.concatenate([jnp.exp((t - m_new).astype(BF16)) for t in tiles], axis=1)
            shrink = m_old - m_new
            alpha = jnp.exp(jnp.where(first_half, shrink[:, :kb_size], shrink[:, kb_size:]))
            values = jnp.concatenate(
                [jnp.concatenate([_pair_blockdiag(v_ref[kb, :, ps], first_ref, second_ref), ones_ref[...]], axis=1)
                 for kb in blocks], axis=0)
            al_s[p] = jnp.concatenate([alpha, alpha], axis=1) * al_s[p] + _dot(pexp, values)
        return carry

    lax.fori_loop(0, i + 1, attend, 0)
    for p, ps in enumerate(pairs):
        state = al_s[p]
        o_ref[:, ps] = (state[:, :LANES] / state[:, LANES:]).astype(o_ref.dtype)


def _tri_consts(n):
    j = jnp.arange(n)[:, None]
    s = jnp.arange(n)[None, :]
    return jnp.stack([j > s, j < s]).astype(BF16)


def _pair_consts(n):
    tri = _tri_consts(n)
    zero = jnp.zeros((n, n), BF16)
    tri2 = jnp.block([[tri[0], zero], [zero, tri[0]]])
    first = (jnp.arange(n)[None, :] < n // 2) & jnp.ones((n, 1), bool)
    ones2 = jnp.concatenate([first, ~first], axis=0).astype(BF16)
    return tri2, ones2, first.astype(BF16), (~first).astype(BF16)


def _dsa(q, k, v, qi, kk, wq, topk, qb):
    bsz, seq, width = q.shape
    nq, nk = seq // qb, seq // ATT_BLOCK
    qblk = lambda a: a.reshape(bsz, nq, qb, a.shape[-1])
    kblk = lambda a: a.reshape(bsz, nk, ATT_BLOCK, a.shape[-1])
    qspec = lambda wd: pl.BlockSpec((None, None, qb, wd), lambda b, i: (b, i, 0, 0))
    kspec = lambda wd: pl.BlockSpec((None, nk, ATT_BLOCK, wd), lambda b, i: (b, 0, 0, 0))
    _, ones2, first, second = _pair_consts(ATT_BLOCK)
    consts = (_tri_consts(ATT_BLOCK), ones2, first, second)
    out = pl.pallas_call(
        functools.partial(_dsa_kernel, topk=topk, qb=qb),
        grid=(bsz, nq),
        in_specs=[qspec(width), qspec(width), qspec(LANES), kspec(width), kspec(width), kspec(LANES)]
                 + [pl.BlockSpec(a.shape, lambda b, i, nd=a.ndim: (0,) * nd) for a in consts],
        out_specs=qspec(width),
        out_shape=jax.ShapeDtypeStruct((bsz, nq, qb, width), BF16),
        scratch_shapes=[pltpu.VMEM((nk, qb, ATT_BLOCK), F32),
                        pltpu.VMEM((qb, width), BF16),
                        pltpu.VMEM((N_HEADS // 2, qb, 2 * ATT_BLOCK), F32),
                        pltpu.VMEM((N_HEADS // 2, qb, 2 * LANES), F32)],
        compiler_params=_cparams(("parallel", "arbitrary")),
    )(qblk(q), qblk(qi), qblk(wq), kblk(k), kblk(v), kblk(kk), *consts)
    return out.reshape(bsz, seq, width)


def _sb_kernel(q_ref, k_ref, v_ref, tri_ref, ones_ref, first_ref, second_ref, o_ref, qs_s, run_s, acc_s, *, qb):
    i = pl.program_id(1)
    kb_size = ATT_BLOCK
    n_diag = qb // kb_size
    n_pairs = N_HEADS // 2
    shape = (qb, 2 * kb_size)
    row = lax.broadcasted_iota(jnp.int32, shape, 0)
    col = lax.broadcasted_iota(jnp.int32, shape, 1) % kb_size
    qs_s[...] = q_ref[...] * (HEAD_DIM ** -0.5)
    run_s[...] = jnp.zeros_like(run_s)
    acc_s[...] = jnp.zeros_like(acc_s)

    def step(kb, masked):
        causal = (kb * kb_size + col) < (i * qb + row)
        pairs = [slice(p * LANES, (p + 1) * LANES) for p in range(n_pairs)]
        log_sig, log_keep = [], []
        for ps in pairs:
            kbd = _pair_blockdiag(k_ref[kb, :, ps], first_ref, second_ref)
            z = _dot_nt(qs_s[:, ps], kbd).astype(BF16)
            soft = jnp.log(1.0 + jnp.exp(-jnp.abs(z)))
            keep = -jnp.maximum(z, 0.0) - soft
            log_sig.append(jnp.minimum(z, 0.0) - soft)
            log_keep.append(jnp.where(causal, keep, jnp.zeros_like(keep)) if masked else keep)
        for p, ps in enumerate(pairs):
            after = _dot(log_keep[p], tri_ref[...]).astype(BF16)
            w = jnp.exp(log_sig[p] + after)
            if masked:
                w = jnp.where(causal, w, jnp.zeros_like(w))
            vbd = _pair_blockdiag(v_ref[kb, :, ps], first_ref, second_ref)
            acc_s[p] += jnp.exp(run_s[p]) * _dot(w, vbd)
            run_s[p] += _dot(log_keep[p], ones_ref[...])

    last = (i + 1) * n_diag - 1

    def diag_step(j, carry):
        step(last - j, True)
        return carry

    def past_step(j, carry):
        step(i * n_diag - 1 - j, False)
        return carry

    lax.fori_loop(0, n_diag, diag_step, 0)
    lax.fori_loop(0, i * n_diag, past_step, 0)
    for p in range(n_pairs):
        o_ref[:, p * LANES:(p + 1) * LANES] = acc_s[p].astype(o_ref.dtype)


def _stick_breaking(q, k, v, qb):
    bsz, seq, width = q.shape
    nq, nk = seq // qb, seq // ATT_BLOCK
    qspec = pl.BlockSpec((None, None, qb, width), lambda b, i: (b, i, 0, 0))
    kspec = pl.BlockSpec((None, nk, ATT_BLOCK, width), lambda b, i: (b, 0, 0, 0))
    consts = _pair_consts(ATT_BLOCK)
    out = pl.pallas_call(
        functools.partial(_sb_kernel, qb=qb),
        grid=(bsz, nq),
        in_specs=[qspec, kspec, kspec] + [pl.BlockSpec(a.shape, lambda b, i: (0, 0)) for a in consts],
        out_specs=qspec,
        out_shape=jax.ShapeDtypeStruct((bsz, nq, qb, width), BF16),
        scratch_shapes=[pltpu.VMEM((qb, width), BF16),
                        pltpu.VMEM((N_HEADS // 2, qb, LANES), F32),
                        pltpu.VMEM((N_HEADS // 2, qb, LANES), F32)],
        compiler_params=_cparams(("parallel", "arbitrary")),
    )(q.reshape(bsz, nq, qb, width), k.reshape(bsz, nk, ATT_BLOCK, width),
      v.reshape(bsz, nk, ATT_BLOCK, width), *consts)
    return out.reshape(bsz, seq, width)


def _cd_weight(cd_w_in):
    cw = N_HEADS * HEAD_DIM
    iw = IDX_HEADS * IDX_DIM
    offs = [0, cw, 2 * cw, 3 * cw, 3 * cw + iw, 3 * cw + iw + IDX_DIM, 3 * cw + iw + IDX_DIM + IDX_HEADS]
    qkv_c = cd_w_in[:, :offs[3]]
    q_idx = cd_w_in[:, offs[3]:offs[4]]
    k_idx = cd_w_in[:, offs[4]:offs[5]]
    w_idx = cd_w_in[:, offs[5]:offs[6]]
    qkv_d = cd_w_in[:, offs[6]:]
    pad = jnp.zeros((cd_w_in.shape[0], LANES - w_idx.shape[1]), cd_w_in.dtype)
    return jnp.concatenate([qkv_c, q_idx, qkv_d, k_idx, k_idx, w_idx, pad], axis=1)


def kernel(x, c, norm_g, ada_w, ada_b, mlp_w1, mlp_w2, ab_w_in, ab_w_out, pool_w, pool_scale, ssm_lam_re, ssm_lam_im, ssm_log_dt, ssm_b_re, ssm_b_im, ssm_c_re, ssm_c_im, ssm_d, glu_w, glu_b, cd_w_in, cd_w_out, final_g):
    bsz, seq, d = x.shape
    depth = norm_g.shape[0]
    topk = min(TOPK_MAX, seq // 4)
    tm = min(512, seq)

    c_pad = jnp.zeros((SUBLANES, d), F32).at[:bsz].set(c)
    mod = _modulation(c_pad, ada_w.reshape(depth * 2, d, 3 * d), ada_b.reshape(depth * 2, 1, 3 * d))
    mod = mod[:, :, :bsz, None, :]

    for i in range(depth):
        j = i // 2
        shift, scale1, gate1 = mod[2 * i, 0], mod[2 * i, 1], mod[2 * i, 2]
        g = norm_g[i, 0][None, :]
        if i % 2 == 0:
            width = ab_w_in.shape[2]
            tt = min(256, seq)
            (proj,) = _norm_proj(x, g, shift, scale1, ab_w_in[j].astype(BF16), (width,), (BF16,), tm)
            s5 = _s5_params(ssm_lam_re[j], ssm_lam_im[j], ssm_log_dt[j],
                            ssm_b_re[j], ssm_b_im[j], ssm_c_re[j], ssm_c_im[j], seg_len=tt // SUBLANES)
            y = _even_mixer(proj, pool_w[j].astype(BF16), pool_scale[j][None, :], s5,
                            ssm_d[j][None, :], glu_w[j].astype(BF16), glu_b[j][None, :],
                            tt=tt, scan_lanes=512)
            x = _out_proj(x, gate1, (y,), ab_w_out[j].astype(BF16), tm)
        else:
            hw = N_HEADS * HEAD_DIM
            widths = (hw,) * 7 + (LANES, LANES)
            dtypes = (BF16,) * 8 + (F32,)
            qc, kc, vc, qi, qd, kd, vd, kk, wq = _norm_proj(
                x, g, shift, scale1, _cd_weight(cd_w_in[j]).astype(BF16), widths, dtypes, tm)
            oc = _dsa(qc, kc, vc, qi, kk, wq, topk, qb=ATT_QUERY_BLOCK)
            od = _stick_breaking(qd, kd, vd, qb=ATT_QUERY_BLOCK)
            x = _out_proj(x, gate1, (oc, od), cd_w_out[j].astype(BF16), tm)
        shift, scale1, gate1 = mod[2 * i + 1, 0], mod[2 * i + 1, 1], mod[2 * i + 1, 2]
        x = _mlp(x, norm_g[i, 1][None, :], shift, scale1, gate1,
                 mlp_w1[i].astype(BF16), mlp_w2[i].astype(BF16), final_g[None, :],
                 final_norm=(i == depth - 1), tm=min(1024, seq), tf=1024)
    return x
```

```python
import functools
import math

import jax
import jax.numpy as jnp
from jax import lax
from jax.experimental import pallas as pl
from jax.experimental.pallas import tpu as pltpu

F32 = jnp.float32
BF16 = jnp.bfloat16

EPS = 1e-6
CHUNK = 64
POOL_WINDOWS = (2, 4, 8, 16)
POOL_CH = 128
SSM_GROUP = 16
SSM_STATE = 64
HEAD_DIM = 64
N_HEADS = 8
IDX_HEADS = 8
IDX_DIM = 64
IDX_SCALE = (IDX_HEADS ** -0.5) * (IDX_DIM ** -0.5)
TOPK_MAX = 256

LANES = 128
SUBLANES = 8
VMEM_LIMIT = 56 * 1024 * 1024

ATT_BLOCK = 128
ATT_QUERY_BLOCK = 512
NEG_BIG = -1e30
COUNT_ALL = 1e9
BISECT_ITERS = 28
COUNT_ROWS = 128
POP_SPAN = 2.0
INTERP_SPAN = 256.0
MAX_POPS = 4096


def _cparams(sem):
    return pltpu.CompilerParams(dimension_semantics=sem, vmem_limit_bytes=VMEM_LIMIT)


def _dot(a, b):
    return jnp.dot(a, b, preferred_element_type=F32)


def _dot_nt(a, b):
    return lax.dot_general(a, b, (((1,), (1,)), ((), ())), preferred_element_type=F32)


def _rowsum(x):
    return jnp.sum(x, axis=1, keepdims=True)


def _rowmax(x):
    return jnp.max(x, axis=1, keepdims=True)


def _norm_modulate(x, g, shift, scale1):
    ms = jnp.mean(x * x, axis=-1, keepdims=True)
    return (x * lax.rsqrt(ms + EPS)) * g * scale1 + shift


def _mod_kernel(c_ref, w_ref, b_ref, o_ref):
    j = pl.program_id(1)
    m = _dot(c_ref[...].astype(BF16), w_ref[...].astype(BF16)) + b_ref[...]
    o_ref[...] = m + jnp.where(j > 0, 1.0, 0.0)


def _modulation(c_pad, ada_w, ada_b):
    n_sub, d, _ = ada_w.shape
    rows = c_pad.shape[0]
    return pl.pallas_call(
        _mod_kernel,
        grid=(n_sub, 3),
        in_specs=[
            pl.BlockSpec((rows, d), lambda s, j: (0, 0)),
            pl.BlockSpec((None, d, d), lambda s, j: (s, 0, j)),
            pl.BlockSpec((None, 1, d), lambda s, j: (s, 0, j)),
        ],
        out_specs=pl.BlockSpec((None, None, rows, d), lambda s, j: (s, j, 0, 0)),
        out_shape=jax.ShapeDtypeStruct((n_sub, 3, rows, d), F32),
        compiler_params=_cparams(("arbitrary", "arbitrary")),
    )(c_pad, ada_w, ada_b)


def _norm_proj_kernel(x_ref, g_ref, shift_ref, scale_ref, w_ref, *o_refs, widths):
    h = _norm_modulate(x_ref[...], g_ref[...], shift_ref[...], scale_ref[...]).astype(BF16)
    y = _dot(h, w_ref[...])
    off = 0
    for o_ref, width in zip(o_refs, widths):
        o_ref[...] = y[:, off:off + width].astype(o_ref.dtype)
        off += width


def _norm_proj(x, g, shift, scale1, w, widths, dtypes, tm):
    bsz, seq, d = x.shape
    n = w.shape[1]
    vec = pl.BlockSpec((None, 1, d), lambda b, i: (b, 0, 0))
    return pl.pallas_call(
        functools.partial(_norm_proj_kernel, widths=widths),
        grid=(bsz, seq // tm),
        in_specs=[
            pl.BlockSpec((None, tm, d), lambda b, i: (b, i, 0)),
            pl.BlockSpec((1, d), lambda b, i: (0, 0)),
            vec, vec,
            pl.BlockSpec((d, n), lambda b, i: (0, 0)),
        ],
        out_specs=[pl.BlockSpec((None, tm, wd), lambda b, i: (b, i, 0)) for wd in widths],
        out_shape=[jax.ShapeDtypeStruct((bsz, seq, wd), dt) for wd, dt in zip(widths, dtypes)],
        compiler_params=_cparams(("parallel", "parallel")),
    )(x, g, shift, scale1, w)


def _out_proj_kernel(x_ref, gate_ref, *refs, n_y):
    y_refs, w_ref, o_ref = refs[:n_y], refs[n_y], refs[n_y + 1]
    y = jnp.concatenate([r[...] for r in y_refs], axis=1) if n_y > 1 else y_refs[0][...]
    o_ref[...] = x_ref[...] + gate_ref[...] * _dot(y, w_ref[...])


def _out_proj(x, gate1, ys, w, tm):
    bsz, seq, d = x.shape
    k = w.shape[0]
    return pl.pallas_call(
        functools.partial(_out_proj_kernel, n_y=len(ys)),
        grid=(bsz, seq // tm),
        in_specs=[
            pl.BlockSpec((None, tm, d), lambda b, i: (b, i, 0)),
            pl.BlockSpec((None, 1, d), lambda b, i: (b, 0, 0)),
            *[pl.BlockSpec((None, tm, y.shape[2]), lambda b, i: (b, i, 0)) for y in ys],
            pl.BlockSpec((k, d), lambda b, i: (0, 0)),
        ],
        out_specs=pl.BlockSpec((None, tm, d), lambda b, i: (b, i, 0)),
        out_shape=jax.ShapeDtypeStruct((bsz, seq, d), F32),
        compiler_params=_cparams(("parallel", "parallel")),
    )(x, gate1, *ys, w)


def _mlp_kernel(x_ref, g_ref, shift_ref, scale_ref, gate_ref, w1_ref, w2_ref, fg_ref, o_ref,
                h_s, acc_s, *, final_norm):
    f = pl.program_id(2)

    @pl.when(f == 0)
    def _():
        h_s[...] = _norm_modulate(x_ref[...], g_ref[...], shift_ref[...], scale_ref[...]).astype(BF16)
        acc_s[...] = jnp.zeros_like(acc_s)

    a = jnp.maximum(_dot(h_s[...], w1_ref[...]), 0.0)
    acc_s[...] += _dot((a * a).astype(BF16), w2_ref[...])

    @pl.when(f == pl.num_programs(2) - 1)
    def _():
        y = x_ref[...] + gate_ref[...] * acc_s[...]
        if final_norm:
            ms = jnp.mean(y * y, axis=-1, keepdims=True)
            y = (y * lax.rsqrt(ms + EPS)) * fg_ref[...]
        o_ref[...] = y


def _mlp(x, g, shift, scale1, gate1, w1, w2, final_g, final_norm, tm, tf):
    bsz, seq, d = x.shape
    dff = w1.shape[1]
    vec = pl.BlockSpec((None, 1, d), lambda b, i, f: (b, 0, 0))
    one = pl.BlockSpec((1, d), lambda b, i, f: (0, 0))
    return pl.pallas_call(
        functools.partial(_mlp_kernel, final_norm=final_norm),
        grid=(bsz, seq // tm, dff // tf),
        in_specs=[
            pl.BlockSpec((None, tm, d), lambda b, i, f: (b, i, 0)),
            one, vec, vec, vec,
            pl.BlockSpec((d, tf), lambda b, i, f: (0, f)),
            pl.BlockSpec((tf, d), lambda b, i, f: (f, 0)),
            one,
        ],
        out_specs=pl.BlockSpec((None, tm, d), lambda b, i, f: (b, i, 0)),
        out_shape=jax.ShapeDtypeStruct((bsz, seq, d), F32),
        scratch_shapes=[pltpu.VMEM((tm, d), BF16), pltpu.VMEM((tm, d), F32)],
        compiler_params=_cparams(("parallel", "parallel", "arbitrary")),
    )(x, g, shift, scale1, gate1, w1, w2, final_g)


def _cmul(ar, ai, br, bi):
    return ar * br - ai * bi, ar * bi + ai * br


def _s5_params_kernel(lr_ref, li_ref, ldt_ref, brt_ref, bit_ref, crt_ref, cit_ref,
                      wb_ref, wc_ref, a_re_ref, a_im_ref, pk_re_ref, pk_im_ref, sg_re_ref, sg_im_ref):
    lr, li = lr_ref[...], li_ref[...]
    dt = jnp.exp(ldt_ref[...])
    mag = jnp.exp(lr * dt)
    ar, ai = mag * jnp.cos(li * dt), mag * jnp.sin(li * dt)
    den = lr * lr + li * li
    zr = ((ar - 1.0) * lr + ai * li) / den
    zi = (ai * lr - (ar - 1.0) * li) / den

    rows, cols = brt_ref.shape
    rgrp = lax.broadcasted_iota(jnp.int32, (rows, cols), 0) // SSM_GROUP
    cgrp = lax.broadcasted_iota(jnp.int32, (rows, cols), 1) // SSM_STATE
    diag = rgrp == cgrp
    br, bi = brt_ref[...], bit_ref[...]
    wb_ref[0] = jnp.where(diag, zr * br - zi * bi, 0.0).astype(wb_ref.dtype)
    wb_ref[1] = jnp.where(diag, zr * bi + zi * br, 0.0).astype(wb_ref.dtype)
    wc_ref[0] = jnp.where(diag, crt_ref[...], 0.0).astype(wc_ref.dtype)
    wc_ref[1] = jnp.where(diag, cit_ref[...], 0.0).astype(wc_ref.dtype)

    n = lr.shape[1]
    rows8 = pk_re_ref.shape[0]
    seg_len = rows8 // SUBLANES
    a_re_ref[...] = jnp.broadcast_to(ar, (SUBLANES, n))
    a_im_ref[...] = jnp.broadcast_to(ai, (SUBLANES, n))
    expo = lax.broadcasted_iota(jnp.int32, (rows8, n), 0) // SUBLANES + 1
    sq_r, sq_i = jnp.broadcast_to(ar, (rows8, n)), jnp.broadcast_to(ai, (rows8, n))
    pr, pi = jnp.ones((rows8, n), F32), jnp.zeros((rows8, n), F32)
    for b in range(seg_len.bit_length()):
        nr, ni = _cmul(pr, pi, sq_r, sq_i)
        take = ((expo >> b) & 1) == 1
        pr, pi = jnp.where(take, nr, pr), jnp.where(take, ni, pi)
        sq_r, sq_i = _cmul(sq_r, sq_i, sq_r, sq_i)
    pk_re_ref[...] = pr
    pk_im_ref[...] = pi
    row = lax.broadcasted_iota(jnp.int32, (SUBLANES, n), 0)
    gr, gi = pr[rows8 - SUBLANES:], pi[rows8 - SUBLANES:]
    for k in range(3):
        keep = row >= (1 << k)
        sg_re_ref[k] = jnp.where(keep, gr, 0.0)
        sg_im_ref[k] = jnp.where(keep, gi, 0.0)
        gr, gi = _cmul(gr, gi, gr, gi)


def _s5_params(lam_re, lam_im, log_dt, b_re, b_im, c_re, c_im, seg_len):
    g, p = lam_re.shape
    c = b_re.shape[2]
    n = g * p
    flat = lambda a: a.reshape(1, n)
    ldt = jnp.broadcast_to(log_dt[:, None], (g, p)).reshape(1, n)
    b_t = lambda a: jnp.tile(jnp.transpose(a, (2, 0, 1)).reshape(c, n), (g, 1))
    c_t = lambda a: jnp.tile(a.reshape(g * c, p), (1, g))
    outs = pl.pallas_call(
        _s5_params_kernel,
        out_shape=[jax.ShapeDtypeStruct((2, g * c, n), BF16), jax.ShapeDtypeStruct((2, g * c, n), BF16),
                   jax.ShapeDtypeStruct((SUBLANES, n), F32), jax.ShapeDtypeStruct((SUBLANES, n), F32),
                   jax.ShapeDtypeStruct((seg_len * SUBLANES, n), F32), jax.ShapeDtypeStruct((seg_len * SUBLANES, n), F32),
                   jax.ShapeDtypeStruct((3, SUBLANES, n), F32), jax.ShapeDtypeStruct((3, SUBLANES, n), F32)],
        compiler_params=pltpu.CompilerParams(vmem_limit_bytes=VMEM_LIMIT),
    )(flat(lam_re), flat(lam_im), ldt, b_t(b_re), b_t(b_im), c_t(c_re), c_t(c_im))
    wb, wct, a_re, a_im, pk_re, pk_im, sg_re, sg_im = outs
    wc = jnp.transpose(wct, (0, 2, 1))
    pk_re, pk_im = (a.reshape(seg_len, SUBLANES, n) for a in (pk_re, pk_im))
    return wb, wc, a_re, a_im, pk_re, pk_im, sg_re, sg_im


def _gelu_tanh(x):
    return 0.5 * x * (1.0 + jnp.tanh(math.sqrt(2.0 / math.pi) * (x + 0.044715 * (x * x * x))))


def _even_kernel(p_ref, band_ref, perm_ref, wpool_ref, pscale_ref, wb_ref, wc_ref, dskip_ref, wglu_ref, bglu_ref,
                 a_re_ref, a_im_ref, pk_re_ref, pk_im_ref, sg_re_ref, sg_im_ref,
                 o_ref, prev_s, xr_s, xi_s, car_re, car_im, *, mix_a, scan_lanes):
    i = pl.program_id(1)
    tt = p_ref.shape[0]

    @pl.when(i == 0)
    def _():
        prev_s[...] = jnp.zeros_like(prev_s)
        car_re[...] = jnp.zeros_like(car_re)
        car_im[...] = jnp.zeros_like(car_im)

    ua = p_ref[:, :mix_a]
    ext = jnp.concatenate([prev_s[...], ua], axis=0)
    t1 = i * tt + lax.broadcasted_iota(jnp.int32, (tt, POOL_CH), 0) + 1
    for g, win in enumerate(POOL_WINDOWS):
        sl = slice(g * POOL_CH, (g + 1) * POOL_CH)
        wsum = _dot(band_ref[g], ext[:, sl])
        cnt = jnp.minimum(t1, win).astype(F32)
        pg = wsum / cnt - ua[:, sl].astype(F32)
        yg = _dot(pg.astype(BF16), wpool_ref[g]) * pscale_ref[:, sl]
        o_ref[:, sl] = yg.astype(o_ref.dtype)
    prev_s[...] = ua

    ub = _dot(perm_ref[0], p_ref[:, mix_a:]).astype(BF16)
    n_tiles = xr_s.shape[0]
    n_diag_blocks = ub.shape[1] // LANES
    tpb = n_tiles // n_diag_blocks
    sw = tpb * LANES
    for j in range(n_diag_blocks):
        cs, ss = slice(j * LANES, (j + 1) * LANES), slice(j * sw, (j + 1) * sw)
        xr, xi = _dot(ub[:, cs], wb_ref[0, cs, ss]), _dot(ub[:, cs], wb_ref[1, cs, ss])
        for q in range(tpb):
            xr_s[j * tpb + q] = xr[:, q * LANES:(q + 1) * LANES]
            xi_s[j * tpb + q] = xi[:, q * LANES:(q + 1) * LANES]

    seg_len = tt // SUBLANES
    seg_rows = lambda k: pl.ds(pl.multiple_of(k * SUBLANES, SUBLANES), SUBLANES)
    first_seg = lax.broadcasted_iota(jnp.int32, (SUBLANES, LANES), 0) == 0
    zeros = jnp.zeros((SUBLANES, LANES), F32)
    tpc = scan_lanes // LANES
    for c in range(n_tiles // tpc):
        tiles = list(range(c * tpc, (c + 1) * tpc))
        lanes = [slice(t * LANES, (t + 1) * LANES) for t in tiles]

        def local(k, carry, tiles=tiles, lanes=lanes):
            out = []
            for t, ls, (hr, hi) in zip(tiles, lanes, carry):
                dr, di = _cmul(a_re_ref[:, ls], a_im_ref[:, ls], hr, hi)
                hr, hi = dr + xr_s[t, seg_rows(k), :], di + xi_s[t, seg_rows(k), :]
                xr_s[t, seg_rows(k), :] = hr
                xi_s[t, seg_rows(k), :] = hi
                out.append((hr, hi))
            return tuple(out)

        ends = lax.fori_loop(0, seg_len, local, tuple((zeros, zeros) for _ in tiles))
        starts = []
        for t, ls, (er, ei) in zip(tiles, lanes, ends):
            fr = jnp.where(first_seg, car_re[:, ls], pltpu.roll(er, 1, 0))
            fi = jnp.where(first_seg, car_im[:, ls], pltpu.roll(ei, 1, 0))
            for k in range(3):
                dr, di = _cmul(sg_re_ref[k, :, ls], sg_im_ref[k, :, ls],
                               pltpu.roll(fr, 1 << k, 0), pltpu.roll(fi, 1 << k, 0))
                fr, fi = fr + dr, fi + di
            starts.append((fr, fi))
            dr, di = _cmul(pk_re_ref[seg_len - 1, :, ls], pk_im_ref[seg_len - 1, :, ls], fr, fi)
            car_re[:, ls] = jnp.broadcast_to((dr + er)[SUBLANES - 1:], fr.shape)
            car_im[:, ls] = jnp.broadcast_to((di + ei)[SUBLANES - 1:], fr.shape)

        def fix(k, carry, tiles=tiles, lanes=lanes, starts=starts):
            for t, ls, (fr, fi) in zip(tiles, lanes, starts):
                dr, di = _cmul(pk_re_ref[k, :, ls], pk_im_ref[k, :, ls], fr, fi)
                xr_s[t, seg_rows(k), :] += dr
                xi_s[t, seg_rows(k), :] += di
            return carry

        lax.fori_loop(0, seg_len, fix, 0)

    state = lambda ref, j: jnp.concatenate([ref[j * tpb + q] for q in range(tpb)], axis=1).astype(BF16)
    y = jnp.concatenate(
        [_dot(state(xr_s, j), wc_ref[0, j * sw:(j + 1) * sw, j * LANES:(j + 1) * LANES])
         - _dot(state(xi_s, j), wc_ref[1, j * sw:(j + 1) * sw, j * LANES:(j + 1) * LANES])
         for j in range(n_diag_blocks)], axis=1)
    y = _gelu_tanh(y + dskip_ref[...] * ub.astype(F32))
    z = _dot(y.astype(BF16), wglu_ref[...]) + bglu_ref[...]
    out = (y * (1.0 / (1.0 + jnp.exp(-z)))).astype(o_ref.dtype)
    o_ref[:, mix_a:] = _dot(perm_ref[1], out).astype(o_ref.dtype)


def _segment_perm(tt):
    seg_len = tt // SUBLANES
    r = jnp.arange(tt)
    src = (r % SUBLANES) * seg_len + r // SUBLANES
    p = (src[:, None] == jnp.arange(tt)[None, :])
    return jnp.stack([p, p.T]).astype(BF16)


def _pool_band(tt):
    t = jnp.arange(tt)[:, None] + tt
    j = jnp.arange(2 * tt)[None, :]
    return jnp.stack([((j <= t) & (j > t - w)) for w in POOL_WINDOWS]).astype(BF16)


def _even_mixer(proj, wpool, pscale, s5, dskip, wglu, bglu, tt, scan_lanes):
    bsz, seq, width = proj.shape
    wb, wc, a_re, a_im, pk_re, pk_im, sg_re, sg_im = s5
    mix_a = len(POOL_WINDOWS) * POOL_CH
    mix_b = width - mix_a
    n_state = wb.shape[2]
    const = lambda a: pl.BlockSpec(a.shape, lambda b, i, nd=a.ndim: (0,) * nd)
    band = _pool_band(tt)
    args = (band, _segment_perm(tt), wpool, pscale, wb, wc, dskip, wglu, bglu, a_re, a_im, pk_re, pk_im, sg_re, sg_im)
    return pl.pallas_call(
        functools.partial(_even_kernel, mix_a=mix_a, scan_lanes=scan_lanes),
        grid=(bsz, seq // tt),
        in_specs=[pl.BlockSpec((None, tt, width), lambda b, i: (b, i, 0))] + [const(a) for a in args],
        out_specs=pl.BlockSpec((None, tt, width), lambda b, i: (b, i, 0)),
        out_shape=jax.ShapeDtypeStruct((bsz, seq, width), BF16),
        scratch_shapes=[pltpu.VMEM((tt, mix_a), BF16),
                        pltpu.VMEM((n_state // LANES, tt, LANES), F32), pltpu.VMEM((n_state // LANES, tt, LANES), F32),
                        pltpu.VMEM((SUBLANES, n_state), F32), pltpu.VMEM((SUBLANES, n_state), F32)],
        compiler_params=_cparams(("parallel", "arbitrary")),
    )(proj, *args)


def _pair_blockdiag(x, first_ref, second_ref):
    return jnp.concatenate([x * first_ref[...], x * second_ref[...]], axis=0)


def _dsa_kernel(q_ref, qi_ref, wq_ref, k_ref, v_ref, kk_ref, tri_ref, ones_ref, first_ref, second_ref, o_ref,
                s_s, qs_s, mx_s, al_s, *, topk, qb):
    i = pl.program_id(1)
    kb_size = ATT_BLOCK
    n_diag = qb // kb_size
    nkb = (i + 1) * n_diag
    shape = (qb, kb_size)
    row = lax.broadcasted_iota(jnp.int32, shape, 0)
    col = lax.broadcasted_iota(jnp.int32, shape, 1)
    q_chunk = (i * qb + row) // CHUNK
    bcast = lambda a: jnp.broadcast_to(a, shape)
    kf = float(topk)
    pairs = [slice(p * LANES, (p + 1) * LANES) for p in range(N_HEADS // 2)]

    wq = wq_ref[...]
    w_cols = [bcast(wq[:, h:h + 1] * IDX_SCALE) for h in range(IDX_HEADS)]

    def score_block(kb, carry):
        rmax, rmin = carry
        kbd = _pair_blockdiag(kk_ref[kb], first_ref, second_ref)
        acc = jnp.zeros(shape, F32)
        for p, ps in enumerate(pairs):
            rel = jnp.maximum(_dot_nt(qi_ref[:, ps], kbd), 0.0)
            acc = acc + rel[:, :kb_size] * w_cols[2 * p] + rel[:, kb_size:] * w_cols[2 * p + 1]
        adm = ((kb * kb_size + col) // CHUNK) <= q_chunk
        s_s[kb] = jnp.where(adm, acc, -jnp.inf)
        return jnp.maximum(rmax, jnp.where(adm, acc, -jnp.inf)), jnp.minimum(rmin, jnp.where(adm, acc, jnp.inf))

    rmax, rmin = lax.fori_loop(0, nkb, score_block,
                               (jnp.full(shape, -jnp.inf, F32), jnp.full(shape, jnp.inf, F32)))
    n_chunks = qb // kb_size
    chunk_rows = lambda c: slice(c * kb_size, (c + 1) * kb_size)
    chunk_subl = lambda c: slice(c * SUBLANES, (c + 1) * SUBLANES)
    ones8 = jnp.ones((SUBLANES, kb_size), BF16)
    tile = (kb_size, kb_size)

    def to_dense(rep):
        return rep.T[:SUBLANES]

    def to_rows(dense):
        return jnp.broadcast_to(dense[:1], tile).T

    def dense_of(fn):
        return jnp.concatenate([fn(c) for c in range(n_chunks)], axis=0)

    def fold_keys(c, fn, init):
        def body(g, a):
            for u in range(n_diag):
                a = fn(a, s_s[g * n_diag + u, chunk_rows(c), :])
            return a
        return lax.fori_loop(0, i + 1, body, init)

    def count(x, cmp):
        xbs = [to_rows(x[chunk_subl(c)]) for c in range(n_chunks)]
        cnts = [fold_keys(c, lambda a, s, xb=xbs[c]: a + jnp.where(cmp(s, xb), 1.0, 0.0), jnp.zeros(tile, F32))
                for c in range(n_chunks)]
        return jnp.concatenate([_dot_nt(ones8, cnt.astype(BF16)) for cnt in cnts], axis=0)

    count_ge = functools.partial(count, cmp=lambda s, xb: s >= xb)
    count_gt = functools.partial(count, cmp=lambda s, xb: s > xb)
    count_eq = functools.partial(count, cmp=lambda s, xb: s == xb)

    smax = dense_of(lambda c: to_dense(jnp.broadcast_to(_rowmax(rmax[chunk_rows(c)]), tile)))
    smin = dense_of(lambda c: to_dense(jnp.broadcast_to(jnp.min(rmin[chunk_rows(c)], axis=1, keepdims=True), tile)))
    dshape = (n_chunks * SUBLANES, kb_size)
    t_dense = (i * qb + (lax.broadcasted_iota(jnp.int32, dshape, 0) // SUBLANES) * kb_size
               + lax.broadcasted_iota(jnp.int32, dshape, 1))
    n_adm = ((t_dense // CHUNK + 1) * CHUNK).astype(F32)
    hi0 = smax + jnp.maximum(jnp.abs(smax) * 1e-6, 1e-30)
    zero = jnp.zeros_like(smin)
    cge0, cgt0 = count_ge(zero), count_gt(zero)
    wide = n_adm > kf
    zero_tie = jnp.logical_and(wide, jnp.logical_and(cgt0 < kf, cge0 >= kf))
    above = jnp.logical_and(wide, cgt0 >= kf)
    below = jnp.logical_and(wide, cge0 < kf)
    lo0 = jnp.where(jnp.logical_or(above, zero_tie), 0.0, smin)
    clo0 = jnp.where(zero_tie, kf, jnp.where(above, cge0, n_adm))
    hi0 = jnp.where(below, 0.0, hi0)
    chi0 = jnp.where(below, cge0, 0.0)
    need0 = jnp.where(zero_tie, kf - cgt0, COUNT_ALL)

    def unresolved(clo, chi):
        return jnp.max(jnp.where(jnp.logical_and(clo > kf, clo - chi > POP_SPAN), 1.0, 0.0))

    def bisect_cond(carry):
        return jnp.logical_and(carry[0] > 0.0, carry[1] < BISECT_ITERS)

    def bisect(carry):
        _, it, lo, hi, clo, chi = carry
        span = clo - chi
        aim = (clo - (kf + 0.5)) / span
        interpolate = jnp.logical_and(span < INTERP_SPAN, it % 2 == 0)
        mid = lo + jnp.where(interpolate, aim, 0.5) * (hi - lo)
        c = count_ge(mid)
        active = clo > kf
        up = jnp.logical_and(active, c >= kf)
        down = jnp.logical_and(active, c < kf)
        clo = jnp.where(up, c, clo)
        chi = jnp.where(down, c, chi)
        return (unresolved(clo, chi), it + 1, jnp.where(up, mid, lo), jnp.where(down, mid, hi), clo, chi)

    _, _, lo, hi, clo, chi = lax.while_loop(
        bisect_cond, bisect, (unresolved(clo0, chi0), jnp.int32(0), lo0, hi0, clo0, chi0))

    done0 = jnp.where(clo <= kf, 1.0, 0.0)

    def pop_cond(carry):
        return jnp.logical_and(carry[0] > 0.0, carry[1] < MAX_POPS)

    def next_below(hi):
        def one(c):
            hib = to_rows(hi[chunk_subl(c)])
            best = fold_keys(c, lambda a, s: jnp.maximum(a, jnp.where(s < hib, s, -jnp.inf)),
                             jnp.full(tile, -jnp.inf, F32))
            return to_dense(jnp.broadcast_to(_rowmax(best), tile))
        return dense_of(one)

    def pop(carry):
        _, it, hi, chi, thr, need, done = carry
        v = next_below(hi)
        cv = count_eq(v)
        active = done < 0.5
        fin = jnp.logical_and(active, chi + cv >= kf)
        go = jnp.logical_and(active, chi + cv < kf)
        thr = jnp.where(fin, v, thr)
        need = jnp.where(fin, kf - chi, need)
        done = jnp.where(fin, 1.0, done)
        hi = jnp.where(go, v, hi)
        chi = jnp.where(go, chi + cv, chi)
        return (jnp.max(1.0 - done), it + 1, hi, chi, thr, need, done)

    left0 = jnp.max(1.0 - done0)
    _, _, _, _, thr, need, _ = lax.while_loop(
        pop_cond, pop, (left0, jnp.int32(0), hi, chi, lo, need0, done0))
    thrb = jnp.concatenate([to_rows(thr[chunk_subl(c)]) for c in range(n_chunks)], axis=0)
    needb = jnp.concatenate([to_rows(need[chunk_subl(c)]) for c in range(n_chunks)], axis=0)
    partial_ties = jnp.max(jnp.where(need < COUNT_ALL, 1.0, 0.0))

    @pl.when(partial_ties > 0.0)
    def _():
        def tie_block(kb, run):
            s = s_s[kb]
            eq = jnp.where(s == thrb, 1.0, 0.0)
            rank = _dot(eq.astype(BF16), tri_ref[1]) + run
            take = jnp.where(s == thrb, jnp.where(rank < needb, 0.0, NEG_BIG), NEG_BIG)
            s_s[kb] = jnp.where(s > thrb, 0.0, take)
            return run + bcast(_rowsum(eq))
        lax.fori_loop(0, nkb, tie_block, jnp.zeros(shape, F32))

    @pl.when(partial_ties <= 0.0)
    def _():
        def sel_block(kb, carry):
            s_s[kb] = jnp.where(s_s[kb] >= thrb, 0.0, NEG_BIG)
            return carry
        lax.fori_loop(0, nkb, sel_block, 0)

    rel_pos = (row - col).astype(F32)
    first_half = col < HEAD_DIM
    qs_s[...] = q_ref[...] * (HEAD_DIM ** -0.5)
    mx_s[...] = jnp.full(mx_s.shape, NEG_BIG, F32)
    al_s[...] = jnp.zeros_like(al_s)
    slopes = [2.0 ** (-8.0 * (h + 1) / N_HEADS) for h in range(N_HEADS)]

    def attend(g, carry):
        blocks = [g * n_diag + u for u in range(n_diag)]
        dists = [jnp.abs(rel_pos + (i * qb - kb * kb_size).astype(F32)) for kb in blocks]
        for p, ps in enumerate(pairs):
            tiles = []
            for kb, dist in zip(blocks, dists):
                mask = s_s[kb]
                kbd = _pair_blockdiag(k_ref[kb, :, ps], first_ref, second_ref)
                bias = jnp.concatenate([mask - slopes[2 * p] * dist, mask - slopes[2 * p + 1] * dist], axis=1)
                tiles.append(_dot_nt(qs_s[:, ps], kbd) + bias)
            top = functools.reduce(jnp.maximum, tiles)
            m_old = mx_s[p]
            m_new = jnp.maximum(m_old, jnp.concatenate(
                [bcast(_rowmax(top[:, :kb_size])), bcast(_rowmax(top[:, kb_size:]))], axis=1))
            mx_s[p] = m_new
            pexp = jnp.concatenate([jnp.exp((t - m_new).astype(BF16)) for t in tiles], axis=1)
            shrink = m_old - m_new
            alpha = jnp.exp(jnp.where(first_half, shrink[:, :kb_size], shrink[:, kb_size:]))
            values = jnp.concatenate(
                [jnp.concatenate([_pair_blockdiag(v_ref[kb, :, ps], first_ref, second_ref), ones_ref[...]], axis=1)
                 for kb in blocks], axis=0)
            al_s[p] = jnp.concatenate([alpha, alpha], axis=1) * al_s[p] + _dot(pexp, values)
        return carry

    lax.fori_loop(0, i + 1, attend, 0)
    for p, ps in enumerate(pairs):
        state = al_s[p]
        o_ref[:, ps] = (state[:, :LANES] / state[:, LANES:]).astype(o_ref.dtype)


def _tri_consts(n):
    j = jnp.arange(n)[:, None]
    s = jnp.arange(n)[None, :]
    return jnp.stack([j > s, j < s]).astype(BF16)


def _pair_consts(n):
    tri = _tri_consts(n)
    zero = jnp.zeros((n, n), BF16)
    tri2 = jnp.block([[tri[0], zero], [zero, tri[0]]])
    first = (jnp.arange(n)[None, :] < n // 2) & jnp.ones((n, 1), bool)
    ones2 = jnp.concatenate([first, ~first], axis=0).astype(BF16)
    return tri2, ones2, first.astype(BF16), (~first).astype(BF16)


def _dsa(q, k, v, qi, kk, wq, topk, qb):
    bsz, seq, width = q.shape
    nq, nk = seq // qb, seq // ATT_BLOCK
    qblk = lambda a: a.reshape(bsz, nq, qb, a.shape[-1])
    kblk = lambda a: a.reshape(bsz, nk, ATT_BLOCK, a.shape[-1])
    qspec = lambda wd: pl.BlockSpec((None, None, qb, wd), lambda b, i: (b, i, 0, 0))
    kspec = lambda wd: pl.BlockSpec((None, nk, ATT_BLOCK, wd), lambda b, i: (b, 0, 0, 0))
    _, ones2, first, second = _pair_consts(ATT_BLOCK)
    consts = (_tri_consts(ATT_BLOCK), ones2, first, second)
    out = pl.pallas_call(
        functools.partial(_dsa_kernel, topk=topk, qb=qb),
        grid=(bsz, nq),
        in_specs=[qspec(width), qspec(width), qspec(LANES), kspec(width), kspec(width), kspec(LANES)]
                 + [pl.BlockSpec(a.shape, lambda b, i, nd=a.ndim: (0,) * nd) for a in consts],
        out_specs=qspec(width),
        out_shape=jax.ShapeDtypeStruct((bsz, nq, qb, width), BF16),
        scratch_shapes=[pltpu.VMEM((nk, qb, ATT_BLOCK), F32),
                        pltpu.VMEM((qb, width), BF16),
                        pltpu.VMEM((N_HEADS // 2, qb, 2 * ATT_BLOCK), F32),
                        pltpu.VMEM((N_HEADS // 2, qb, 2 * LANES), F32)],
        compiler_params=_cparams(("parallel", "arbitrary")),
    )(qblk(q), qblk(qi), qblk(wq), kblk(k), kblk(v), kblk(kk), *consts)
    return out.reshape(bsz, seq, width)


def _sb_kernel(q_ref, k_ref, v_ref, tri_ref, ones_ref, first_ref, second_ref, o_ref, qs_s, run_s, acc_s, *, qb):
    i = pl.program_id(1)
    kb_size = ATT_BLOCK
    n_diag = qb // kb_size
    n_pairs = N_HEADS // 2
    shape = (qb, 2 * kb_size)
    row = lax.broadcasted_iota(jnp.int32, shape, 0)
    col = lax.broadcasted_iota(jnp.int32, shape, 1) % kb_size
    qs_s[...] = q_ref[...] * (HEAD_DIM ** -0.5)
    run_s[...] = jnp.zeros_like(run_s)
    acc_s[...] = jnp.zeros_like(acc_s)

    def step(kb, masked, r0=0):
        rows = slice(r0, qb)
        causal = ((kb * kb_size + col) < (i * qb + row))[rows]
        pairs = [slice(p * LANES, (p + 1) * LANES) for p in range(n_pairs)]
        log_sig, log_keep = [], []
        for ps in pairs:
            kbd = _pair_blockdiag(k_ref[kb, :, ps], first_ref, second_ref)
            z = _dot_nt(qs_s[rows, ps], kbd).astype(BF16)
            soft = jnp.log(1.0 + jnp.exp(-jnp.abs(z)))
            keep = -jnp.maximum(z, 0.0) - soft
            log_sig.append(jnp.minimum(z, 0.0) - soft)
            log_keep.append(jnp.where(causal, keep, jnp.zeros_like(keep)) if masked else keep)
        for p, ps in enumerate(pairs):
            after = _dot(log_keep[p], tri_ref[...]).astype(BF16)
            w = jnp.exp(log_sig[p] + after)
            if masked:
                w = jnp.where(causal, w, jnp.zeros_like(w))
            vbd = _pair_blockdiag(v_ref[kb, :, ps], first_ref, second_ref)
            acc_s[p, rows] += jnp.exp(run_s[p, rows]) * _dot(w, vbd)
            run_s[p, rows] += _dot(log_keep[p], ones_ref[...])

    for u in reversed(range(n_diag)):
        step(i * n_diag + u, True, u * kb_size)

    def past_step(j, carry):
        step(i * n_diag - 1 - j, False)
        return carry

    lax.fori_loop(0, i * n_diag, past_step, 0)
    for p in range(n_pairs):
        o_ref[:, p * LANES:(p + 1) * LANES] = acc_s[p].astype(o_ref.dtype)


def _stick_breaking(q, k, v, qb):
    bsz, seq, width = q.shape
    nq, nk = seq // qb, seq // ATT_BLOCK
    qspec = pl.BlockSpec((None, None, qb, width), lambda b, i: (b, i, 0, 0))
    kspec = pl.BlockSpec((None, nk, ATT_BLOCK, width), lambda b, i: (b, 0, 0, 0))
    consts = _pair_consts(ATT_BLOCK)
    out = pl.pallas_call(
        functools.partial(_sb_kernel, qb=qb),
        grid=(bsz, nq),
        in_specs=[qspec, kspec, kspec] + [pl.BlockSpec(a.shape, lambda b, i: (0, 0)) for a in consts],
        out_specs=qspec,
        out_shape=jax.ShapeDtypeStruct((bsz, nq, qb, width), BF16),
        scratch_shapes=[pltpu.VMEM((qb, width), BF16),
                        pltpu.VMEM((N_HEADS // 2, qb, LANES), F32),
                        pltpu.VMEM((N_HEADS // 2, qb, LANES), F32)],
        compiler_params=_cparams(("parallel", "arbitrary")),
    )(q.reshape(bsz, nq, qb, width), k.reshape(bsz, nk, ATT_BLOCK, width),
      v.reshape(bsz, nk, ATT_BLOCK, width), *consts)
    return out.reshape(bsz, seq, width)


def _cd_weight(cd_w_in):
    cw = N_HEADS * HEAD_DIM
    iw = IDX_HEADS * IDX_DIM
    offs = [0, cw, 2 * cw, 3 * cw, 3 * cw + iw, 3 * cw + iw + IDX_DIM, 3 * cw + iw + IDX_DIM + IDX_HEADS]
    qkv_c = cd_w_in[:, :offs[3]]
    q_idx = cd_w_in[:, offs[3]:offs[4]]
    k_idx = cd_w_in[:, offs[4]:offs[5]]
    w_idx = cd_w_in[:, offs[5]:offs[6]]
    qkv_d = cd_w_in[:, offs[6]:]
    pad = jnp.zeros((cd_w_in.shape[0], LANES - w_idx.shape[1]), cd_w_in.dtype)
    return jnp.concatenate([qkv_c, q_idx, qkv_d, k_idx, k_idx, w_idx, pad], axis=1)


def kernel(x, c, norm_g, ada_w, ada_b, mlp_w1, mlp_w2, ab_w_in, ab_w_out, pool_w, pool_scale, ssm_lam_re, ssm_lam_im, ssm_log_dt, ssm_b_re, ssm_b_im, ssm_c_re, ssm_c_im, ssm_d, glu_w, glu_b, cd_w_in, cd_w_out, final_g):
    bsz, seq, d = x.shape
    depth = norm_g.shape[0]
    topk = min(TOPK_MAX, seq // 4)
    tm = min(512, seq)

    c_pad = jnp.zeros((SUBLANES, d), F32).at[:bsz].set(c)
    mod = _modulation(c_pad, ada_w.reshape(depth * 2, d, 3 * d), ada_b.reshape(depth * 2, 1, 3 * d))
    mod = mod[:, :, :bsz, None, :]

    for i in range(depth):
        j = i // 2
        shift, scale1, gate1 = mod[2 * i, 0], mod[2 * i, 1], mod[2 * i, 2]
        g = norm_g[i, 0][None, :]
        if i % 2 == 0:
            width = ab_w_in.shape[2]
            tt = min(256, seq)
            (proj,) = _norm_proj(x, g, shift, scale1, ab_w_in[j].astype(BF16), (width,), (BF16,), tm)
            s5 = _s5_params(ssm_lam_re[j], ssm_lam_im[j], ssm_log_dt[j],
                            ssm_b_re[j], ssm_b_im[j], ssm_c_re[j], ssm_c_im[j], seg_len=tt // SUBLANES)
            y = _even_mixer(proj, pool_w[j].astype(BF16), pool_scale[j][None, :], s5,
                            ssm_d[j][None, :], glu_w[j].astype(BF16), glu_b[j][None, :],
                            tt=tt, scan_lanes=512)
            x = _out_proj(x, gate1, (y,), ab_w_out[j].astype(BF16), tm)
        else:
            hw = N_HEADS * HEAD_DIM
            widths = (hw,) * 7 + (LANES, LANES)
            dtypes = (BF16,) * 8 + (F32,)
            qc, kc, vc, qi, qd, kd, vd, kk, wq = _norm_proj(
                x, g, shift, scale1, _cd_weight(cd_w_in[j]).astype(BF16), widths, dtypes, tm)
            oc = _dsa(qc, kc, vc, qi, kk, wq, topk, qb=ATT_QUERY_BLOCK)
            od = _stick_breaking(qd, kd, vd, qb=ATT_QUERY_BLOCK)
            x = _out_proj(x, gate1, (oc, od), cd_w_out[j].astype(BF16), tm)
        shift, scale1, gate1 = mod[2 * i + 1, 0], mod[2 * i + 1, 1], mod[2 * i + 1, 2]
        x = _mlp(x, norm_g[i, 1][None, :], shift, scale1, gate1,
                 mlp_w1[i].astype(BF16), mlp_w2[i].astype(BF16), final_g[None, :],
                 final_norm=(i == depth - 1), tm=min(1024, seq), tf=1024)
    return x
```

```python
import functools
import math

import jax
import jax.numpy as jnp
from jax import lax
from jax.experimental import pallas as pl
from jax.experimental.pallas import tpu as pltpu

F32 = jnp.float32
BF16 = jnp.bfloat16

EPS = 1e-6
CHUNK = 64
POOL_WINDOWS = (2, 4, 8, 16)
POOL_CH = 128
SSM_GROUP = 16
SSM_STATE = 64
HEAD_DIM = 64
N_HEADS = 8
IDX_HEADS = 8
IDX_DIM = 64
IDX_SCALE = (IDX_HEADS ** -0.5) * (IDX_DIM ** -0.5)
TOPK_MAX = 256

LANES = 128
SUBLANES = 8
VMEM_LIMIT = 56 * 1024 * 1024

ATT_BLOCK = 128
ATT_QUERY_BLOCK = 512
NEG_BIG = -1e30
COUNT_ALL = 1e9
BISECT_ITERS = 28
COUNT_ROWS = 128
POP_SPAN = 2.0
INTERP_SPAN = 256.0
MAX_POPS = 4096


def _cparams(sem):
    return pltpu.CompilerParams(dimension_semantics=sem, vmem_limit_bytes=VMEM_LIMIT)


def _dot(a, b):
    return jnp.dot(a, b, preferred_element_type=F32)


def _dot_nt(a, b):
    return lax.dot_general(a, b, (((1,), (1,)), ((), ())), preferred_element_type=F32)


def _rowsum(x):
    return jnp.sum(x, axis=1, keepdims=True)


def _rowmax(x):
    return jnp.max(x, axis=1, keepdims=True)


def _norm_modulate(x, g, shift, scale1):
    ms = jnp.mean(x * x, axis=-1, keepdims=True)
    return (x * lax.rsqrt(ms + EPS)) * g * scale1 + shift


def _mod_kernel(cb_ref, w_ref, b_ref, o_ref):
    j = pl.program_id(1)
    n_b, n_slabs = cb_ref.shape[0], cb_ref.shape[1]
    tn = w_ref.shape[1]

    def slab(kc, accs):
        w = w_ref[pl.ds(pl.multiple_of(kc * SUBLANES, SUBLANES), SUBLANES), :]
        return tuple(acc + w * jnp.concatenate([cb_ref[b, kc]] * (tn // LANES), axis=1) for b, acc in enumerate(accs))

    accs = lax.fori_loop(0, n_slabs, slab, tuple(jnp.zeros((SUBLANES, tn), F32) for _ in range(n_b)), unroll=4)
    m = jnp.concatenate([jnp.sum(acc, axis=0, keepdims=True) for acc in accs], axis=0) + b_ref[...]
    o_ref[...] = m + jnp.where(j > 0, 1.0, 0.0)


def _modulation(c, ada_w, ada_b):
    n_sub, d, _ = ada_w.shape
    bsz = c.shape[0]
    cb = jnp.broadcast_to(c.reshape(bsz, d // SUBLANES, SUBLANES, 1), (bsz, d // SUBLANES, SUBLANES, LANES))
    return pl.pallas_call(
        _mod_kernel,
        grid=(n_sub, 3),
        in_specs=[
            pl.BlockSpec(cb.shape, lambda s, j: (0, 0, 0, 0)),
            pl.BlockSpec((None, d, d), lambda s, j: (s, 0, j)),
            pl.BlockSpec((None, 1, d), lambda s, j: (s, 0, j)),
        ],
        out_specs=pl.BlockSpec((None, None, bsz, d), lambda s, j: (s, j, 0, 0)),
        out_shape=jax.ShapeDtypeStruct((n_sub, 3, bsz, d), F32),
        compiler_params=_cparams(("arbitrary", "arbitrary")),
    )(cb, ada_w, ada_b)


def _norm_proj_kernel(x_ref, g_ref, shift_ref, scale_ref, w_ref, *o_refs, widths):
    h = _norm_modulate(x_ref[...], g_ref[...], shift_ref[...], scale_ref[...]).astype(BF16)
    y = _dot(h, w_ref[...])
    off = 0
    for o_ref, width in zip(o_refs, widths):
        o_ref[...] = y[:, off:off + width].astype(o_ref.dtype)
        off += width


def _norm_proj(x, g, shift, scale1, w, widths, dtypes, tm):
    bsz, seq, d = x.shape
    n = w.shape[1]
    vec = pl.BlockSpec((None, 1, d), lambda b, i: (b, 0, 0))
    return pl.pallas_call(
        functools.partial(_norm_proj_kernel, widths=widths),
        grid=(bsz, seq // tm),
        in_specs=[
            pl.BlockSpec((None, tm, d), lambda b, i: (b, i, 0)),
            pl.BlockSpec((1, d), lambda b, i: (0, 0)),
            vec, vec,
            pl.BlockSpec((d, n), lambda b, i: (0, 0)),
        ],
        out_specs=[pl.BlockSpec((None, tm, wd), lambda b, i: (b, i, 0)) for wd in widths],
        out_shape=[jax.ShapeDtypeStruct((bsz, seq, wd), dt) for wd, dt in zip(widths, dtypes)],
        compiler_params=_cparams(("parallel", "parallel")),
    )(x, g, shift, scale1, w)


def _out_proj_kernel(x_ref, gate_ref, *refs, n_y):
    y_refs, w_ref, o_ref = refs[:n_y], refs[n_y], refs[n_y + 1]
    y = jnp.concatenate([r[...] for r in y_refs], axis=1) if n_y > 1 else y_refs[0][...]
    o_ref[...] = x_ref[...] + gate_ref[...] * _dot(y, w_ref[...])


def _out_proj(x, gate1, ys, w, tm):
    bsz, seq, d = x.shape
    k = w.shape[0]
    return pl.pallas_call(
        functools.partial(_out_proj_kernel, n_y=len(ys)),
        grid=(bsz, seq // tm),
        in_specs=[
            pl.BlockSpec((None, tm, d), lambda b, i: (b, i, 0)),
            pl.BlockSpec((None, 1, d), lambda b, i: (b, 0, 0)),
            *[pl.BlockSpec((None, tm, y.shape[2]), lambda b, i: (b, i, 0)) for y in ys],
            pl.BlockSpec((k, d), lambda b, i: (0, 0)),
        ],
        out_specs=pl.BlockSpec((None, tm, d), lambda b, i: (b, i, 0)),
        out_shape=jax.ShapeDtypeStruct((bsz, seq, d), F32),
        compiler_params=_cparams(("parallel", "parallel")),
    )(x, gate1, *ys, w)


def _mlp_kernel(x_ref, g_ref, shift_ref, scale_ref, gate_ref, w1_ref, w2_ref, fg_ref, o_ref,
                h_s, acc_s, *, final_norm):
    f = pl.program_id(2)

    @pl.when(f == 0)
    def _():
        h_s[...] = _norm_modulate(x_ref[...], g_ref[...], shift_ref[...], scale_ref[...]).astype(BF16)
        acc_s[...] = jnp.zeros_like(acc_s)

    a = jnp.maximum(_dot(h_s[...], w1_ref[...]), 0.0)
    acc_s[...] += _dot((a * a).astype(BF16), w2_ref[...])

    @pl.when(f == pl.num_programs(2) - 1)
    def _():
        y = x_ref[...] + gate_ref[...] * acc_s[...]
        if final_norm:
            ms = jnp.mean(y * y, axis=-1, keepdims=True)
            y = (y * lax.rsqrt(ms + EPS)) * fg_ref[...]
        o_ref[...] = y


def _mlp(x, g, shift, scale1, gate1, w1, w2, final_g, final_norm, tm, tf):
    bsz, seq, d = x.shape
    dff = w1.shape[1]
    vec = pl.BlockSpec((None, 1, d), lambda b, i, f: (b, 0, 0))
    one = pl.BlockSpec((1, d), lambda b, i, f: (0, 0))
    return pl.pallas_call(
        functools.partial(_mlp_kernel, final_norm=final_norm),
        grid=(bsz, seq // tm, dff // tf),
        in_specs=[
            pl.BlockSpec((None, tm, d), lambda b, i, f: (b, i, 0)),
            one, vec, vec, vec,
            pl.BlockSpec((d, tf), lambda b, i, f: (0, f)),
            pl.BlockSpec((tf, d), lambda b, i, f: (f, 0)),
            one,
        ],
        out_specs=pl.BlockSpec((None, tm, d), lambda b, i, f: (b, i, 0)),
        out_shape=jax.ShapeDtypeStruct((bsz, seq, d), F32),
        scratch_shapes=[pltpu.VMEM((tm, d), BF16), pltpu.VMEM((tm, d), F32)],
        compiler_params=_cparams(("parallel", "parallel", "arbitrary")),
    )(x, g, shift, scale1, gate1, w1, w2, final_g)


def _cmul(ar, ai, br, bi):
    return ar * br - ai * bi, ar * bi + ai * br


def _s5_params_kernel(lr_ref, li_ref, ldt_ref, brt_ref, bit_ref, crt_ref, cit_ref,
                      wb_ref, wc_ref, a_re_ref, a_im_ref, pk_re_ref, pk_im_ref, sg_re_ref, sg_im_ref):
    lr, li = lr_ref[...], li_ref[...]
    dt = jnp.exp(ldt_ref[...])
    mag = jnp.exp(lr * dt)
    ar, ai = mag * jnp.cos(li * dt), mag * jnp.sin(li * dt)
    den = lr * lr + li * li
    zr = ((ar - 1.0) * lr + ai * li) / den
    zi = (ai * lr - (ar - 1.0) * li) / den

    rows, cols = brt_ref.shape
    rgrp = lax.broadcasted_iota(jnp.int32, (rows, cols), 0) // SSM_GROUP
    cgrp = lax.broadcasted_iota(jnp.int32, (rows, cols), 1) // SSM_STATE
    diag = rgrp == cgrp
    br, bi = brt_ref[...], bit_ref[...]
    wb_ref[0] = jnp.where(diag, zr * br - zi * bi, 0.0).astype(wb_ref.dtype)
    wb_ref[1] = jnp.where(diag, zr * bi + zi * br, 0.0).astype(wb_ref.dtype)
    wc_ref[0] = jnp.where(diag, crt_ref[...], 0.0).astype(wc_ref.dtype)
    wc_ref[1] = jnp.where(diag, cit_ref[...], 0.0).astype(wc_ref.dtype)

    n = lr.shape[1]
    rows8 = pk_re_ref.shape[0]
    seg_len = rows8 // SUBLANES
    a_re_ref[...] = jnp.broadcast_to(ar, (SUBLANES, n))
    a_im_ref[...] = jnp.broadcast_to(ai, (SUBLANES, n))
    expo = lax.broadcasted_iota(jnp.int32, (rows8, n), 0) // SUBLANES + 1
    sq_r, sq_i = jnp.broadcast_to(ar, (rows8, n)), jnp.broadcast_to(ai, (rows8, n))
    pr, pi = jnp.ones((rows8, n), F32), jnp.zeros((rows8, n), F32)
    for b in range(seg_len.bit_length()):
        nr, ni = _cmul(pr, pi, sq_r, sq_i)
        take = ((expo >> b) & 1) == 1
        pr, pi = jnp.where(take, nr, pr), jnp.where(take, ni, pi)
        sq_r, sq_i = _cmul(sq_r, sq_i, sq_r, sq_i)
    pk_re_ref[...] = pr
    pk_im_ref[...] = pi
    row = lax.broadcasted_iota(jnp.int32, (SUBLANES, n), 0)
    gr, gi = pr[rows8 - SUBLANES:], pi[rows8 - SUBLANES:]
    for k in range(3):
        keep = row >= (1 << k)
        sg_re_ref[k] = jnp.where(keep, gr, 0.0)
        sg_im_ref[k] = jnp.where(keep, gi, 0.0)
        gr, gi = _cmul(gr, gi, gr, gi)


def _s5_params(lam_re, lam_im, log_dt, b_re, b_im, c_re, c_im, seg_len):
    g, p = lam_re.shape
    c = b_re.shape[2]
    n = g * p
    flat = lambda a: a.reshape(1, n)
    ldt = jnp.broadcast_to(log_dt[:, None], (g, p)).reshape(1, n)
    b_t = lambda a: jnp.tile(jnp.transpose(a, (2, 0, 1)).reshape(c, n), (g, 1))
    c_t = lambda a: jnp.tile(a.reshape(g * c, p), (1, g))
    outs = pl.pallas_call(
        _s5_params_kernel,
        out_shape=[jax.ShapeDtypeStruct((2, g * c, n), BF16), jax.ShapeDtypeStruct((2, g * c, n), BF16),
                   jax.ShapeDtypeStruct((SUBLANES, n), F32), jax.ShapeDtypeStruct((SUBLANES, n), F32),
                   jax.ShapeDtypeStruct((seg_len * SUBLANES, n), F32), jax.ShapeDtypeStruct((seg_len * SUBLANES, n), F32),
                   jax.ShapeDtypeStruct((3, SUBLANES, n), F32), jax.ShapeDtypeStruct((3, SUBLANES, n), F32)],
        compiler_params=pltpu.CompilerParams(vmem_limit_bytes=VMEM_LIMIT),
    )(flat(lam_re), flat(lam_im), ldt, b_t(b_re), b_t(b_im), c_t(c_re), c_t(c_im))
    wb, wct, a_re, a_im, pk_re, pk_im, sg_re, sg_im = outs
    wc = jnp.transpose(wct, (0, 2, 1))
    pk_re, pk_im = (a.reshape(seg_len, SUBLANES, n) for a in (pk_re, pk_im))
    return wb, wc, a_re, a_im, pk_re, pk_im, sg_re, sg_im


def _gelu_tanh(x):
    return 0.5 * x * (1.0 + jnp.tanh(math.sqrt(2.0 / math.pi) * (x + 0.044715 * (x * x * x))))


def _even_kernel(p_ref, band_ref, perm_ref, wpool_ref, pscale_ref, wb_ref, wc_ref, dskip_ref, wglu_ref, bglu_ref,
                 a_re_ref, a_im_ref, pk_re_ref, pk_im_ref, sg_re_ref, sg_im_ref,
                 o_ref, prev_s, xr_s, xi_s, car_re, car_im, *, mix_a, scan_lanes):
    i = pl.program_id(1)
    tt = p_ref.shape[0]

    @pl.when(i == 0)
    def _():
        prev_s[...] = jnp.zeros_like(prev_s)
        car_re[...] = jnp.zeros_like(car_re)
        car_im[...] = jnp.zeros_like(car_im)

    ua = p_ref[:, :mix_a]
    ext = jnp.concatenate([prev_s[...], ua], axis=0)
    t1 = i * tt + lax.broadcasted_iota(jnp.int32, (tt, POOL_CH), 0) + 1
    for g, win in enumerate(POOL_WINDOWS):
        sl = slice(g * POOL_CH, (g + 1) * POOL_CH)
        wsum = _dot(band_ref[g], ext[:, sl])
        cnt = jnp.minimum(t1, win).astype(F32)
        pg = wsum / cnt - ua[:, sl].astype(F32)
        yg = _dot(pg.astype(BF16), wpool_ref[g]) * pscale_ref[:, sl]
        o_ref[:, sl] = yg.astype(o_ref.dtype)
    prev_s[...] = ua

    ub = _dot(perm_ref[0], p_ref[:, mix_a:]).astype(BF16)
    n_tiles = xr_s.shape[0]
    n_diag_blocks = ub.shape[1] // LANES
    tpb = n_tiles // n_diag_blocks
    sw = tpb * LANES
    for j in range(n_diag_blocks):
        cs, ss = slice(j * LANES, (j + 1) * LANES), slice(j * sw, (j + 1) * sw)
        xr, xi = _dot(ub[:, cs], wb_ref[0, cs, ss]), _dot(ub[:, cs], wb_ref[1, cs, ss])
        for q in range(tpb):
            xr_s[j * tpb + q] = xr[:, q * LANES:(q + 1) * LANES]
            xi_s[j * tpb + q] = xi[:, q * LANES:(q + 1) * LANES]

    seg_len = tt // SUBLANES
    seg_rows = lambda k: pl.ds(pl.multiple_of(k * SUBLANES, SUBLANES), SUBLANES)
    first_seg = lax.broadcasted_iota(jnp.int32, (SUBLANES, LANES), 0) == 0
    zeros = jnp.zeros((SUBLANES, LANES), F32)
    tpc = scan_lanes // LANES
    for c in range(n_tiles // tpc):
        tiles = list(range(c * tpc, (c + 1) * tpc))
        lanes = [slice(t * LANES, (t + 1) * LANES) for t in tiles]

        def local(k, carry, tiles=tiles, lanes=lanes):
            out = []
            for t, ls, (hr, hi) in zip(tiles, lanes, carry):
                dr, di = _cmul(a_re_ref[:, ls], a_im_ref[:, ls], hr, hi)
                hr, hi = dr + xr_s[t, seg_rows(k), :], di + xi_s[t, seg_rows(k), :]
                xr_s[t, seg_rows(k), :] = hr
                xi_s[t, seg_rows(k), :] = hi
                out.append((hr, hi))
            return tuple(out)

        ends = lax.fori_loop(0, seg_len, local, tuple((zeros, zeros) for _ in tiles))
        starts = []
        for t, ls, (er, ei) in zip(tiles, lanes, ends):
            fr = jnp.where(first_seg, car_re[:, ls], pltpu.roll(er, 1, 0))
            fi = jnp.where(first_seg, car_im[:, ls], pltpu.roll(ei, 1, 0))
            for k in range(3):
                dr, di = _cmul(sg_re_ref[k, :, ls], sg_im_ref[k, :, ls],
                               pltpu.roll(fr, 1 << k, 0), pltpu.roll(fi, 1 << k, 0))
                fr, fi = fr + dr, fi + di
            starts.append((fr, fi))
            dr, di = _cmul(pk_re_ref[seg_len - 1, :, ls], pk_im_ref[seg_len - 1, :, ls], fr, fi)
            car_re[:, ls] = jnp.broadcast_to((dr + er)[SUBLANES - 1:], fr.shape)
            car_im[:, ls] = jnp.broadcast_to((di + ei)[SUBLANES - 1:], fr.shape)

        def fix(k, carry, tiles=tiles, lanes=lanes, starts=starts):
            for t, ls, (fr, fi) in zip(tiles, lanes, starts):
                dr, di = _cmul(pk_re_ref[k, :, ls], pk_im_ref[k, :, ls], fr, fi)
                xr_s[t, seg_rows(k), :] += dr
                xi_s[t, seg_rows(k), :] += di
            return carry

        lax.fori_loop(0, seg_len, fix, 0)

    state = lambda ref, j: jnp.concatenate([ref[j * tpb + q] for q in range(tpb)], axis=1).astype(BF16)
    y = jnp.concatenate(
        [_dot(state(xr_s, j), wc_ref[0, j * sw:(j + 1) * sw, j * LANES:(j + 1) * LANES])
         - _dot(state(xi_s, j), wc_ref[1, j * sw:(j + 1) * sw, j * LANES:(j + 1) * LANES])
         for j in range(n_diag_blocks)], axis=1)
    y = _gelu_tanh(y + dskip_ref[...] * ub.astype(F32))
    z = _dot(y.astype(BF16), wglu_ref[...]) + bglu_ref[...]
    out = (y * (1.0 / (1.0 + jnp.exp(-z)))).astype(o_ref.dtype)
    o_ref[:, mix_a:] = _dot(perm_ref[1], out).astype(o_ref.dtype)


def _segment_perm(tt):
    seg_len = tt // SUBLANES
    r = jnp.arange(tt)
    src = (r % SUBLANES) * seg_len + r // SUBLANES
    p = (src[:, None] == jnp.arange(tt)[None, :])
    return jnp.stack([p, p.T]).astype(BF16)


def _pool_band(tt):
    t = jnp.arange(tt)[:, None] + tt
    j = jnp.arange(2 * tt)[None, :]
    return jnp.stack([((j <= t) & (j > t - w)) for w in POOL_WINDOWS]).astype(BF16)


def _even_mixer(proj, wpool, pscale, s5, dskip, wglu, bglu, tt, scan_lanes):
    bsz, seq, width = proj.shape
    wb, wc, a_re, a_im, pk_re, pk_im, sg_re, sg_im = s5
    mix_a = len(POOL_WINDOWS) * POOL_CH
    mix_b = width - mix_a
    n_state = wb.shape[2]
    const = lambda a: pl.BlockSpec(a.shape, lambda b, i, nd=a.ndim: (0,) * nd)
    band = _pool_band(tt)
    args = (band, _segment_perm(tt), wpool, pscale, wb, wc, dskip, wglu, bglu, a_re, a_im, pk_re, pk_im, sg_re, sg_im)
    return pl.pallas_call(
        functools.partial(_even_kernel, mix_a=mix_a, scan_lanes=scan_lanes),
        grid=(bsz, seq // tt),
        in_specs=[pl.BlockSpec((None, tt, width), lambda b, i: (b, i, 0))] + [const(a) for a in args],
        out_specs=pl.BlockSpec((None, tt, width), lambda b, i: (b, i, 0)),
        out_shape=jax.ShapeDtypeStruct((bsz, seq, width), BF16),
        scratch_shapes=[pltpu.VMEM((tt, mix_a), BF16),
                        pltpu.VMEM((n_state // LANES, tt, LANES), F32), pltpu.VMEM((n_state // LANES, tt, LANES), F32),
                        pltpu.VMEM((SUBLANES, n_state), F32), pltpu.VMEM((SUBLANES, n_state), F32)],
        compiler_params=_cparams(("parallel", "arbitrary")),
    )(proj, *args)


def _pair_blockdiag(x, first_ref, second_ref):
    return jnp.concatenate([x * first_ref[...], x * second_ref[...]], axis=0)


def _dsa_kernel(q_ref, qi_ref, wq_ref, k_ref, v_ref, kk_ref, tri_ref, ones_ref, first_ref, second_ref, o_ref,
                s_s, qs_s, mx_s, al_s, *, topk, qb):
    i = pl.program_id(1)
    kb_size = ATT_BLOCK
    n_diag = qb // kb_size
    nkb = (i + 1) * n_diag
    shape = (qb, kb_size)
    row = lax.broadcasted_iota(jnp.int32, shape, 0)
    col = lax.broadcasted_iota(jnp.int32, shape, 1)
    q_chunk = (i * qb + row) // CHUNK
    bcast = lambda a: jnp.broadcast_to(a, shape)
    kf = float(topk)
    pairs = [slice(p * LANES, (p + 1) * LANES) for p in range(N_HEADS // 2)]

    wq = wq_ref[...]
    w_cols = [bcast(wq[:, h:h + 1] * IDX_SCALE) for h in range(IDX_HEADS)]

    def score_block(kb, carry):
        rmax, rmin = carry
        kbd = _pair_blockdiag(kk_ref[kb], first_ref, second_ref)
        acc = jnp.zeros(shape, F32)
        for p, ps in enumerate(pairs):
            rel = jnp.maximum(_dot_nt(qi_ref[:, ps], kbd), 0.0)
            acc = acc + rel[:, :kb_size] * w_cols[2 * p] + rel[:, kb_size:] * w_cols[2 * p + 1]
        adm = ((kb * kb_size + col) // CHUNK) <= q_chunk
        s_s[kb] = jnp.where(adm, acc, -jnp.inf)
        return jnp.maximum(rmax, jnp.where(adm, acc, -jnp.inf)), jnp.minimum(rmin, jnp.where(adm, acc, jnp.inf))

    rmax, rmin = lax.fori_loop(0, nkb, score_block,
                               (jnp.full(shape, -jnp.inf, F32), jnp.full(shape, jnp.inf, F32)))
    n_chunks = qb // kb_size
    chunk_rows = lambda c: slice(c * kb_size, (c + 1) * kb_size)
    chunk_subl = lambda c: slice(c * SUBLANES, (c + 1) * SUBLANES)
    ones8 = jnp.ones((SUBLANES, kb_size), BF16)
    tile = (kb_size, kb_size)

    def to_dense(rep):
        return rep.T[:SUBLANES]

    def to_rows(dense):
        return jnp.broadcast_to(dense[:1], tile).T

    def dense_of(fn):
        return jnp.concatenate([fn(c) for c in range(n_chunks)], axis=0)

    def fold_keys(c, fn, init):
        def body(g, a):
            for u in range(n_diag):
                a = fn(a, s_s[g * n_diag + u, chunk_rows(c), :])
            return a
        return lax.fori_loop(0, i + 1, body, init)

    def count(x, cmp):
        xbs = [to_rows(x[chunk_subl(c)]) for c in range(n_chunks)]
        cnts = [fold_keys(c, lambda a, s, xb=xbs[c]: a + jnp.where(cmp(s, xb), 1.0, 0.0), jnp.zeros(tile, F32))
                for c in range(n_chunks)]
        return jnp.concatenate([_dot_nt(ones8, cnt.astype(BF16)) for cnt in cnts], axis=0)

    count_ge = functools.partial(count, cmp=lambda s, xb: s >= xb)
    count_gt = functools.partial(count, cmp=lambda s, xb: s > xb)
    count_eq = functools.partial(count, cmp=lambda s, xb: s == xb)

    smax = dense_of(lambda c: to_dense(jnp.broadcast_to(_rowmax(rmax[chunk_rows(c)]), tile)))
    smin = dense_of(lambda c: to_dense(jnp.broadcast_to(jnp.min(rmin[chunk_rows(c)], axis=1, keepdims=True), tile)))
    dshape = (n_chunks * SUBLANES, kb_size)
    t_dense = (i * qb + (lax.broadcasted_iota(jnp.int32, dshape, 0) // SUBLANES) * kb_size
               + lax.broadcasted_iota(jnp.int32, dshape, 1))
    n_adm = ((t_dense // CHUNK + 1) * CHUNK).astype(F32)
    hi0 = smax + jnp.maximum(jnp.abs(smax) * 1e-6, 1e-30)
    zero = jnp.zeros_like(smin)
    cge0, cgt0 = count_ge(zero), count_gt(zero)
    wide = n_adm > kf
    zero_tie = jnp.logical_and(wide, jnp.logical_and(cgt0 < kf, cge0 >= kf))
    above = jnp.logical_and(wide, cgt0 >= kf)
    below = jnp.logical_and(wide, cge0 < kf)
    lo0 = jnp.where(jnp.logical_or(above, zero_tie), 0.0, smin)
    clo0 = jnp.where(zero_tie, kf, jnp.where(above, cge0, n_adm))
    hi0 = jnp.where(below, 0.0, hi0)
    chi0 = jnp.where(below, cge0, 0.0)
    need0 = jnp.where(zero_tie, kf - cgt0, COUNT_ALL)

    def unresolved(clo, chi):
        return jnp.max(jnp.where(jnp.logical_and(clo > kf, clo - chi > POP_SPAN), 1.0, 0.0))

    def bisect_cond(carry):
        return jnp.logical_and(carry[0] > 0.0, carry[1] < BISECT_ITERS)

    def bisect(carry):
        _, it, lo, hi, clo, chi = carry
        span = clo - chi
        aim = (clo - (kf + 0.5)) / span
        interpolate = jnp.logical_and(span < INTERP_SPAN, it % 2 == 0)
        mid = lo + jnp.where(interpolate, aim, 0.5) * (hi - lo)
        c = count_ge(mid)
        active = clo > kf
        up = jnp.logical_and(active, c >= kf)
        down = jnp.logical_and(active, c < kf)
        clo = jnp.where(up, c, clo)
        chi = jnp.where(down, c, chi)
        return (unresolved(clo, chi), it + 1, jnp.where(up, mid, lo), jnp.where(down, mid, hi), clo, chi)

    _, _, lo, hi, clo, chi = lax.while_loop(
        bisect_cond, bisect, (unresolved(clo0, chi0), jnp.int32(0), lo0, hi0, clo0, chi0))

    done0 = jnp.where(clo <= kf, 1.0, 0.0)

    def pop_cond(carry):
        return jnp.logical_and(carry[0] > 0.0, carry[1] < MAX_POPS)

    def next_below(hi):
        def one(c):
            hib = to_rows(hi[chunk_subl(c)])
            best = fold_keys(c, lambda a, s: jnp.maximum(a, jnp.where(s < hib, s, -jnp.inf)),
                             jnp.full(tile, -jnp.inf, F32))
            return to_dense(jnp.broadcast_to(_rowmax(best), tile))
        return dense_of(one)

    def pop(carry):
        _, it, hi, chi, thr, need, done = carry
        v = next_below(hi)
        cv = count_eq(v)
        active = done < 0.5
        fin = jnp.logical_and(active, chi + cv >= kf)
        go = jnp.logical_and(active, chi + cv < kf)
        thr = jnp.where(fin, v, thr)
        need = jnp.where(fin, kf - chi, need)
        done = jnp.where(fin, 1.0, done)
        hi = jnp.where(go, v, hi)
        chi = jnp.where(go, chi + cv, chi)
        return (jnp.max(1.0 - done), it + 1, hi, chi, thr, need, done)

    left0 = jnp.max(1.0 - done0)
    _, _, _, _, thr, need, _ = lax.while_loop(
        pop_cond, pop, (left0, jnp.int32(0), hi, chi, lo, need0, done0))
    thrb = jnp.concatenate([to_rows(thr[chunk_subl(c)]) for c in range(n_chunks)], axis=0)
    needb = jnp.concatenate([to_rows(need[chunk_subl(c)]) for c in range(n_chunks)], axis=0)
    partial_ties = jnp.max(jnp.where(need < COUNT_ALL, 1.0, 0.0))

    @pl.when(partial_ties > 0.0)
    def _():
        def tie_block(kb, run):
            s = s_s[kb]
            eq = jnp.where(s == thrb, 1.0, 0.0)
            rank = _dot(eq.astype(BF16), tri_ref[1]) + run
            take = jnp.where(s == thrb, jnp.where(rank < needb, 0.0, NEG_BIG), NEG_BIG)
            s_s[kb] = jnp.where(s > thrb, 0.0, take)
            return run + bcast(_rowsum(eq))
        lax.fori_loop(0, nkb, tie_block, jnp.zeros(shape, F32))

    @pl.when(partial_ties <= 0.0)
    def _():
        def sel_block(kb, carry):
            s_s[kb] = jnp.where(s_s[kb] >= thrb, 0.0, NEG_BIG)
            return carry
        lax.fori_loop(0, nkb, sel_block, 0)

    rel_pos = (row - col).astype(F32)
    first_half = col < HEAD_DIM
    qs_s[...] = q_ref[...] * (HEAD_DIM ** -0.5)
    mx_s[...] = jnp.full(mx_s.shape, NEG_BIG, F32)
    al_s[...] = jnp.zeros_like(al_s)
    slopes = [2.0 ** (-8.0 * (h + 1) / N_HEADS) for h in range(N_HEADS)]

    def attend(g, carry):
        blocks = [g * n_diag + u for u in range(n_diag)]
        dists = [jnp.abs(rel_pos + (i * qb - kb * kb_size).astype(F32)) for kb in blocks]
        for p, ps in enumerate(pairs):
            tiles = []
            for kb, dist in zip(blocks, dists):
                mask = s_s[kb]
                kbd = _pair_blockdiag(k_ref[kb, :, ps], first_ref, second_ref)
                bias = jnp.concatenate([mask - slopes[2 * p] * dist, mask - slopes[2 * p + 1] * dist], axis=1)
                tiles.append(_dot_nt(qs_s[:, ps], kbd) + bias)
            top = functools.reduce(jnp.maximum, tiles)
            m_old = mx_s[p]
            m_new = jnp.maximum(m_old, jnp.concatenate(
                [bcast(_rowmax(top[:, :kb_size])), bcast(_rowmax(top[:, kb_size:]))], axis=1))
            mx_s[p] = m_new
            pexp = jnp.concatenate([jnp.exp((t - m_new).astype(BF16)) for t in tiles], axis=1)
            shrink = m_old - m_new
            alpha = jnp.exp(jnp.where(first_half, shrink[:, :kb_size], shrink[:, kb_size:]))
            values = jnp.concatenate(
                [jnp.concatenate([_pair_blockdiag(v_ref[kb, :, ps], first_ref, second_ref), ones_ref[...]], axis=1)
                 for kb in blocks], axis=0)
            al_s[p] = jnp.concatenate([alpha, alpha], axis=1) * al_s[p] + _dot(pexp, values)
        return carry

    lax.fori_loop(0, i + 1, attend, 0)
    for p, ps in enumerate(pairs):
        state = al_s[p]
        o_ref[:, ps] = (state[:, :LANES] / state[:, LANES:]).astype(o_ref.dtype)


def _tri_consts(n):
    j = jnp.arange(n)[:, None]
    s = jnp.arange(n)[None, :]
    return jnp.stack([j > s, j < s]).astype(BF16)


def _pair_consts(n):
    tri = _tri_consts(n)
    zero = jnp.zeros((n, n), BF16)
    tri2 = jnp.block([[tri[0], zero], [zero, tri[0]]])
    first = (jnp.arange(n)[None, :] < n // 2) & jnp.ones((n, 1), bool)
    ones2 = jnp.concatenate([first, ~first], axis=0).astype(BF16)
    return tri2, ones2, first.astype(BF16), (~first).astype(BF16)


def _dsa(q, k, v, qi, kk, wq, topk, qb):
    bsz, seq, width = q.shape
    nq, nk = seq // qb, seq // ATT_BLOCK
    qblk = lambda a: a.reshape(bsz, nq, qb, a.shape[-1])
    kblk = lambda a: a.reshape(bsz, nk, ATT_BLOCK, a.shape[-1])
    qspec = lambda wd: pl.BlockSpec((None, None, qb, wd), lambda b, i: (b, i, 0, 0))
    kspec = lambda wd: pl.BlockSpec((None, nk, ATT_BLOCK, wd), lambda b, i: (b, 0, 0, 0))
    _, ones2, first, second = _pair_consts(ATT_BLOCK)
    consts = (_tri_consts(ATT_BLOCK), ones2, first, second)
    out = pl.pallas_call(
        functools.partial(_dsa_kernel, topk=topk, qb=qb),
        grid=(bsz, nq),
        in_specs=[qspec(width), qspec(width), qspec(LANES), kspec(width), kspec(width), kspec(LANES)]
                 + [pl.BlockSpec(a.shape, lambda b, i, nd=a.ndim: (0,) * nd) for a in consts],
        out_specs=qspec(width),
        out_shape=jax.ShapeDtypeStruct((bsz, nq, qb, width), BF16),
        scratch_shapes=[pltpu.VMEM((nk, qb, ATT_BLOCK), F32),
                        pltpu.VMEM((qb, width), BF16),
                        pltpu.VMEM((N_HEADS // 2, qb, 2 * ATT_BLOCK), F32),
                        pltpu.VMEM((N_HEADS // 2, qb, 2 * LANES), F32)],
        compiler_params=_cparams(("parallel", "arbitrary")),
    )(qblk(q), qblk(qi), qblk(wq), kblk(k), kblk(v), kblk(kk), *consts)
    return out.reshape(bsz, seq, width)


def _sb_kernel(q_ref, k_ref, v_ref, tri_ref, ones_ref, first_ref, second_ref, o_ref, qs_s, rem_s, acc_s, *, qb):
    i = pl.program_id(1)
    kb_size = ATT_BLOCK
    n_diag = qb // kb_size
    n_pairs = N_HEADS // 2
    shape = (qb, 2 * kb_size)
    row = lax.broadcasted_iota(jnp.int32, shape, 0)
    col = lax.broadcasted_iota(jnp.int32, shape, 1) % kb_size
    qs_s[...] = q_ref[...] * (HEAD_DIM ** -0.5)
    rem_s[...] = jnp.ones_like(rem_s)
    acc_s[...] = jnp.zeros_like(acc_s)

    def step(kb, masked, r0=0):
        rows = slice(r0, qb)
        causal = ((kb * kb_size + col) < (i * qb + row))[rows]
        pairs = [slice(p * LANES, (p + 1) * LANES) for p in range(n_pairs)]
        log_sig, log_keep = [], []
        for ps in pairs:
            kbd = _pair_blockdiag(k_ref[kb, :, ps], first_ref, second_ref)
            z = _dot_nt(qs_s[rows, ps], kbd).astype(BF16)
            soft = jnp.log(1.0 + jnp.exp(-jnp.abs(z)))
            keep = -jnp.maximum(z, 0.0) - soft
            log_sig.append(jnp.minimum(z, 0.0) - soft)
            log_keep.append(jnp.where(causal, keep, jnp.zeros_like(keep)) if masked else keep)
        for p, ps in enumerate(pairs):
            after = _dot(log_keep[p], tri_ref[...]).astype(BF16)
            w = jnp.exp(log_sig[p] + after)
            if masked:
                w = jnp.where(causal, w, jnp.zeros_like(w))
            vbd = _pair_blockdiag(v_ref[kb, :, ps], first_ref, second_ref)
            out = _dot(w, jnp.concatenate([vbd, ones_ref[...]], axis=1))
            rem = rem_s[p, rows]
            acc_s[p, rows] += rem * out[:, :LANES]
            rem_s[p, rows] = rem * jnp.maximum(1.0 - out[:, LANES:], 0.0)

    for u in reversed(range(n_diag)):
        step(i * n_diag + u, True, u * kb_size)

    def past_step(j, carry):
        step(i * n_diag - 1 - j, False)
        return carry

    lax.fori_loop(0, i * n_diag, past_step, 0)
    for p in range(n_pairs):
        o_ref[:, p * LANES:(p + 1) * LANES] = acc_s[p].astype(o_ref.dtype)


def _stick_breaking(q, k, v, qb):
    bsz, seq, width = q.shape
    nq, nk = seq // qb, seq // ATT_BLOCK
    qspec = pl.BlockSpec((None, None, qb, width), lambda b, i: (b, i, 0, 0))
    kspec = pl.BlockSpec((None, nk, ATT_BLOCK, width), lambda b, i: (b, 0, 0, 0))
    consts = _pair_consts(ATT_BLOCK)
    out = pl.pallas_call(
        functools.partial(_sb_kernel, qb=qb),
        grid=(bsz, nq),
        in_specs=[qspec, kspec, kspec] + [pl.BlockSpec(a.shape, lambda b, i: (0, 0)) for a in consts],
        out_specs=qspec,
        out_shape=jax.ShapeDtypeStruct((bsz, nq, qb, width), BF16),
        scratch_shapes=[pltpu.VMEM((qb, width), BF16),
                        pltpu.VMEM((N_HEADS // 2, qb, LANES), F32),
                        pltpu.VMEM((N_HEADS // 2, qb, LANES), F32)],
        compiler_params=_cparams(("parallel", "arbitrary")),
    )(q.reshape(bsz, nq, qb, width), k.reshape(bsz, nk, ATT_BLOCK, width),
      v.reshape(bsz, nk, ATT_BLOCK, width), *consts)
    return out.reshape(bsz, seq, width)


def _cd_weight(cd_w_in):
    cw = N_HEADS * HEAD_DIM
    iw = IDX_HEADS * IDX_DIM
    offs = [0, cw, 2 * cw, 3 * cw, 3 * cw + iw, 3 * cw + iw + IDX_DIM, 3 * cw + iw + IDX_DIM + IDX_HEADS]
    qkv_c = cd_w_in[:, :offs[3]]
    q_idx = cd_w_in[:, offs[3]:offs[4]]
    k_idx = cd_w_in[:, offs[4]:offs[5]]
    w_idx = cd_w_in[:, offs[5]:offs[6]]
    qkv_d = cd_w_in[:, offs[6]:]
    pad = jnp.zeros((cd_w_in.shape[0], LANES - w_idx.shape[1]), cd_w_in.dtype)
    return jnp.concatenate([qkv_c, q_idx, qkv_d, k_idx, k_idx, w_idx, pad], axis=1)


def kernel(x, c, norm_g, ada_w, ada_b, mlp_w1, mlp_w2, ab_w_in, ab_w_out, pool_w, pool_scale, ssm_lam_re, ssm_lam_im, ssm_log_dt, ssm_b_re, ssm_b_im, ssm_c_re, ssm_c_im, ssm_d, glu_w, glu_b, cd_w_in, cd_w_out, final_g):
    bsz, seq, d = x.shape
    depth = norm_g.shape[0]
    topk = min(TOPK_MAX, seq // 4)
    tm = min(512, seq)

    mod = _modulation(c, ada_w.reshape(depth * 2, d, 3 * d), ada_b.reshape(depth * 2, 1, 3 * d))
    mod = mod[:, :, :, None, :]

    for i in range(depth):
        j = i // 2
        shift, scale1, gate1 = mod[2 * i, 0], mod[2 * i, 1], mod[2 * i, 2]
        g = norm_g[i, 0][None, :]
        if i % 2 == 0:
            width = ab_w_in.shape[2]
            tt = min(256, seq)
            (proj,) = _norm_proj(x, g, shift, scale1, ab_w_in[j].astype(BF16), (width,), (BF16,), tm)
            s5 = _s5_params(ssm_lam_re[j], ssm_lam_im[j], ssm_log_dt[j],
                            ssm_b_re[j], ssm_b_im[j], ssm_c_re[j], ssm_c_im[j], seg_len=tt // SUBLANES)
            y = _even_mixer(proj, pool_w[j].astype(BF16), pool_scale[j][None, :], s5,
                            ssm_d[j][None, :], glu_w[j].astype(BF16), glu_b[j][None, :],
                            tt=tt, scan_lanes=512)
            x = _out_proj(x, gate1, (y,), ab_w_out[j].astype(BF16), tm)
        else:
            hw = N_HEADS * HEAD_DIM
            widths = (hw,) * 7 + (LANES, LANES)
            dtypes = (BF16,) * 8 + (F32,)
            qc, kc, vc, qi, qd, kd, vd, kk, wq = _norm_proj(
                x, g, shift, scale1, _cd_weight(cd_w_in[j]).astype(BF16), widths, dtypes, tm)
            oc = _dsa(qc, kc, vc, qi, kk, wq, topk, qb=ATT_QUERY_BLOCK)
            od = _stick_breaking(qd, kd, vd, qb=ATT_QUERY_BLOCK)
            x = _out_proj(x, gate1, (oc, od), cd_w_out[j].astype(BF16), tm)
        shift, scale1, gate1 = mod[2 * i + 1, 0], mod[2 * i + 1, 1], mod[2 * i + 1, 2]
        x = _mlp(x, norm_g[i, 1][None, :], shift, scale1, gate1,
                 mlp_w1[i].astype(BF16), mlp_w2[i].astype(BF16), final_g[None, :],
                 final_norm=(i == depth - 1), tm=min(1024, seq), tf=1024)
    return x
```

```python
import functools
import math

import jax
import jax.numpy as jnp
from jax import lax
from jax.experimental import pallas as pl
from jax.experimental.pallas import tpu as pltpu

F32 = jnp.float32
BF16 = jnp.bfloat16

EPS = 1e-6
CHUNK = 64
POOL_WINDOWS = (2, 4, 8, 16)
POOL_CH = 128
SSM_GROUP = 16
SSM_STATE = 64
HEAD_DIM = 64
N_HEADS = 8
IDX_HEADS = 8
IDX_DIM = 64
IDX_SCALE = (IDX_HEADS ** -0.5) * (IDX_DIM ** -0.5)
TOPK_MAX = 256

LANES = 128
SUBLANES = 8
VMEM_LIMIT = 56 * 1024 * 1024

ATT_BLOCK = 128
ATT_QUERY_BLOCK = 512
NEG_BIG = -1e30
COUNT_ALL = 1e9
BISECT_ITERS = 28
POP_SPAN = 2.0
INTERP_SPAN = 256.0
MAX_POPS = 4096


def _cparams(sem):
    return pltpu.CompilerParams(dimension_semantics=sem, vmem_limit_bytes=VMEM_LIMIT)


def _dot(a, b):
    return jnp.dot(a, b, preferred_element_type=F32)


def _dot_nt(a, b):
    return lax.dot_general(a, b, (((1,), (1,)), ((), ())), preferred_element_type=F32)


def _rowmax(x):
    return jnp.max(x, axis=1, keepdims=True)


def _norm_modulate(x, g, shift, scale1):
    ms = jnp.mean(x * x, axis=-1, keepdims=True)
    return (x * lax.rsqrt(ms + EPS)) * g * scale1 + shift


def _mod_kernel(c_ref, w_ref, b_ref, o_ref):
    j = pl.program_id(1)
    m = _dot(c_ref[...], w_ref[...]) + b_ref[...]
    o_ref[...] = m + jnp.where(j > 0, 1.0, 0.0)


def _modulation(c_pad, ada_w, ada_b):
    n_sub, d, _ = ada_w.shape
    rows = c_pad.shape[0]
    return pl.pallas_call(
        _mod_kernel,
        grid=(n_sub, 3),
        in_specs=[
            pl.BlockSpec((rows, d), lambda s, j: (0, 0)),
            pl.BlockSpec((None, d, d), lambda s, j: (s, 0, j)),
            pl.BlockSpec((None, 1, d), lambda s, j: (s, 0, j)),
        ],
        out_specs=pl.BlockSpec((None, None, rows, d), lambda s, j: (s, j, 0, 0)),
        out_shape=jax.ShapeDtypeStruct((n_sub, 3, rows, d), F32),
        compiler_params=_cparams(("arbitrary", "arbitrary")),
    )(c_pad, ada_w, ada_b)


def _norm_proj_kernel(x_ref, g_ref, shift_ref, scale_ref, w_ref, *o_refs, widths):
    h = _norm_modulate(x_ref[...], g_ref[...], shift_ref[...], scale_ref[...]).astype(BF16)
    y = _dot(h, w_ref[...])
    off = 0
    for o_ref, width in zip(o_refs, widths):
        o_ref[...] = y[:, off:off + width].astype(o_ref.dtype)
        off += width


def _norm_proj(x, g, shift, scale1, w, widths, dtypes, tm):
    bsz, seq, d = x.shape
    n = w.shape[1]
    vec = pl.BlockSpec((None, 1, d), lambda b, i: (b, 0, 0))
    return pl.pallas_call(
        functools.partial(_norm_proj_kernel, widths=widths),
        grid=(bsz, seq // tm),
        in_specs=[
            pl.BlockSpec((None, tm, d), lambda b, i: (b, i, 0)),
            pl.BlockSpec((1, d), lambda b, i: (0, 0)),
            vec, vec,
            pl.BlockSpec((d, n), lambda b, i: (0, 0)),
        ],
        out_specs=[pl.BlockSpec((None, tm, wd), lambda b, i: (b, i, 0)) for wd in widths],
        out_shape=[jax.ShapeDtypeStruct((bsz, seq, wd), dt) for wd, dt in zip(widths, dtypes)],
        compiler_params=_cparams(("parallel", "parallel")),
    )(x, g, shift, scale1, w)


def _out_mlp_kernel(x_ref, gate_mix_ref, *refs, n_y, final_norm):
    y_refs = refs[:n_y]
    w_out_ref, g_ref, shift_ref, scale_ref, gate_ref, w1_ref, w2_ref, fg_ref, o_ref, h_s, acc_s = refs[n_y:]
    f = pl.program_id(2)

    @pl.when(f == 0)
    def _():
        y = jnp.concatenate([r[...] for r in y_refs], axis=1) if n_y > 1 else y_refs[0][...]
        x1 = x_ref[...] + gate_mix_ref[...] * _dot(y, w_out_ref[...])
        o_ref[...] = x1
        h_s[...] = _norm_modulate(x1, g_ref[...], shift_ref[...], scale_ref[...]).astype(BF16)
        acc_s[...] = jnp.zeros_like(acc_s)

    a = jnp.maximum(_dot(h_s[...], w1_ref[...]), 0.0)
    acc_s[...] += _dot((a * a).astype(BF16), w2_ref[...])

    @pl.when(f == pl.num_programs(2) - 1)
    def _():
        y = o_ref[...] + gate_ref[...] * acc_s[...]
        if final_norm:
            ms = jnp.mean(y * y, axis=-1, keepdims=True)
            y = (y * lax.rsqrt(ms + EPS)) * fg_ref[...]
        o_ref[...] = y


def _out_mlp(x, gate_mix, ys, w_out, g, shift, scale1, gate1, w1, w2, final_g, final_norm, tm, tf):
    bsz, seq, d = x.shape
    dff = w1.shape[1]
    row = lambda wd: pl.BlockSpec((None, tm, wd), lambda b, i, f: (b, i, 0))
    vec = pl.BlockSpec((None, 1, d), lambda b, i, f: (b, 0, 0))
    one = pl.BlockSpec((1, d), lambda b, i, f: (0, 0))
    return pl.pallas_call(
        functools.partial(_out_mlp_kernel, n_y=len(ys), final_norm=final_norm),
        grid=(bsz, seq // tm, dff // tf),
        in_specs=[
            row(d), vec, *[row(y.shape[2]) for y in ys],
            pl.BlockSpec(w_out.shape, lambda b, i, f: (0, 0)),
            one, vec, vec, vec,
            pl.BlockSpec((d, tf), lambda b, i, f: (0, f)),
            pl.BlockSpec((tf, d), lambda b, i, f: (f, 0)),
            one,
        ],
        out_specs=row(d),
        out_shape=jax.ShapeDtypeStruct((bsz, seq, d), F32),
        scratch_shapes=[pltpu.VMEM((tm, d), BF16), pltpu.VMEM((tm, d), F32)],
        compiler_params=_cparams(("parallel", "parallel", "arbitrary")),
    )(x, gate_mix, *ys, w_out, g, shift, scale1, gate1, w1, w2, final_g)


def _cmul(ar, ai, br, bi):
    return ar * br - ai * bi, ar * bi + ai * br


def _s5_params_kernel(lr_ref, li_ref, ldt_ref, brt_ref, bit_ref, crt_ref, cit_ref,
                      wb_ref, wc_ref, a_re_ref, a_im_ref, pk_re_ref, pk_im_ref, sg_re_ref, sg_im_ref):
    lr, li = lr_ref[...], li_ref[...]
    dt = jnp.exp(ldt_ref[...])
    mag = jnp.exp(lr * dt)
    ar, ai = mag * jnp.cos(li * dt), mag * jnp.sin(li * dt)
    den = lr * lr + li * li
    zr = ((ar - 1.0) * lr + ai * li) / den
    zi = (ai * lr - (ar - 1.0) * li) / den

    rows, cols = brt_ref.shape
    rgrp = lax.broadcasted_iota(jnp.int32, (rows, cols), 0) // SSM_GROUP
    cgrp = lax.broadcasted_iota(jnp.int32, (rows, cols), 1) // SSM_STATE
    diag = rgrp == cgrp
    br, bi = brt_ref[...], bit_ref[...]
    wb_ref[0] = jnp.where(diag, zr * br - zi * bi, 0.0).astype(wb_ref.dtype)
    wb_ref[1] = jnp.where(diag, zr * bi + zi * br, 0.0).astype(wb_ref.dtype)
    wc_ref[0] = jnp.where(diag, crt_ref[...], 0.0).astype(wc_ref.dtype)
    wc_ref[1] = jnp.where(diag, cit_ref[...], 0.0).astype(wc_ref.dtype)

    n = lr.shape[1]
    rows8 = pk_re_ref.shape[0]
    seg_len = rows8 // SUBLANES
    a_re_ref[...] = jnp.broadcast_to(ar, (SUBLANES, n))
    a_im_ref[...] = jnp.broadcast_to(ai, (SUBLANES, n))
    expo = lax.broadcasted_iota(jnp.int32, (rows8, n), 0) // SUBLANES + 1
    sq_r, sq_i = jnp.broadcast_to(ar, (rows8, n)), jnp.broadcast_to(ai, (rows8, n))
    pr, pi = jnp.ones((rows8, n), F32), jnp.zeros((rows8, n), F32)
    for b in range(seg_len.bit_length()):
        nr, ni = _cmul(pr, pi, sq_r, sq_i)
        take = ((expo >> b) & 1) == 1
        pr, pi = jnp.where(take, nr, pr), jnp.where(take, ni, pi)
        sq_r, sq_i = _cmul(sq_r, sq_i, sq_r, sq_i)
    pk_re_ref[...] = pr
    pk_im_ref[...] = pi
    row = lax.broadcasted_iota(jnp.int32, (SUBLANES, n), 0)
    gr, gi = pr[rows8 - SUBLANES:], pi[rows8 - SUBLANES:]
    for k in range(3):
        keep = row >= (1 << k)
        sg_re_ref[k] = jnp.where(keep, gr, 0.0)
        sg_im_ref[k] = jnp.where(keep, gi, 0.0)
        gr, gi = _cmul(gr, gi, gr, gi)


def _s5_params(lam_re, lam_im, log_dt, b_re, b_im, c_re, c_im, seg_len):
    g, p = lam_re.shape
    c = b_re.shape[2]
    n = g * p
    flat = lambda a: a.reshape(1, n)
    ldt = jnp.broadcast_to(log_dt[:, None], (g, p)).reshape(1, n)
    b_t = lambda a: jnp.tile(jnp.transpose(a, (2, 0, 1)).reshape(c, n), (g, 1))
    c_t = lambda a: jnp.tile(a.reshape(g * c, p), (1, g))
    outs = pl.pallas_call(
        _s5_params_kernel,
        out_shape=[jax.ShapeDtypeStruct((2, g * c, n), BF16), jax.ShapeDtypeStruct((2, g * c, n), BF16),
                   jax.ShapeDtypeStruct((SUBLANES, n), F32), jax.ShapeDtypeStruct((SUBLANES, n), F32),
                   jax.ShapeDtypeStruct((seg_len * SUBLANES, n), F32), jax.ShapeDtypeStruct((seg_len * SUBLANES, n), F32),
                   jax.ShapeDtypeStruct((3, SUBLANES, n), F32), jax.ShapeDtypeStruct((3, SUBLANES, n), F32)],
        compiler_params=pltpu.CompilerParams(vmem_limit_bytes=VMEM_LIMIT),
    )(flat(lam_re), flat(lam_im), ldt, b_t(b_re), b_t(b_im), c_t(c_re), c_t(c_im))
    wb, wct, a_re, a_im, pk_re, pk_im, sg_re, sg_im = outs
    wc = jnp.transpose(wct, (0, 2, 1))
    pk_re, pk_im = (a.reshape(seg_len, SUBLANES, n) for a in (pk_re, pk_im))
    return wb, wc, a_re, a_im, pk_re, pk_im, sg_re, sg_im


def _gelu_tanh(x):
    return 0.5 * x * (1.0 + jnp.tanh(math.sqrt(2.0 / math.pi) * (x + 0.044715 * (x * x * x))))


def _even_kernel(p_ref, band_ref, perm_ref, wpool_ref, pscale_ref, wb_ref, wc_ref, dskip_ref, wglu_ref, bglu_ref,
                 a_re_ref, a_im_ref, pk_re_ref, pk_im_ref, sg_re_ref, sg_im_ref,
                 o_ref, prev_s, xr_s, xi_s, car_re, car_im, *, mix_a, scan_lanes):
    i = pl.program_id(1)
    tt = p_ref.shape[0]

    @pl.when(i == 0)
    def _():
        prev_s[...] = jnp.zeros_like(prev_s)
        car_re[...] = jnp.zeros_like(car_re)
        car_im[...] = jnp.zeros_like(car_im)

    ua = p_ref[:, :mix_a]
    ext = jnp.concatenate([prev_s[...], ua], axis=0)
    t1 = i * tt + lax.broadcasted_iota(jnp.int32, (tt, POOL_CH), 0) + 1
    for g, win in enumerate(POOL_WINDOWS):
        sl = slice(g * POOL_CH, (g + 1) * POOL_CH)
        wsum = _dot(band_ref[g], ext[:, sl])
        cnt = jnp.minimum(t1, win).astype(F32)
        pg = wsum / cnt - ua[:, sl].astype(F32)
        yg = _dot(pg.astype(BF16), wpool_ref[g]) * pscale_ref[:, sl]
        o_ref[:, sl] = yg.astype(o_ref.dtype)
    prev_s[...] = ua

    ub = _dot(perm_ref[0], p_ref[:, mix_a:]).astype(BF16)
    n_tiles = xr_s.shape[0]
    n_diag_blocks = ub.shape[1] // LANES
    tpb = n_tiles // n_diag_blocks
    sw = tpb * LANES
    for j in range(n_diag_blocks):
        cs, ss = slice(j * LANES, (j + 1) * LANES), slice(j * sw, (j + 1) * sw)
        xr, xi = _dot(ub[:, cs], wb_ref[0, cs, ss]), _dot(ub[:, cs], wb_ref[1, cs, ss])
        for q in range(tpb):
            xr_s[j * tpb + q] = xr[:, q * LANES:(q + 1) * LANES]
            xi_s[j * tpb + q] = xi[:, q * LANES:(q + 1) * LANES]

    seg_len = tt // SUBLANES
    seg_rows = lambda k: pl.ds(pl.multiple_of(k * SUBLANES, SUBLANES), SUBLANES)
    first_seg = lax.broadcasted_iota(jnp.int32, (SUBLANES, LANES), 0) == 0
    zeros = jnp.zeros((SUBLANES, LANES), F32)
    tpc = scan_lanes // LANES
    for c in range(n_tiles // tpc):
        tiles = list(range(c * tpc, (c + 1) * tpc))
        lanes = [slice(t * LANES, (t + 1) * LANES) for t in tiles]

        def local(k, carry, tiles=tiles, lanes=lanes):
            out = []
            for t, ls, (hr, hi) in zip(tiles, lanes, carry):
                dr, di = _cmul(a_re_ref[:, ls], a_im_ref[:, ls], hr, hi)
                hr, hi = dr + xr_s[t, seg_rows(k), :], di + xi_s[t, seg_rows(k), :]
                xr_s[t, seg_rows(k), :] = hr
                xi_s[t, seg_rows(k), :] = hi
                out.append((hr, hi))
            return tuple(out)

        ends = lax.fori_loop(0, seg_len, local, tuple((zeros, zeros) for _ in tiles))
        starts = []
        for t, ls, (er, ei) in zip(tiles, lanes, ends):
            fr = jnp.where(first_seg, car_re[:, ls], pltpu.roll(er, 1, 0))
            fi = jnp.where(first_seg, car_im[:, ls], pltpu.roll(ei, 1, 0))
            for k in range(3):
                dr, di = _cmul(sg_re_ref[k, :, ls], sg_im_ref[k, :, ls],
                               pltpu.roll(fr, 1 << k, 0), pltpu.roll(fi, 1 << k, 0))
                fr, fi = fr + dr, fi + di
            starts.append((fr, fi))
            dr, di = _cmul(pk_re_ref[seg_len - 1, :, ls], pk_im_ref[seg_len - 1, :, ls], fr, fi)
            car_re[:, ls] = jnp.broadcast_to((dr + er)[SUBLANES - 1:], fr.shape)
            car_im[:, ls] = jnp.broadcast_to((di + ei)[SUBLANES - 1:], fr.shape)

        def fix(k, carry, tiles=tiles, lanes=lanes, starts=starts):
            for t, ls, (fr, fi) in zip(tiles, lanes, starts):
                dr, di = _cmul(pk_re_ref[k, :, ls], pk_im_ref[k, :, ls], fr, fi)
                xr_s[t, seg_rows(k), :] += dr
                xi_s[t, seg_rows(k), :] += di
            return carry

        lax.fori_loop(0, seg_len, fix, 0)

    state = lambda ref, j: jnp.concatenate([ref[j * tpb + q] for q in range(tpb)], axis=1).astype(BF16)
    y = jnp.concatenate(
        [_dot(state(xr_s, j), wc_ref[0, j * sw:(j + 1) * sw, j * LANES:(j + 1) * LANES])
         - _dot(state(xi_s, j), wc_ref[1, j * sw:(j + 1) * sw, j * LANES:(j + 1) * LANES])
         for j in range(n_diag_blocks)], axis=1)
    y = _gelu_tanh(y + dskip_ref[...] * ub.astype(F32))
    z = _dot(y.astype(BF16), wglu_ref[...]) + bglu_ref[...]
    out = (y * (1.0 / (1.0 + jnp.exp(-z)))).astype(o_ref.dtype)
    o_ref[:, mix_a:] = _dot(perm_ref[1], out).astype(o_ref.dtype)


def _segment_perm(tt):
    seg_len = tt // SUBLANES
    r = jnp.arange(tt)
    src = (r % SUBLANES) * seg_len + r // SUBLANES
    p = (src[:, None] == jnp.arange(tt)[None, :])
    return jnp.stack([p, p.T]).astype(BF16)


def _pool_band(tt):
    t = jnp.arange(tt)[:, None] + tt
    j = jnp.arange(2 * tt)[None, :]
    return jnp.stack([((j <= t) & (j > t - w)) for w in POOL_WINDOWS]).astype(BF16)


def _even_mixer(proj, wpool, pscale, s5, dskip, wglu, bglu, tt, scan_lanes):
    bsz, seq, width = proj.shape
    wb, wc, a_re, a_im, pk_re, pk_im, sg_re, sg_im = s5
    mix_a = len(POOL_WINDOWS) * POOL_CH
    mix_b = width - mix_a
    n_state = wb.shape[2]
    const = lambda a: pl.BlockSpec(a.shape, lambda b, i, nd=a.ndim: (0,) * nd)
    band = _pool_band(tt)
    args = (band, _segment_perm(tt), wpool, pscale, wb, wc, dskip, wglu, bglu, a_re, a_im, pk_re, pk_im, sg_re, sg_im)
    return pl.pallas_call(
        functools.partial(_even_kernel, mix_a=mix_a, scan_lanes=scan_lanes),
        grid=(bsz, seq // tt),
        in_specs=[pl.BlockSpec((None, tt, width), lambda b, i: (b, i, 0))] + [const(a) for a in args],
        out_specs=pl.BlockSpec((None, tt, width), lambda b, i: (b, i, 0)),
        out_shape=jax.ShapeDtypeStruct((bsz, seq, width), BF16),
        scratch_shapes=[pltpu.VMEM((tt, mix_a), BF16),
                        pltpu.VMEM((n_state // LANES, tt, LANES), F32), pltpu.VMEM((n_state // LANES, tt, LANES), F32),
                        pltpu.VMEM((SUBLANES, n_state), F32), pltpu.VMEM((SUBLANES, n_state), F32)],
        compiler_params=_cparams(("parallel", "arbitrary")),
    )(proj, *args)


def _pair_blockdiag(x, first_ref, second_ref):
    return jnp.concatenate([x * first_ref[...], x * second_ref[...]], axis=0)


def _dsa_kernel(q_ref, qi_ref, wq_ref, k_ref, v_ref, kk_ref, tri_ref, ones_ref, first_ref, second_ref, alibi_ref,
                o_ref,
                s_s, qs_s, mx_s, al_s, *, topk, qb):
    i = pl.program_id(1)
    kb_size = ATT_BLOCK
    n_diag = qb // kb_size
    nkb = (i + 1) * n_diag
    shape = (qb, kb_size)
    row = lax.broadcasted_iota(jnp.int32, shape, 0)
    col = lax.broadcasted_iota(jnp.int32, shape, 1)
    q_chunk = (i * qb + row) // CHUNK
    bcast = lambda a: jnp.broadcast_to(a, shape)
    kf = float(topk)
    pairs = [slice(p * LANES, (p + 1) * LANES) for p in range(N_HEADS // 2)]

    wq = wq_ref[...]
    w_cols = [bcast(wq[:, h:h + 1] * IDX_SCALE) for h in range(IDX_HEADS)]

    def indexer(kb, rows):
        kbd = _pair_blockdiag(kk_ref[kb], first_ref, second_ref)
        acc = jnp.zeros((rows.stop - rows.start, kb_size), F32)
        for p, ps in enumerate(pairs):
            rel = jnp.maximum(_dot_nt(qi_ref[rows, ps], kbd), 0.0)
            acc = acc + rel[:, :kb_size] * w_cols[2 * p][rows] + rel[:, kb_size:] * w_cols[2 * p + 1][rows]
        return acc

    def past_block(kb, carry):
        rmax, rmin = carry
        acc = indexer(kb, slice(0, qb))
        s_s[kb] = acc
        return jnp.maximum(rmax, acc), jnp.minimum(rmin, acc)

    rmax, rmin = lax.fori_loop(0, i * n_diag, past_block,
                               (jnp.full(shape, -jnp.inf, F32), jnp.full(shape, jnp.inf, F32)))
    for u in range(n_diag):
        kb, r0 = i * n_diag + u, u * kb_size
        rows = slice(r0, qb)
        acc = indexer(kb, rows)
        adm = (((kb * kb_size + col) // CHUNK) <= q_chunk)[rows]
        s_s[kb, rows] = jnp.where(adm, acc, -jnp.inf)
        hi_part = jnp.maximum(rmax[rows], jnp.where(adm, acc, -jnp.inf))
        lo_part = jnp.minimum(rmin[rows], jnp.where(adm, acc, jnp.inf))
        if r0:
            s_s[kb, :r0] = jnp.full((r0, kb_size), -jnp.inf, F32)
            hi_part = jnp.concatenate([rmax[:r0], hi_part], axis=0)
            lo_part = jnp.concatenate([rmin[:r0], lo_part], axis=0)
        rmax, rmin = hi_part, lo_part
    n_chunks = qb // kb_size
    chunk_rows = lambda c: slice(c * kb_size, (c + 1) * kb_size)
    chunk_subl = lambda c: slice(c * SUBLANES, (c + 1) * SUBLANES)
    ones8 = jnp.ones((SUBLANES, kb_size), BF16)
    tile = (kb_size, kb_size)

    def to_dense(rep):
        return rep.T[:SUBLANES]

    def to_rows(dense):
        return jnp.broadcast_to(dense[:1], tile).T

    def dense_of(fn):
        return jnp.concatenate([fn(c) for c in range(n_chunks)], axis=0)

    def fold_keys(c, fn, init):
        def body(g, a):
            for u in range(n_diag):
                a = fn(a, s_s[g * n_diag + u, chunk_rows(c), :])
            return a
        return lax.fori_loop(0, i + 1, body, init)

    def count(x, cmp):
        xbs = [to_rows(x[chunk_subl(c)]) for c in range(n_chunks)]
        cnts = [fold_keys(c, lambda a, s, xb=xbs[c]: a + jnp.where(cmp(s, xb), 1.0, 0.0), jnp.zeros(tile, F32))
                for c in range(n_chunks)]
        return jnp.concatenate([_dot_nt(ones8, cnt.astype(BF16)) for cnt in cnts], axis=0)

    count_ge = functools.partial(count, cmp=lambda s, xb: s >= xb)
    count_gt = functools.partial(count, cmp=lambda s, xb: s > xb)
    count_eq = functools.partial(count, cmp=lambda s, xb: s == xb)

    smax = dense_of(lambda c: to_dense(jnp.broadcast_to(_rowmax(rmax[chunk_rows(c)]), tile)))
    smin = dense_of(lambda c: to_dense(jnp.broadcast_to(jnp.min(rmin[chunk_rows(c)], axis=1, keepdims=True), tile)))
    dshape = (n_chunks * SUBLANES, kb_size)
    t_dense = (i * qb + (lax.broadcasted_iota(jnp.int32, dshape, 0) // SUBLANES) * kb_size
               + lax.broadcasted_iota(jnp.int32, dshape, 1))
    n_adm = ((t_dense // CHUNK + 1) * CHUNK).astype(F32)
    hi0 = smax + jnp.maximum(jnp.abs(smax) * 1e-6, 1e-30)
    zero = jnp.zeros_like(smin)
    cge0, cgt0 = count_ge(zero), count_gt(zero)
    wide = n_adm > kf
    zero_tie = jnp.logical_and(wide, jnp.logical_and(cgt0 < kf, cge0 >= kf))
    above = jnp.logical_and(wide, cgt0 >= kf)
    below = jnp.logical_and(wide, cge0 < kf)
    lo0 = jnp.where(jnp.logical_or(above, zero_tie), 0.0, smin)
    clo0 = jnp.where(zero_tie, kf, jnp.where(above, cge0, n_adm))
    hi0 = jnp.where(below, 0.0, hi0)
    chi0 = jnp.where(below, cge0, 0.0)
    need0 = jnp.where(zero_tie, kf - cgt0, COUNT_ALL)

    def unresolved(clo, chi):
        return jnp.max(jnp.where(jnp.logical_and(clo > kf, clo - chi > POP_SPAN), 1.0, 0.0))

    def bisect_cond(carry):
        return jnp.logical_and(carry[0] > 0.0, carry[1] < BISECT_ITERS)

    def bisect(carry):
        _, it, lo, hi, clo, chi = carry
        span = clo - chi
        aim = (clo - (kf + 0.5)) / span
        interpolate = jnp.logical_and(span < INTERP_SPAN, it % 2 == 0)
        mid = lo + jnp.where(interpolate, aim, 0.5) * (hi - lo)
        c = count_ge(mid)
        active = clo > kf
        up = jnp.logical_and(active, c >= kf)
        down = jnp.logical_and(active, c < kf)
        clo = jnp.where(up, c, clo)
        chi = jnp.where(down, c, chi)
        return (unresolved(clo, chi), it + 1, jnp.where(up, mid, lo), jnp.where(down, mid, hi), clo, chi)

    _, _, lo, hi, clo, chi = lax.while_loop(
        bisect_cond, bisect, (unresolved(clo0, chi0), jnp.int32(0), lo0, hi0, clo0, chi0))

    done0 = jnp.where(clo <= kf, 1.0, 0.0)

    def pop_cond(carry):
        return jnp.logical_and(carry[0] > 0.0, carry[1] < MAX_POPS)

    def next_below(hi):
        def one(c):
            hib = to_rows(hi[chunk_subl(c)])
            best = fold_keys(c, lambda a, s: jnp.maximum(a, jnp.where(s < hib, s, -jnp.inf)),
                             jnp.full(tile, -jnp.inf, F32))
            return to_dense(jnp.broadcast_to(_rowmax(best), tile))
        return dense_of(one)

    def pop(carry):
        _, it, hi, chi, thr, need, done = carry
        v = next_below(hi)
        cv = count_eq(v)
        active = done < 0.5
        fin = jnp.logical_and(active, chi + cv >= kf)
        go = jnp.logical_and(active, chi + cv < kf)
        thr = jnp.where(fin, v, thr)
        need = jnp.where(fin, kf - chi, need)
        done = jnp.where(fin, 1.0, done)
        hi = jnp.where(go, v, hi)
        chi = jnp.where(go, chi + cv, chi)
        return (jnp.max(1.0 - done), it + 1, hi, chi, thr, need, done)

    left0 = jnp.max(1.0 - done0)
    _, _, _, _, thr, need, _ = lax.while_loop(
        pop_cond, pop, (left0, jnp.int32(0), hi, chi, lo, need0, done0))
    thrb = jnp.concatenate([to_rows(thr[chunk_subl(c)]) for c in range(n_chunks)], axis=0)
    needb = jnp.concatenate([to_rows(need[chunk_subl(c)]) for c in range(n_chunks)], axis=0)
    partial_ties = jnp.max(jnp.where(need < COUNT_ALL, 1.0, 0.0))

    @pl.when(partial_ties > 0.0)
    def _():
        def tie_block(kb, run):
            s = s_s[kb]
            eq = jnp.where(s == thrb, 1.0, 0.0).astype(BF16)
            rank = _dot(eq, tri_ref[1]) + run
            take = jnp.where(s == thrb, jnp.where(rank < needb, 0.0, NEG_BIG), NEG_BIG)
            s_s[kb] = jnp.where(s > thrb, 0.0, take)
            return run + _dot(eq, tri_ref[2])
        lax.fori_loop(0, nkb, tie_block, jnp.zeros(shape, F32))

    @pl.when(partial_ties <= 0.0)
    def _():
        def sel_block(kb, carry):
            s_s[kb] = jnp.where(s_s[kb] >= thrb, 0.0, NEG_BIG)
            return carry
        lax.fori_loop(0, nkb, sel_block, 0)

    first_half = col < HEAD_DIM
    qs_s[...] = q_ref[...] * (HEAD_DIM ** -0.5)
    mx_s[...] = jnp.full(mx_s.shape, NEG_BIG, F32)
    al_s[...] = jnp.zeros_like(al_s)
    slopes = [2.0 ** (-8.0 * (h + 1) / N_HEADS) for h in range(N_HEADS)]

    q_alibi = jnp.where(col < 4, 1.0, 0.0).astype(BF16)

    def attend_group(g, past):
        blocks = [g * n_diag + u for u in range(n_diag)]
        if not past:
            t_pos = (i * qb + row).astype(F32)
            hats = []
            for kb in blocks:
                s_pos = (kb * kb_size + col).astype(F32)
                hats.append(jnp.minimum(s_pos, 2.0 * t_pos - s_pos))
        for p, ps in enumerate(pairs):
            tiles = []
            for u, kb in enumerate(blocks):
                mask = s_s[kb]
                kbd = _pair_blockdiag(k_ref[kb, :, ps], first_ref, second_ref)
                if past:
                    k_alibi = alibi_ref[0, p] + alibi_ref[1, p] * kb.astype(BF16)
                    logits = _dot_nt(jnp.concatenate([qs_s[:, ps], q_alibi], axis=1),
                                     jnp.concatenate([kbd, k_alibi], axis=1))
                    tiles.append(logits + jnp.concatenate([mask, mask], axis=1))
                else:
                    bias = jnp.concatenate([mask + slopes[2 * p] * hats[u], mask + slopes[2 * p + 1] * hats[u]], axis=1)
                    tiles.append(_dot_nt(qs_s[:, ps], kbd) + bias)
            top = functools.reduce(jnp.maximum, tiles)
            m_old = mx_s[p]
            m_new = jnp.maximum(m_old, jnp.concatenate(
                [bcast(_rowmax(top[:, :kb_size])), bcast(_rowmax(top[:, kb_size:]))], axis=1))
            mx_s[p] = m_new
            pexp = jnp.concatenate([jnp.exp((t - m_new).astype(BF16)) for t in tiles], axis=1)
            shrink = m_old - m_new
            alpha = jnp.exp(jnp.where(first_half, shrink[:, :kb_size], shrink[:, kb_size:]))
            values = jnp.concatenate(
                [jnp.concatenate([_pair_blockdiag(v_ref[kb, :, ps], first_ref, second_ref), ones_ref[...]], axis=1)
                 for kb in blocks], axis=0)
            al_s[p] = jnp.concatenate([alpha, alpha], axis=1) * al_s[p] + _dot(pexp, values)

    def past_group(g, carry):
        attend_group(g, True)
        return carry

    lax.fori_loop(0, i, past_group, 0)
    attend_group(i, False)
    for p, ps in enumerate(pairs):
        state = al_s[p]
        o_ref[:, ps] = (state[:, :LANES] / state[:, LANES:]).astype(o_ref.dtype)


def _tri_consts(n):
    j = jnp.arange(n)[:, None]
    s = jnp.arange(n)[None, :]
    return jnp.stack([j > s, j < s, jnp.ones((n, n), bool)]).astype(BF16)


def _pair_consts(n):
    tri = _tri_consts(n)
    zero = jnp.zeros((n, n), BF16)
    tri2 = jnp.block([[tri[0], zero], [zero, tri[0]]])
    first = (jnp.arange(n)[None, :] < n // 2) & jnp.ones((n, 1), bool)
    ones2 = jnp.concatenate([first, ~first], axis=0).astype(BF16)
    return tri2, ones2, first.astype(BF16), (~first).astype(BF16)


def _alibi_key_terms(n):
    r = jnp.arange(2 * n)[:, None]
    lane = jnp.arange(LANES)[None, :]
    head_b = r >= n
    out = []
    for p in range(N_HEADS // 2):
        slope = jnp.where(head_b, 2.0 ** (-8.0 * (2 * p + 2) / N_HEADS), 2.0 ** (-8.0 * (2 * p + 1) / N_HEADS))
        in_block = jnp.where(lane == jnp.where(head_b, 3, 1), slope * (r % n), 0.0)
        per_block = jnp.where(lane == jnp.where(head_b, 2, 0), slope * n, 0.0)
        out.append(jnp.stack([in_block, per_block]))
    return jnp.stack(out, axis=1).astype(BF16)


def _dsa(q, k, v, qi, kk, wq, topk, qb):
    bsz, seq, width = q.shape
    nq, nk = seq // qb, seq // ATT_BLOCK
    qblk = lambda a: a.reshape(bsz, nq, qb, a.shape[-1])
    kblk = lambda a: a.reshape(bsz, nk, ATT_BLOCK, a.shape[-1])
    qspec = lambda wd: pl.BlockSpec((None, None, qb, wd), lambda b, i: (b, i, 0, 0))
    kspec = lambda wd: pl.BlockSpec((None, nk, ATT_BLOCK, wd), lambda b, i: (b, 0, 0, 0))
    _, ones2, first, second = _pair_consts(ATT_BLOCK)
    consts = (_tri_consts(ATT_BLOCK), ones2, first, second, _alibi_key_terms(ATT_BLOCK))
    out = pl.pallas_call(
        functools.partial(_dsa_kernel, topk=topk, qb=qb),
        grid=(bsz, nq),
        in_specs=[qspec(width), qspec(width), qspec(LANES), kspec(width), kspec(width), kspec(LANES)]
                 + [pl.BlockSpec(a.shape, lambda b, i, nd=a.ndim: (0,) * nd) for a in consts],
        out_specs=qspec(width),
        out_shape=jax.ShapeDtypeStruct((bsz, nq, qb, width), BF16),
        scratch_shapes=[pltpu.VMEM((nk, qb, ATT_BLOCK), F32),
                        pltpu.VMEM((qb, width), BF16),
                        pltpu.VMEM((N_HEADS // 2, qb, 2 * ATT_BLOCK), F32),
                        pltpu.VMEM((N_HEADS // 2, qb, 2 * LANES), F32)],
        compiler_params=_cparams(("parallel", "arbitrary")),
    )(qblk(q), qblk(qi), qblk(wq), kblk(k), kblk(v), kblk(kk), *consts)
    return out.reshape(bsz, seq, width)


def _sb_kernel(q_ref, k_ref, v_ref, tri_ref, ones_ref, first_ref, second_ref, o_ref, qs_s, rem_s, acc_s, *, qb):
    i = pl.program_id(1)
    kb_size = ATT_BLOCK
    n_diag = qb // kb_size
    n_pairs = N_HEADS // 2
    shape = (qb, 2 * kb_size)
    row = lax.broadcasted_iota(jnp.int32, shape, 0)
    col = lax.broadcasted_iota(jnp.int32, shape, 1) % kb_size
    qs_s[...] = q_ref[...] * (HEAD_DIM ** -0.5)
    rem_s[...] = jnp.ones_like(rem_s)
    acc_s[...] = jnp.zeros_like(acc_s)

    def step(kb, masked, r0=0):
        rows = slice(r0, qb)
        causal = ((kb * kb_size + col) < (i * qb + row))[rows]
        pairs = [slice(p * LANES, (p + 1) * LANES) for p in range(n_pairs)]
        log_sig, log_keep = [], []
        for ps in pairs:
            kbd = _pair_blockdiag(k_ref[kb, :, ps], first_ref, second_ref)
            z = _dot_nt(qs_s[rows, ps], kbd).astype(BF16)
            soft = jnp.log(1.0 + jnp.exp(-jnp.abs(z)))
            keep = -jnp.maximum(z, 0.0) - soft
            log_sig.append(jnp.minimum(z, 0.0) - soft)
            log_keep.append(jnp.where(causal, keep, jnp.zeros_like(keep)) if masked else keep)
        for p, ps in enumerate(pairs):
            after = _dot(log_keep[p], tri_ref[...]).astype(BF16)
            w = jnp.exp(log_sig[p] + after)
            if masked:
                w = jnp.where(causal, w, jnp.zeros_like(w))
            vbd = _pair_blockdiag(v_ref[kb, :, ps], first_ref, second_ref)
            out = _dot(w, jnp.concatenate([vbd, ones_ref[...]], axis=1))
            rem = rem_s[p, rows]
            acc_s[p, rows] += rem * out[:, :LANES]
            rem_s[p, rows] = rem * jnp.maximum(1.0 - out[:, LANES:], 0.0)

    for u in reversed(range(n_diag)):
        step(i * n_diag + u, True, u * kb_size)

    def past_step(j, carry):
        step(i * n_diag - 1 - j, False)
        return carry

    lax.fori_loop(0, i * n_diag, past_step, 0)
    for p in range(n_pairs):
        o_ref[:, p * LANES:(p + 1) * LANES] = acc_s[p].astype(o_ref.dtype)


def _stick_breaking(q, k, v, qb):
    bsz, seq, width = q.shape
    nq, nk = seq // qb, seq // ATT_BLOCK
    qspec = pl.BlockSpec((None, None, qb, width), lambda b, i: (b, i, 0, 0))
    kspec = pl.BlockSpec((None, nk, ATT_BLOCK, width), lambda b, i: (b, 0, 0, 0))
    consts = _pair_consts(ATT_BLOCK)
    out = pl.pallas_call(
        functools.partial(_sb_kernel, qb=qb),
        grid=(bsz, nq),
        in_specs=[qspec, kspec, kspec] + [pl.BlockSpec(a.shape, lambda b, i: (0, 0)) for a in consts],
        out_specs=qspec,
        out_shape=jax.ShapeDtypeStruct((bsz, nq, qb, width), BF16),
        scratch_shapes=[pltpu.VMEM((qb, width), BF16),
                        pltpu.VMEM((N_HEADS // 2, qb, LANES), F32),
                        pltpu.VMEM((N_HEADS // 2, qb, LANES), F32)],
        compiler_params=_cparams(("parallel", "arbitrary")),
    )(q.reshape(bsz, nq, qb, width), k.reshape(bsz, nk, ATT_BLOCK, width),
      v.reshape(bsz, nk, ATT_BLOCK, width), *consts)
    return out.reshape(bsz, seq, width)


def _cd_weight(cd_w_in):
    cw = N_HEADS * HEAD_DIM
    iw = IDX_HEADS * IDX_DIM
    offs = [0, cw, 2 * cw, 3 * cw, 3 * cw + iw, 3 * cw + iw + IDX_DIM, 3 * cw + iw + IDX_DIM + IDX_HEADS]
    qkv_c = cd_w_in[:, :offs[3]]
    q_idx = cd_w_in[:, offs[3]:offs[4]]
    k_idx = cd_w_in[:, offs[4]:offs[5]]
    w_idx = cd_w_in[:, offs[5]:offs[6]]
    qkv_d = cd_w_in[:, offs[6]:]
    pad = jnp.zeros((cd_w_in.shape[0], LANES - w_idx.shape[1]), cd_w_in.dtype)
    return jnp.concatenate([qkv_c, q_idx, qkv_d, k_idx, k_idx, w_idx, pad], axis=1)


def kernel(x, c, norm_g, ada_w, ada_b, mlp_w1, mlp_w2, ab_w_in, ab_w_out, pool_w, pool_scale, ssm_lam_re, ssm_lam_im, ssm_log_dt, ssm_b_re, ssm_b_im, ssm_c_re, ssm_c_im, ssm_d, glu_w, glu_b, cd_w_in, cd_w_out, final_g):
    bsz, seq, d = x.shape
    depth = norm_g.shape[0]
    topk = min(TOPK_MAX, seq // 4)
    tm = min(512, seq)

    c_pad = jnp.zeros((SUBLANES, d), F32).at[:bsz].set(c)
    mod = _modulation(c_pad, ada_w.reshape(depth * 2, d, 3 * d), ada_b.reshape(depth * 2, 1, 3 * d))
    mod = mod[:, :, :bsz, None, :]

    for i in range(depth):
        j = i // 2
        shift, scale1, gate1 = mod[2 * i, 0], mod[2 * i, 1], mod[2 * i, 2]
        g = norm_g[i, 0][None, :]
        if i % 2 == 0:
            width = ab_w_in.shape[2]
            tt = min(256, seq)
            (proj,) = _norm_proj(x, g, shift, scale1, ab_w_in[j].astype(BF16), (width,), (BF16,), tm)
            s5 = _s5_params(ssm_lam_re[j], ssm_lam_im[j], ssm_log_dt[j],
                            ssm_b_re[j], ssm_b_im[j], ssm_c_re[j], ssm_c_im[j], seg_len=tt // SUBLANES)
            y = _even_mixer(proj, pool_w[j].astype(BF16), pool_scale[j][None, :], s5,
                            ssm_d[j][None, :], glu_w[j].astype(BF16), glu_b[j][None, :],
                            tt=tt, scan_lanes=512)
            ys, w_out = (y,), ab_w_out[j]
        else:
            hw = N_HEADS * HEAD_DIM
            widths = (hw,) * 7 + (LANES, LANES)
            dtypes = (BF16,) * 8 + (F32,)
            qc, kc, vc, qi, qd, kd, vd, kk, wq = _norm_proj(
                x, g, shift, scale1, _cd_weight(cd_w_in[j]).astype(BF16), widths, dtypes, tm)
            oc = _dsa(qc, kc, vc, qi, kk, wq, topk, qb=ATT_QUERY_BLOCK)
            od = _stick_breaking(qd, kd, vd, qb=ATT_QUERY_BLOCK)
            ys, w_out = (oc, od), cd_w_out[j]
        shift, scale1, gate2 = mod[2 * i + 1, 0], mod[2 * i + 1, 1], mod[2 * i + 1, 2]
        x = _out_mlp(x, gate1, ys, w_out.astype(BF16), norm_g[i, 1][None, :], shift, scale1, gate2,
                     mlp_w1[i].astype(BF16), mlp_w2[i].astype(BF16), final_g[None, :],
                     final_norm=(i == depth - 1), tm=min(1024, seq), tf=1024)
    return x
```

```python
import functools
import math

import jax
import jax.numpy as jnp
from jax import lax
from jax.experimental import pallas as pl
from jax.experimental.pallas import tpu as pltpu

F32 = jnp.float32
BF16 = jnp.bfloat16

EPS = 1e-6
CHUNK = 64
POOL_WINDOWS = (2, 4, 8, 16)
POOL_CH = 128
SSM_GROUP = 16
SSM_STATE = 64
HEAD_DIM = 64
N_HEADS = 8
IDX_HEADS = 8
IDX_DIM = 64
IDX_SCALE = (IDX_HEADS ** -0.5) * (IDX_DIM ** -0.5)
TOPK_MAX = 256

LANES = 128
SUBLANES = 8
VMEM_LIMIT = 56 * 1024 * 1024

ATT_BLOCK = 128
ATT_QUERY_BLOCK = 512
NEG_BIG = -1e30
COUNT_ALL = 1e9
BISECT_ITERS = 28
POP_SPAN = 2.0
INTERP_SPAN = 256.0
MAX_POPS = 4096


def _cparams(sem):
    return pltpu.CompilerParams(dimension_semantics=sem, vmem_limit_bytes=VMEM_LIMIT)


def _dot(a, b):
    return jnp.dot(a, b, preferred_element_type=F32)


def _dot_nt(a, b):
    return lax.dot_general(a, b, (((1,), (1,)), ((), ())), preferred_element_type=F32)


def _rowmax(x):
    return jnp.max(x, axis=1, keepdims=True)


def _norm_modulate(x, g, shift, scale1):
    ms = jnp.mean(x * x, axis=-1, keepdims=True)
    return (x * lax.rsqrt(ms + EPS)) * g * scale1 + shift


def _mod_kernel(c_ref, w_ref, b_ref, o_ref):
    j = pl.program_id(1)
    m = _dot(c_ref[...], w_ref[...]) + b_ref[...]
    o_ref[...] = m + jnp.where(j > 0, 1.0, 0.0)


def _modulation(c_pad, ada_w, ada_b):
    n_sub, d, _ = ada_w.shape
    rows = c_pad.shape[0]
    return pl.pallas_call(
        _mod_kernel,
        grid=(n_sub, 3),
        in_specs=[
            pl.BlockSpec((rows, d), lambda s, j: (0, 0)),
            pl.BlockSpec((None, d, d), lambda s, j: (s, 0, j)),
            pl.BlockSpec((None, 1, d), lambda s, j: (s, 0, j)),
        ],
        out_specs=pl.BlockSpec((None, None, rows, d), lambda s, j: (s, j, 0, 0)),
        out_shape=jax.ShapeDtypeStruct((n_sub, 3, rows, d), F32),
        compiler_params=_cparams(("arbitrary", "arbitrary")),
    )(c_pad, ada_w, ada_b)


def _norm_proj_kernel(x_ref, g_ref, shift_ref, scale_ref, w_ref, *o_refs, widths):
    h = _norm_modulate(x_ref[...], g_ref[...], shift_ref[...], scale_ref[...]).astype(BF16)
    y = _dot(h, w_ref[...])
    off = 0
    for o_ref, width in zip(o_refs, widths):
        o_ref[...] = y[:, off:off + width].astype(o_ref.dtype)
        off += width


def _norm_proj(x, g, shift, scale1, w, widths, dtypes, tm):
    bsz, seq, d = x.shape
    n = w.shape[1]
    vec = pl.BlockSpec((None, 1, d), lambda b, i: (b, 0, 0))
    return pl.pallas_call(
        functools.partial(_norm_proj_kernel, widths=widths),
        grid=(bsz, seq // tm),
        in_specs=[
            pl.BlockSpec((None, tm, d), lambda b, i: (b, i, 0)),
            pl.BlockSpec((1, d), lambda b, i: (0, 0)),
            vec, vec,
            pl.BlockSpec((d, n), lambda b, i: (0, 0)),
        ],
        out_specs=[pl.BlockSpec((None, tm, wd), lambda b, i: (b, i, 0)) for wd in widths],
        out_shape=[jax.ShapeDtypeStruct((bsz, seq, wd), dt) for wd, dt in zip(widths, dtypes)],
        compiler_params=_cparams(("parallel", "parallel")),
    )(x, g, shift, scale1, w)


def _out_mlp_kernel(x_ref, gate_mix_ref, *refs, n_y, final_norm):
    y_refs = refs[:n_y]
    w_out_ref, g_ref, shift_ref, scale_ref, gate_ref, w1_ref, w2_ref, fg_ref, o_ref, h_s, acc_s = refs[n_y:]
    f = pl.program_id(2)

    @pl.when(f == 0)
    def _():
        y = jnp.concatenate([r[...] for r in y_refs], axis=1) if n_y > 1 else y_refs[0][...]
        x1 = x_ref[...] + gate_mix_ref[...] * _dot(y, w_out_ref[...])
        o_ref[...] = x1
        h_s[...] = _norm_modulate(x1, g_ref[...], shift_ref[...], scale_ref[...]).astype(BF16)
        acc_s[...] = jnp.zeros_like(acc_s)

    a = jnp.maximum(_dot(h_s[...], w1_ref[...]), 0.0)
    acc_s[...] += _dot((a * a).astype(BF16), w2_ref[...])

    @pl.when(f == pl.num_programs(2) - 1)
    def _():
        y = o_ref[...] + gate_ref[...] * acc_s[...]
        if final_norm:
            ms = jnp.mean(y * y, axis=-1, keepdims=True)
            y = (y * lax.rsqrt(ms + EPS)) * fg_ref[...]
        o_ref[...] = y


def _out_mlp(x, gate_mix, ys, w_out, g, shift, scale1, gate1, w1, w2, final_g, final_norm, tm, tf):
    bsz, seq, d = x.shape
    dff = w1.shape[1]
    row = lambda wd: pl.BlockSpec((None, tm, wd), lambda b, i, f: (b, i, 0))
    vec = pl.BlockSpec((None, 1, d), lambda b, i, f: (b, 0, 0))
    one = pl.BlockSpec((1, d), lambda b, i, f: (0, 0))
    return pl.pallas_call(
        functools.partial(_out_mlp_kernel, n_y=len(ys), final_norm=final_norm),
        grid=(bsz, seq // tm, dff // tf),
        in_specs=[
            row(d), vec, *[row(y.shape[2]) for y in ys],
            pl.BlockSpec(w_out.shape, lambda b, i, f: (0, 0)),
            one, vec, vec, vec,
            pl.BlockSpec((d, tf), lambda b, i, f: (0, f)),
            pl.BlockSpec((tf, d), lambda b, i, f: (f, 0)),
            one,
        ],
        out_specs=row(d),
        out_shape=jax.ShapeDtypeStruct((bsz, seq, d), F32),
        scratch_shapes=[pltpu.VMEM((tm, d), BF16), pltpu.VMEM((tm, d), F32)],
        compiler_params=_cparams(("parallel", "parallel", "arbitrary")),
    )(x, gate_mix, *ys, w_out, g, shift, scale1, gate1, w1, w2, final_g)


def _cmul(ar, ai, br, bi):
    return ar * br - ai * bi, ar * bi + ai * br


def _s5_params_kernel(lr_ref, li_ref, ldt_ref, brt_ref, bit_ref, crt_ref, cit_ref,
                      wb_ref, wc_ref, a_re_ref, a_im_ref, pk_re_ref, pk_im_ref, sg_re_ref, sg_im_ref):
    lr, li = lr_ref[...], li_ref[...]
    dt = jnp.exp(ldt_ref[...])
    mag = jnp.exp(lr * dt)
    ar, ai = mag * jnp.cos(li * dt), mag * jnp.sin(li * dt)
    den = lr * lr + li * li
    zr = ((ar - 1.0) * lr + ai * li) / den
    zi = (ai * lr - (ar - 1.0) * li) / den

    rows, cols = brt_ref.shape
    rgrp = lax.broadcasted_iota(jnp.int32, (rows, cols), 0) // SSM_GROUP
    cgrp = lax.broadcasted_iota(jnp.int32, (rows, cols), 1) // SSM_STATE
    diag = rgrp == cgrp
    br, bi = brt_ref[...], bit_ref[...]
    wb_ref[0] = jnp.where(diag, zr * br - zi * bi, 0.0).astype(wb_ref.dtype)
    wb_ref[1] = jnp.where(diag, zr * bi + zi * br, 0.0).astype(wb_ref.dtype)
    wc_ref[0] = jnp.where(diag, crt_ref[...], 0.0).astype(wc_ref.dtype)
    wc_ref[1] = jnp.where(diag, cit_ref[...], 0.0).astype(wc_ref.dtype)

    n = lr.shape[1]
    rows8 = pk_re_ref.shape[0]
    seg_len = rows8 // SUBLANES
    a_re_ref[...] = jnp.broadcast_to(ar, (SUBLANES, n))
    a_im_ref[...] = jnp.broadcast_to(ai, (SUBLANES, n))
    expo = lax.broadcasted_iota(jnp.int32, (rows8, n), 0) // SUBLANES + 1
    sq_r, sq_i = jnp.broadcast_to(ar, (rows8, n)), jnp.broadcast_to(ai, (rows8, n))
    pr, pi = jnp.ones((rows8, n), F32), jnp.zeros((rows8, n), F32)
    for b in range(seg_len.bit_length()):
        nr, ni = _cmul(pr, pi, sq_r, sq_i)
        take = ((expo >> b) & 1) == 1
        pr, pi = jnp.where(take, nr, pr), jnp.where(take, ni, pi)
        sq_r, sq_i = _cmul(sq_r, sq_i, sq_r, sq_i)
    pk_re_ref[...] = pr
    pk_im_ref[...] = pi
    row = lax.broadcasted_iota(jnp.int32, (SUBLANES, n), 0)
    gr, gi = pr[rows8 - SUBLANES:], pi[rows8 - SUBLANES:]
    for k in range(3):
        keep = row >= (1 << k)
        sg_re_ref[k] = jnp.where(keep, gr, 0.0)
        sg_im_ref[k] = jnp.where(keep, gi, 0.0)
        gr, gi = _cmul(gr, gi, gr, gi)


def _s5_params(lam_re, lam_im, log_dt, b_re, b_im, c_re, c_im, seg_len):
    g, p = lam_re.shape
    c = b_re.shape[2]
    n = g * p
    flat = lambda a: a.reshape(1, n)
    ldt = jnp.broadcast_to(log_dt[:, None], (g, p)).reshape(1, n)
    b_t = lambda a: jnp.tile(jnp.transpose(a, (2, 0, 1)).reshape(c, n), (g, 1))
    c_t = lambda a: jnp.tile(a.reshape(g * c, p), (1, g))
    outs = pl.pallas_call(
        _s5_params_kernel,
        out_shape=[jax.ShapeDtypeStruct((2, g * c, n), BF16), jax.ShapeDtypeStruct((2, g * c, n), BF16),
                   jax.ShapeDtypeStruct((SUBLANES, n), F32), jax.ShapeDtypeStruct((SUBLANES, n), F32),
                   jax.ShapeDtypeStruct((seg_len * SUBLANES, n), F32), jax.ShapeDtypeStruct((seg_len * SUBLANES, n), F32),
                   jax.ShapeDtypeStruct((3, SUBLANES, n), F32), jax.ShapeDtypeStruct((3, SUBLANES, n), F32)],
        compiler_params=pltpu.CompilerParams(vmem_limit_bytes=VMEM_LIMIT),
    )(flat(lam_re), flat(lam_im), ldt, b_t(b_re), b_t(b_im), c_t(c_re), c_t(c_im))
    wb, wct, a_re, a_im, pk_re, pk_im, sg_re, sg_im = outs
    wc = jnp.transpose(wct, (0, 2, 1))
    pk_re, pk_im = (a.reshape(seg_len, SUBLANES, n) for a in (pk_re, pk_im))
    return wb, wc, a_re, a_im, pk_re, pk_im, sg_re, sg_im


def _gelu_tanh(x):
    return 0.5 * x * (1.0 + jnp.tanh(math.sqrt(2.0 / math.pi) * (x + 0.044715 * (x * x * x))))


def _even_kernel(p_ref, band_ref, perm_ref, wpool_ref, pscale_ref, wb_ref, wc_ref, dskip_ref, wglu_ref, bglu_ref,
                 a_re_ref, a_im_ref, pk_re_ref, pk_im_ref, sg_re_ref, sg_im_ref,
                 o_ref, prev_s, xr_s, xi_s, car_re, car_im, *, mix_a, scan_lanes):
    i = pl.program_id(1)
    tt = p_ref.shape[0]

    @pl.when(i == 0)
    def _():
        prev_s[...] = jnp.zeros_like(prev_s)
        car_re[...] = jnp.zeros_like(car_re)
        car_im[...] = jnp.zeros_like(car_im)

    ua = p_ref[:, :mix_a]
    ext = jnp.concatenate([prev_s[...], ua], axis=0)
    t1 = i * tt + lax.broadcasted_iota(jnp.int32, (tt, POOL_CH), 0) + 1
    for g, win in enumerate(POOL_WINDOWS):
        sl = slice(g * POOL_CH, (g + 1) * POOL_CH)
        wsum = _dot(band_ref[g], ext[:, sl])
        cnt = jnp.minimum(t1, win).astype(F32)
        pg = wsum / cnt - ua[:, sl].astype(F32)
        yg = _dot(pg.astype(BF16), wpool_ref[g]) * pscale_ref[:, sl]
        o_ref[:, sl] = yg.astype(o_ref.dtype)
    prev_s[...] = ua

    ub = _dot(perm_ref[0], p_ref[:, mix_a:]).astype(BF16)
    n_tiles = xr_s.shape[0]
    n_diag_blocks = ub.shape[1] // LANES
    tpb = n_tiles // n_diag_blocks
    sw = tpb * LANES
    for j in range(n_diag_blocks):
        cs, ss = slice(j * LANES, (j + 1) * LANES), slice(j * sw, (j + 1) * sw)
        xr, xi = _dot(ub[:, cs], wb_ref[0, cs, ss]), _dot(ub[:, cs], wb_ref[1, cs, ss])
        for q in range(tpb):
            xr_s[j * tpb + q] = xr[:, q * LANES:(q + 1) * LANES]
            xi_s[j * tpb + q] = xi[:, q * LANES:(q + 1) * LANES]

    seg_len = tt // SUBLANES
    seg_rows = lambda k: pl.ds(pl.multiple_of(k * SUBLANES, SUBLANES), SUBLANES)
    first_seg = lax.broadcasted_iota(jnp.int32, (SUBLANES, LANES), 0) == 0
    zeros = jnp.zeros((SUBLANES, LANES), F32)
    tpc = scan_lanes // LANES
    for c in range(n_tiles // tpc):
        tiles = list(range(c * tpc, (c + 1) * tpc))
        lanes = [slice(t * LANES, (t + 1) * LANES) for t in tiles]

        def local(k, carry, tiles=tiles, lanes=lanes):
            out = []
            for t, ls, (hr, hi) in zip(tiles, lanes, carry):
                dr, di = _cmul(a_re_ref[:, ls], a_im_ref[:, ls], hr, hi)
                hr, hi = dr + xr_s[t, seg_rows(k), :], di + xi_s[t, seg_rows(k), :]
                xr_s[t, seg_rows(k), :] = hr
                xi_s[t, seg_rows(k), :] = hi
                out.append((hr, hi))
            return tuple(out)

        ends = lax.fori_loop(0, seg_len, local, tuple((zeros, zeros) for _ in tiles))
        starts = []
        for t, ls, (er, ei) in zip(tiles, lanes, ends):
            fr = jnp.where(first_seg, car_re[:, ls], pltpu.roll(er, 1, 0))
            fi = jnp.where(first_seg, car_im[:, ls], pltpu.roll(ei, 1, 0))
            for k in range(3):
                dr, di = _cmul(sg_re_ref[k, :, ls], sg_im_ref[k, :, ls],
                               pltpu.roll(fr, 1 << k, 0), pltpu.roll(fi, 1 << k, 0))
                fr, fi = fr + dr, fi + di
            starts.append((fr, fi))
            dr, di = _cmul(pk_re_ref[seg_len - 1, :, ls], pk_im_ref[seg_len - 1, :, ls], fr, fi)
            car_re[:, ls] = jnp.broadcast_to((dr + er)[SUBLANES - 1:], fr.shape)
            car_im[:, ls] = jnp.broadcast_to((di + ei)[SUBLANES - 1:], fr.shape)

        def fix(k, carry, tiles=tiles, lanes=lanes, starts=starts):
            for t, ls, (fr, fi) in zip(tiles, lanes, starts):
                dr, di = _cmul(pk_re_ref[k, :, ls], pk_im_ref[k, :, ls], fr, fi)
                xr_s[t, seg_rows(k), :] += dr
                xi_s[t, seg_rows(k), :] += di
            return carry

        lax.fori_loop(0, seg_len, fix, 0)

    state = lambda ref, j: jnp.concatenate([ref[j * tpb + q] for q in range(tpb)], axis=1).astype(BF16)
    y = jnp.concatenate(
        [_dot(state(xr_s, j), wc_ref[0, j * sw:(j + 1) * sw, j * LANES:(j + 1) * LANES])
         - _dot(state(xi_s, j), wc_ref[1, j * sw:(j + 1) * sw, j * LANES:(j + 1) * LANES])
         for j in range(n_diag_blocks)], axis=1)
    y = _gelu_tanh(y + dskip_ref[...] * ub.astype(F32))
    z = _dot(y.astype(BF16), wglu_ref[...]) + bglu_ref[...]
    out = (y * (1.0 / (1.0 + jnp.exp(-z)))).astype(o_ref.dtype)
    o_ref[:, mix_a:] = _dot(perm_ref[1], out).astype(o_ref.dtype)


def _segment_perm(tt):
    seg_len = tt // SUBLANES
    r = jnp.arange(tt)
    src = (r % SUBLANES) * seg_len + r // SUBLANES
    p = (src[:, None] == jnp.arange(tt)[None, :])
    return jnp.stack([p, p.T]).astype(BF16)


def _pool_band(tt):
    t = jnp.arange(tt)[:, None] + tt
    j = jnp.arange(2 * tt)[None, :]
    return jnp.stack([((j <= t) & (j > t - w)) for w in POOL_WINDOWS]).astype(BF16)


def _even_mixer(proj, wpool, pscale, s5, dskip, wglu, bglu, tt, scan_lanes):
    bsz, seq, width = proj.shape
    wb, wc, a_re, a_im, pk_re, pk_im, sg_re, sg_im = s5
    mix_a = len(POOL_WINDOWS) * POOL_CH
    mix_b = width - mix_a
    n_state = wb.shape[2]
    const = lambda a: pl.BlockSpec(a.shape, lambda b, i, nd=a.ndim: (0,) * nd)
    band = _pool_band(tt)
    args = (band, _segment_perm(tt), wpool, pscale, wb, wc, dskip, wglu, bglu, a_re, a_im, pk_re, pk_im, sg_re, sg_im)
    return pl.pallas_call(
        functools.partial(_even_kernel, mix_a=mix_a, scan_lanes=scan_lanes),
        grid=(bsz, seq // tt),
        in_specs=[pl.BlockSpec((None, tt, width), lambda b, i: (b, i, 0))] + [const(a) for a in args],
        out_specs=pl.BlockSpec((None, tt, width), lambda b, i: (b, i, 0)),
        out_shape=jax.ShapeDtypeStruct((bsz, seq, width), BF16),
        scratch_shapes=[pltpu.VMEM((tt, mix_a), BF16),
                        pltpu.VMEM((n_state // LANES, tt, LANES), F32), pltpu.VMEM((n_state // LANES, tt, LANES), F32),
                        pltpu.VMEM((SUBLANES, n_state), F32), pltpu.VMEM((SUBLANES, n_state), F32)],
        compiler_params=_cparams(("parallel", "arbitrary")),
    )(proj, *args)


def _pair_blockdiag(x, first_ref, second_ref):
    return jnp.concatenate([x * first_ref[...], x * second_ref[...]], axis=0)


def _dsa_kernel(q_ref, qi_ref, wq_ref, k_ref, v_ref, kk_ref, tri_ref, ones_ref, first_ref, second_ref, alibi_ref,
                o_ref,
                s_s, qs_s, mx_s, al_s, *, topk, qb):
    i = pl.program_id(1)
    kb_size = ATT_BLOCK
    n_diag = qb // kb_size
    nkb = (i + 1) * n_diag
    shape = (qb, kb_size)
    row = lax.broadcasted_iota(jnp.int32, shape, 0)
    col = lax.broadcasted_iota(jnp.int32, shape, 1)
    q_chunk = (i * qb + row) // CHUNK
    bcast = lambda a: jnp.broadcast_to(a, shape)
    kf = float(topk)
    pairs = [slice(p * LANES, (p + 1) * LANES) for p in range(N_HEADS // 2)]

    wq = wq_ref[...]
    w_cols = [bcast(wq[:, h:h + 1] * IDX_SCALE) for h in range(IDX_HEADS)]

    def indexer(kb, rows):
        kbd = _pair_blockdiag(kk_ref[kb], first_ref, second_ref)
        acc = jnp.zeros((rows.stop - rows.start, kb_size), F32)
        for p, ps in enumerate(pairs):
            rel = jnp.maximum(_dot_nt(qi_ref[rows, ps], kbd), 0.0)
            acc = acc + rel[:, :kb_size] * w_cols[2 * p][rows] + rel[:, kb_size:] * w_cols[2 * p + 1][rows]
        return acc

    def past_block(kb, carry):
        rmax, rmin = carry
        acc = indexer(kb, slice(0, qb))
        s_s[kb] = acc
        return jnp.maximum(rmax, acc), jnp.minimum(rmin, acc)

    rmax, rmin = lax.fori_loop(0, i * n_diag, past_block,
                               (jnp.full(shape, -jnp.inf, F32), jnp.full(shape, jnp.inf, F32)))
    for u in range(n_diag):
        kb, r0 = i * n_diag + u, u * kb_size
        rows = slice(r0, qb)
        acc = indexer(kb, rows)
        adm = (((kb * kb_size + col) // CHUNK) <= q_chunk)[rows]
        s_s[kb, rows] = jnp.where(adm, acc, -jnp.inf)
        hi_part = jnp.maximum(rmax[rows], jnp.where(adm, acc, -jnp.inf))
        lo_part = jnp.minimum(rmin[rows], jnp.where(adm, acc, jnp.inf))
        if r0:
            s_s[kb, :r0] = jnp.full((r0, kb_size), -jnp.inf, F32)
            hi_part = jnp.concatenate([rmax[:r0], hi_part], axis=0)
            lo_part = jnp.concatenate([rmin[:r0], lo_part], axis=0)
        rmax, rmin = hi_part, lo_part
    n_chunks = qb // kb_size
    chunk_rows = lambda c: slice(c * kb_size, (c + 1) * kb_size)
    chunk_subl = lambda c: slice(c * SUBLANES, (c + 1) * SUBLANES)
    ones8 = jnp.ones((SUBLANES, kb_size), BF16)
    tile = (kb_size, kb_size)

    def to_dense(rep):
        return rep.T[:SUBLANES]

    def to_rows(dense):
        return jnp.broadcast_to(dense[:1], tile).T

    def dense_of(fn):
        return jnp.concatenate([fn(c) for c in range(n_chunks)], axis=0)

    def fold_keys(c, fn, init):
        def body(g, a):
            for u in range(n_diag):
                a = fn(a, s_s[g * n_diag + u, chunk_rows(c), :])
            return a
        return lax.fori_loop(0, i + 1, body, init)

    def count(x, cmp):
        xbs = [to_rows(x[chunk_subl(c)]) for c in range(n_chunks)]
        cnts = [fold_keys(c, lambda a, s, xb=xbs[c]: a + jnp.where(cmp(s, xb), 1.0, 0.0), jnp.zeros(tile, F32))
                for c in range(n_chunks)]
        return jnp.concatenate([_dot_nt(ones8, cnt.astype(BF16)) for cnt in cnts], axis=0)

    count_ge = functools.partial(count, cmp=lambda s, xb: s >= xb)
    count_gt = functools.partial(count, cmp=lambda s, xb: s > xb)
    count_eq = functools.partial(count, cmp=lambda s, xb: s == xb)

    smax = dense_of(lambda c: to_dense(jnp.broadcast_to(_rowmax(rmax[chunk_rows(c)]), tile)))
    smin = dense_of(lambda c: to_dense(jnp.broadcast_to(jnp.min(rmin[chunk_rows(c)], axis=1, keepdims=True), tile)))
    dshape = (n_chunks * SUBLANES, kb_size)
    t_dense = (i * qb + (lax.broadcasted_iota(jnp.int32, dshape, 0) // SUBLANES) * kb_size
               + lax.broadcasted_iota(jnp.int32, dshape, 1))
    n_adm = ((t_dense // CHUNK + 1) * CHUNK).astype(F32)
    hi0 = smax + jnp.maximum(jnp.abs(smax) * 1e-6, 1e-30)
    zero = jnp.zeros_like(smin)
    cge0, cgt0 = count_ge(zero), count_gt(zero)
    wide = n_adm > kf
    zero_tie = jnp.logical_and(wide, jnp.logical_and(cgt0 < kf, cge0 >= kf))
    above = jnp.logical_and(wide, cgt0 >= kf)
    below = jnp.logical_and(wide, cge0 < kf)
    lo0 = jnp.where(jnp.logical_or(above, zero_tie), 0.0, smin)
    clo0 = jnp.where(zero_tie, kf, jnp.where(above, cge0, n_adm))
    hi0 = jnp.where(below, 0.0, hi0)
    chi0 = jnp.where(below, cge0, 0.0)
    need0 = jnp.where(zero_tie, kf - cgt0, COUNT_ALL)

    def unresolved(clo, chi):
        return jnp.max(jnp.where(jnp.logical_and(clo > kf, clo - chi > POP_SPAN), 1.0, 0.0))

    def bisect_cond(carry):
        return jnp.logical_and(carry[0] > 0.0, carry[1] < BISECT_ITERS)

    def bisect(carry):
        _, it, lo, hi, clo, chi = carry
        span = clo - chi
        aim = (clo - (kf + 0.5)) / span
        interpolate = jnp.logical_and(span < INTERP_SPAN, it % 2 == 0)
        mid = lo + jnp.where(interpolate, aim, 0.5) * (hi - lo)
        c = count_ge(mid)
        active = clo > kf
        up = jnp.logical_and(active, c >= kf)
        down = jnp.logical_and(active, c < kf)
        clo = jnp.where(up, c, clo)
        chi = jnp.where(down, c, chi)
        return (unresolved(clo, chi), it + 1, jnp.where(up, mid, lo), jnp.where(down, mid, hi), clo, chi)

    _, _, lo, hi, clo, chi = lax.while_loop(
        bisect_cond, bisect, (unresolved(clo0, chi0), jnp.int32(0), lo0, hi0, clo0, chi0))

    done0 = jnp.where(clo <= kf, 1.0, 0.0)

    def pop_cond(carry):
        return jnp.logical_and(carry[0] > 0.0, carry[1] < MAX_POPS)

    def next_below(hi):
        def one(c):
            hib = to_rows(hi[chunk_subl(c)])
            best = fold_keys(c, lambda a, s: jnp.maximum(a, jnp.where(s < hib, s, -jnp.inf)),
                             jnp.full(tile, -jnp.inf, F32))
            return to_dense(jnp.broadcast_to(_rowmax(best), tile))
        return dense_of(one)

    def pop(carry):
        _, it, hi, chi, thr, need, done = carry
        v = next_below(hi)
        cv = count_eq(v)
        active = done < 0.5
        fin = jnp.logical_and(active, chi + cv >= kf)
        go = jnp.logical_and(active, chi + cv < kf)
        thr = jnp.where(fin, v, thr)
        need = jnp.where(fin, kf - chi, need)
        done = jnp.where(fin, 1.0, done)
        hi = jnp.where(go, v, hi)
        chi = jnp.where(go, chi + cv, chi)
        return (jnp.max(1.0 - done), it + 1, hi, chi, thr, need, done)

    left0 = jnp.max(1.0 - done0)
    _, _, _, _, thr, need, _ = lax.while_loop(
        pop_cond, pop, (left0, jnp.int32(0), hi, chi, lo, need0, done0))
    thrb = jnp.concatenate([to_rows(thr[chunk_subl(c)]) for c in range(n_chunks)], axis=0)
    needb = jnp.concatenate([to_rows(need[chunk_subl(c)]) for c in range(n_chunks)], axis=0)
    partial_ties = jnp.max(jnp.where(need < COUNT_ALL, 1.0, 0.0))

    @pl.when(partial_ties > 0.0)
    def _():
        def tie_block(kb, run):
            s = s_s[kb]
            eq = jnp.where(s == thrb, 1.0, 0.0).astype(BF16)
            rank = _dot(eq, tri_ref[1]) + run
            take = jnp.where(s == thrb, jnp.where(rank < needb, 0.0, NEG_BIG), NEG_BIG)
            s_s[kb] = jnp.where(s > thrb, 0.0, take)
            return run + _dot(eq, tri_ref[2])
        lax.fori_loop(0, nkb, tie_block, jnp.zeros(shape, F32))

    @pl.when(partial_ties <= 0.0)
    def _():
        def sel_block(kb, carry):
            s_s[kb] = jnp.where(s_s[kb] >= thrb, 0.0, NEG_BIG)
            return carry
        lax.fori_loop(0, nkb, sel_block, 0)

    first_half = col < HEAD_DIM
    qs_s[...] = q_ref[...] * (HEAD_DIM ** -0.5)
    mx_s[...] = jnp.full(mx_s.shape, NEG_BIG, F32)
    al_s[...] = jnp.zeros_like(al_s)
    slopes = [2.0 ** (-8.0 * (h + 1) / N_HEADS) for h in range(N_HEADS)]

    q_alibi = jnp.where(col < 4, 1.0, 0.0).astype(BF16)

    def attend_group(g, past):
        blocks = [g * n_diag + u for u in range(n_diag)]
        if not past:
            t_pos = (i * qb + row).astype(F32)
            hats = []
            for kb in blocks:
                s_pos = (kb * kb_size + col).astype(F32)
                hats.append(jnp.minimum(s_pos, 2.0 * t_pos - s_pos))
        for p, ps in enumerate(pairs):
            tiles = []
            for u, kb in enumerate(blocks):
                kbd = _pair_blockdiag(k_ref[kb, :, ps], first_ref, second_ref)
                if past:
                    mask = s_s[kb]
                    k_alibi = alibi_ref[0, p] + alibi_ref[1, p] * kb.astype(BF16)
                    logits = _dot_nt(jnp.concatenate([qs_s[:, ps], q_alibi], axis=1),
                                     jnp.concatenate([kbd, k_alibi], axis=1))
                    tiles.append((0, logits + jnp.concatenate([mask, mask], axis=1)))
                else:
                    r0 = u * kb_size
                    mask, hat = s_s[kb, r0:, :], hats[u][r0:]
                    bias = jnp.concatenate([mask + slopes[2 * p] * hat, mask + slopes[2 * p + 1] * hat], axis=1)
                    tiles.append((r0, _dot_nt(qs_s[r0:, ps], kbd) + bias))

            def padded(r0, part, fill):
                return part if r0 == 0 else jnp.concatenate(
                    [jnp.full((r0, part.shape[1]), fill, part.dtype), part], axis=0)

            top = functools.reduce(jnp.maximum, [padded(r0, t, NEG_BIG) for r0, t in tiles])
            m_old = mx_s[p]
            m_new = jnp.maximum(m_old, jnp.concatenate(
                [bcast(_rowmax(top[:, :kb_size])), bcast(_rowmax(top[:, kb_size:]))], axis=1))
            mx_s[p] = m_new
            pexp = jnp.concatenate(
                [padded(r0, jnp.exp((t - m_new[r0:]).astype(BF16)), 0.0) for r0, t in tiles], axis=1)
            shrink = m_old - m_new
            alpha = jnp.exp(jnp.where(first_half, shrink[:, :kb_size], shrink[:, kb_size:]))
            values = jnp.concatenate(
                [jnp.concatenate([_pair_blockdiag(v_ref[kb, :, ps], first_ref, second_ref), ones_ref[...]], axis=1)
                 for kb in blocks], axis=0)
            al_s[p] = jnp.concatenate([alpha, alpha], axis=1) * al_s[p] + _dot(pexp, values)

    def past_group(g, carry):
        attend_group(g, True)
        return carry

    lax.fori_loop(0, i, past_group, 0)
    attend_group(i, False)
    for p, ps in enumerate(pairs):
        state = al_s[p]
        o_ref[:, ps] = (state[:, :LANES] / state[:, LANES:]).astype(o_ref.dtype)


def _tri_consts(n):
    j = jnp.arange(n)[:, None]
    s = jnp.arange(n)[None, :]
    return jnp.stack([j > s, j < s, jnp.ones((n, n), bool)]).astype(BF16)


def _pair_consts(n):
    tri = _tri_consts(n)
    zero = jnp.zeros((n, n), BF16)
    tri2 = jnp.block([[tri[0], zero], [zero, tri[0]]])
    first = (jnp.arange(n)[None, :] < n // 2) & jnp.ones((n, 1), bool)
    ones2 = jnp.concatenate([first, ~first], axis=0).astype(BF16)
    return tri2, ones2, first.astype(BF16), (~first).astype(BF16)


def _alibi_key_terms(n):
    r = jnp.arange(2 * n)[:, None]
    lane = jnp.arange(LANES)[None, :]
    head_b = r >= n
    out = []
    for p in range(N_HEADS // 2):
        slope = jnp.where(head_b, 2.0 ** (-8.0 * (2 * p + 2) / N_HEADS), 2.0 ** (-8.0 * (2 * p + 1) / N_HEADS))
        in_block = jnp.where(lane == jnp.where(head_b, 3, 1), slope * (r % n), 0.0)
        per_block = jnp.where(lane == jnp.where(head_b, 2, 0), slope * n, 0.0)
        out.append(jnp.stack([in_block, per_block]))
    return jnp.stack(out, axis=1).astype(BF16)


def _dsa(q, k, v, qi, kk, wq, topk, qb):
    bsz, seq, width = q.shape
    nq, nk = seq // qb, seq // ATT_BLOCK
    qblk = lambda a: a.reshape(bsz, nq, qb, a.shape[-1])
    kblk = lambda a: a.reshape(bsz, nk, ATT_BLOCK, a.shape[-1])
    qspec = lambda wd: pl.BlockSpec((None, None, qb, wd), lambda b, i: (b, i, 0, 0))
    kspec = lambda wd: pl.BlockSpec((None, nk, ATT_BLOCK, wd), lambda b, i: (b, 0, 0, 0))
    _, ones2, first, second = _pair_consts(ATT_BLOCK)
    consts = (_tri_consts(ATT_BLOCK), ones2, first, second, _alibi_key_terms(ATT_BLOCK))
    out = pl.pallas_call(
        functools.partial(_dsa_kernel, topk=topk, qb=qb),
        grid=(bsz, nq),
        in_specs=[qspec(width), qspec(width), qspec(LANES), kspec(width), kspec(width), kspec(LANES)]
                 + [pl.BlockSpec(a.shape, lambda b, i, nd=a.ndim: (0,) * nd) for a in consts],
        out_specs=qspec(width),
        out_shape=jax.ShapeDtypeStruct((bsz, nq, qb, width), BF16),
        scratch_shapes=[pltpu.VMEM((nk, qb, ATT_BLOCK), F32),
                        pltpu.VMEM((qb, width), BF16),
                        pltpu.VMEM((N_HEADS // 2, qb, 2 * ATT_BLOCK), F32),
                        pltpu.VMEM((N_HEADS // 2, qb, 2 * LANES), F32)],
        compiler_params=_cparams(("parallel", "arbitrary")),
    )(qblk(q), qblk(qi), qblk(wq), kblk(k), kblk(v), kblk(kk), *consts)
    return out.reshape(bsz, seq, width)


def _sb_kernel(q_ref, k_ref, v_ref, tri_ref, ones_ref, first_ref, second_ref, o_ref, qs_s, rem_s, acc_s, *, qb):
    i = pl.program_id(1)
    kb_size = ATT_BLOCK
    n_diag = qb // kb_size
    n_pairs = N_HEADS // 2
    shape = (qb, 2 * kb_size)
    row = lax.broadcasted_iota(jnp.int32, shape, 0)
    col = lax.broadcasted_iota(jnp.int32, shape, 1) % kb_size
    qs_s[...] = q_ref[...] * (HEAD_DIM ** -0.5)
    rem_s[...] = jnp.ones_like(rem_s)
    acc_s[...] = jnp.zeros_like(acc_s)

    def step(kb, masked, r0=0):
        rows = slice(r0, qb)
        causal = ((kb * kb_size + col) < (i * qb + row))[rows]
        pairs = [slice(p * LANES, (p + 1) * LANES) for p in range(n_pairs)]
        log_sig, log_keep = [], []
        for ps in pairs:
            kbd = _pair_blockdiag(k_ref[kb, :, ps], first_ref, second_ref)
            z = _dot_nt(qs_s[rows, ps], kbd).astype(BF16)
            soft = jnp.log(1.0 + jnp.exp(-jnp.abs(z)))
            keep = -jnp.maximum(z, 0.0) - soft
            log_sig.append(jnp.minimum(z, 0.0) - soft)
            log_keep.append(jnp.where(causal, keep, jnp.zeros_like(keep)) if masked else keep)
        for p, ps in enumerate(pairs):
            after = _dot(log_keep[p], tri_ref[...]).astype(BF16)
            w = jnp.exp(log_sig[p] + after)
            if masked:
                w = jnp.where(causal, w, jnp.zeros_like(w))
            vbd = _pair_blockdiag(v_ref[kb, :, ps], first_ref, second_ref)
            out = _dot(w, jnp.concatenate([vbd, ones_ref[...]], axis=1))
            rem = rem_s[p, rows]
            acc_s[p, rows] += rem * out[:, :LANES]
            rem_s[p, rows] = rem * jnp.maximum(1.0 - out[:, LANES:], 0.0)

    for u in reversed(range(n_diag)):
        step(i * n_diag + u, True, u * kb_size)

    def past_step(j, carry):
        step(i * n_diag - 1 - j, False)
        return carry

    lax.fori_loop(0, i * n_diag, past_step, 0)
    for p in range(n_pairs):
        o_ref[:, p * LANES:(p + 1) * LANES] = acc_s[p].astype(o_ref.dtype)


def _stick_breaking(q, k, v, qb):
    bsz, seq, width = q.shape
    nq, nk = seq // qb, seq // ATT_BLOCK
    qspec = pl.BlockSpec((None, None, qb, width), lambda b, i: (b, i, 0, 0))
    kspec = pl.BlockSpec((None, nk, ATT_BLOCK, width), lambda b, i: (b, 0, 0, 0))
    consts = _pair_consts(ATT_BLOCK)
    out = pl.pallas_call(
        functools.partial(_sb_kernel, qb=qb),
        grid=(bsz, nq),
        in_specs=[qspec, kspec, kspec] + [pl.BlockSpec(a.shape, lambda b, i: (0, 0)) for a in consts],
        out_specs=qspec,
        out_shape=jax.ShapeDtypeStruct((bsz, nq, qb, width), BF16),
        scratch_shapes=[pltpu.VMEM((qb, width), BF16),
                        pltpu.VMEM((N_HEADS // 2, qb, LANES), F32),
                        pltpu.VMEM((N_HEADS // 2, qb, LANES), F32)],
        compiler_params=_cparams(("parallel", "arbitrary")),
    )(q.reshape(bsz, nq, qb, width), k.reshape(bsz, nk, ATT_BLOCK, width),
      v.reshape(bsz, nk, ATT_BLOCK, width), *consts)
    return out.reshape(bsz, seq, width)


def _cd_weight(cd_w_in):
    cw = N_HEADS * HEAD_DIM
    iw = IDX_HEADS * IDX_DIM
    offs = [0, cw, 2 * cw, 3 * cw, 3 * cw + iw, 3 * cw + iw + IDX_DIM, 3 * cw + iw + IDX_DIM + IDX_HEADS]
    qkv_c = cd_w_in[:, :offs[3]]
    q_idx = cd_w_in[:, offs[3]:offs[4]]
    k_idx = cd_w_in[:, offs[4]:offs[5]]
    w_idx = cd_w_in[:, offs[5]:offs[6]]
    qkv_d = cd_w_in[:, offs[6]:]
    pad = jnp.zeros((cd_w_in.shape[0], LANES - w_idx.shape[1]), cd_w_in.dtype)
    return jnp.concatenate([qkv_c, q_idx, qkv_d, k_idx, k_idx, w_idx, pad], axis=1)


def kernel(x, c, norm_g, ada_w, ada_b, mlp_w1, mlp_w2, ab_w_in, ab_w_out, pool_w, pool_scale, ssm_lam_re, ssm_lam_im, ssm_log_dt, ssm_b_re, ssm_b_im, ssm_c_re, ssm_c_im, ssm_d, glu_w, glu_b, cd_w_in, cd_w_out, final_g):
    bsz, seq, d = x.shape
    depth = norm_g.shape[0]
    topk = min(TOPK_MAX, seq // 4)
    tm = min(512, seq)

    c_pad = jnp.zeros((SUBLANES, d), F32).at[:bsz].set(c)
    mod = _modulation(c_pad, ada_w.reshape(depth * 2, d, 3 * d), ada_b.reshape(depth * 2, 1, 3 * d))
    mod = mod[:, :, :bsz, None, :]

    for i in range(depth):
        j = i // 2
        shift, scale1, gate1 = mod[2 * i, 0], mod[2 * i, 1], mod[2 * i, 2]
        g = norm_g[i, 0][None, :]
        if i % 2 == 0:
            width = ab_w_in.shape[2]
            tt = min(256, seq)
            (proj,) = _norm_proj(x, g, shift, scale1, ab_w_in[j].astype(BF16), (width,), (BF16,), tm)
            s5 = _s5_params(ssm_lam_re[j], ssm_lam_im[j], ssm_log_dt[j],
                            ssm_b_re[j], ssm_b_im[j], ssm_c_re[j], ssm_c_im[j], seg_len=tt // SUBLANES)
            y = _even_mixer(proj, pool_w[j].astype(BF16), pool_scale[j][None, :], s5,
                            ssm_d[j][None, :], glu_w[j].astype(BF16), glu_b[j][None, :],
                            tt=tt, scan_lanes=512)
            ys, w_out = (y,), ab_w_out[j]
        else:
            hw = N_HEADS * HEAD_DIM
            widths = (hw,) * 7 + (LANES, LANES)
            dtypes = (BF16,) * 8 + (F32,)
            qc, kc, vc, qi, qd, kd, vd, kk, wq = _norm_proj(
                x, g, shift, scale1, _cd_weight(cd_w_in[j]).astype(BF16), widths, dtypes, tm)
            oc = _dsa(qc, kc, vc, qi, kk, wq, topk, qb=ATT_QUERY_BLOCK)
            od = _stick_breaking(qd, kd, vd, qb=ATT_QUERY_BLOCK)
            ys, w_out = (oc, od), cd_w_out[j]
        shift, scale1, gate2 = mod[2 * i + 1, 0], mod[2 * i + 1, 1], mod[2 * i + 1, 2]
        x = _out_mlp(x, gate1, ys, w_out.astype(BF16), norm_g[i, 1][None, :], shift, scale1, gate2,
                     mlp_w1[i].astype(BF16), mlp_w2[i].astype(BF16), final_g[None, :],
                     final_norm=(i == depth - 1), tm=min(1024, seq), tf=1024)
    return x
```

```python
import functools
import math

import jax
import jax.numpy as jnp
from jax import lax
from jax.experimental import pallas as pl
from jax.experimental.pallas import tpu as pltpu

F32 = jnp.float32
BF16 = jnp.bfloat16

EPS = 1e-6
CHUNK = 64
POOL_WINDOWS = (2, 4, 8, 16)
POOL_CH = 128
SSM_GROUP = 16
SSM_STATE = 64
HEAD_DIM = 64
N_HEADS = 8
IDX_HEADS = 8
IDX_DIM = 64
IDX_SCALE = (IDX_HEADS ** -0.5) * (IDX_DIM ** -0.5)
TOPK_MAX = 256

LANES = 128
SUBLANES = 8
VMEM_LIMIT = 56 * 1024 * 1024

ATT_BLOCK = 128
ATT_QUERY_BLOCK = 512
NEG_BIG = -1e30
COUNT_ALL = 1e9
BISECT_ITERS = 28
POP_SPAN = 2.0
INTERP_SPAN = 256.0
MAX_POPS = 4096


def _cparams(sem):
    return pltpu.CompilerParams(dimension_semantics=sem, vmem_limit_bytes=VMEM_LIMIT)


def _dot(a, b):
    return jnp.dot(a, b, preferred_element_type=F32)


def _dot_nt(a, b):
    return lax.dot_general(a, b, (((1,), (1,)), ((), ())), preferred_element_type=F32)


def _rowmax(x):
    return jnp.max(x, axis=1, keepdims=True)


def _norm_modulate(x, g, shift, scale1):
    ms = jnp.mean(x * x, axis=-1, keepdims=True)
    return (x * lax.rsqrt(ms + EPS)) * g * scale1 + shift


def _mod_kernel(c_ref, w_ref, b_ref, o_ref):
    j = pl.program_id(1)
    m = _dot(c_ref[...], w_ref[...]) + b_ref[...]
    o_ref[...] = m + jnp.where(j > 0, 1.0, 0.0)


def _modulation(c_pad, ada_w, ada_b):
    n_sub, d, _ = ada_w.shape
    rows = c_pad.shape[0]
    return pl.pallas_call(
        _mod_kernel,
        grid=(n_sub, 3),
        in_specs=[
            pl.BlockSpec((rows, d), lambda s, j: (0, 0)),
            pl.BlockSpec((None, d, d), lambda s, j: (s, 0, j)),
            pl.BlockSpec((None, 1, d), lambda s, j: (s, 0, j)),
        ],
        out_specs=pl.BlockSpec((None, None, rows, d), lambda s, j: (s, j, 0, 0)),
        out_shape=jax.ShapeDtypeStruct((n_sub, 3, rows, d), F32),
        compiler_params=_cparams(("arbitrary", "arbitrary")),
    )(c_pad, ada_w, ada_b)


def _norm_proj_kernel(x_ref, g_ref, shift_ref, scale_ref, w_ref, *o_refs, widths):
    h = _norm_modulate(x_ref[...], g_ref[...], shift_ref[...], scale_ref[...]).astype(BF16)
    y = _dot(h, w_ref[...])
    off = 0
    for o_ref, width in zip(o_refs, widths):
        o_ref[...] = y[:, off:off + width].astype(o_ref.dtype)
        off += width


def _norm_proj(x, g, shift, scale1, w, widths, dtypes, tm):
    bsz, seq, d = x.shape
    n = w.shape[1]
    vec = pl.BlockSpec((None, 1, d), lambda b, i: (b, 0, 0))
    return pl.pallas_call(
        functools.partial(_norm_proj_kernel, widths=widths),
        grid=(bsz, seq // tm),
        in_specs=[
            pl.BlockSpec((None, tm, d), lambda b, i: (b, i, 0)),
            pl.BlockSpec((1, d), lambda b, i: (0, 0)),
            vec, vec,
            pl.BlockSpec((d, n), lambda b, i: (0, 0)),
        ],
        out_specs=[pl.BlockSpec((None, tm, wd), lambda b, i: (b, i, 0)) for wd in widths],
        out_shape=[jax.ShapeDtypeStruct((bsz, seq, wd), dt) for wd, dt in zip(widths, dtypes)],
        compiler_params=_cparams(("parallel", "parallel")),
    )(x, g, shift, scale1, w)


def _out_mlp_kernel(x_ref, gate_mix_ref, *refs, n_y, final_norm):
    y_refs = refs[:n_y]
    w_out_ref, g_ref, shift_ref, scale_ref, gate_ref, w1_ref, w2_ref, fg_ref, o_ref, h_s, acc_s = refs[n_y:]
    f = pl.program_id(2)

    @pl.when(f == 0)
    def _():
        y = jnp.concatenate([r[...] for r in y_refs], axis=1) if n_y > 1 else y_refs[0][...]
        x1 = x_ref[...] + gate_mix_ref[...] * _dot(y, w_out_ref[...])
        o_ref[...] = x1
        h_s[...] = _norm_modulate(x1, g_ref[...], shift_ref[...], scale_ref[...]).astype(BF16)
        acc_s[...] = jnp.zeros_like(acc_s)

    a = jnp.maximum(_dot(h_s[...], w1_ref[...]), 0.0)
    acc_s[...] += _dot((a * a).astype(BF16), w2_ref[...])

    @pl.when(f == pl.num_programs(2) - 1)
    def _():
        y = o_ref[...] + gate_ref[...] * acc_s[...]
        if final_norm:
            ms = jnp.mean(y * y, axis=-1, keepdims=True)
            y = (y * lax.rsqrt(ms + EPS)) * fg_ref[...]
        o_ref[...] = y


def _out_mlp(x, gate_mix, ys, w_out, g, shift, scale1, gate1, w1, w2, final_g, final_norm, tm, tf):
    bsz, seq, d = x.shape
    dff = w1.shape[1]
    row = lambda wd: pl.BlockSpec((None, tm, wd), lambda b, i, f: (b, i, 0))
    vec = pl.BlockSpec((None, 1, d), lambda b, i, f: (b, 0, 0))
    one = pl.BlockSpec((1, d), lambda b, i, f: (0, 0))
    return pl.pallas_call(
        functools.partial(_out_mlp_kernel, n_y=len(ys), final_norm=final_norm),
        grid=(bsz, seq // tm, dff // tf),
        in_specs=[
            row(d), vec, *[row(y.shape[2]) for y in ys],
            pl.BlockSpec(w_out.shape, lambda b, i, f: (0, 0)),
            one, vec, vec, vec,
            pl.BlockSpec((d, tf), lambda b, i, f: (0, f)),
            pl.BlockSpec((tf, d), lambda b, i, f: (f, 0)),
            one,
        ],
        out_specs=row(d),
        out_shape=jax.ShapeDtypeStruct((bsz, seq, d), F32),
        scratch_shapes=[pltpu.VMEM((tm, d), BF16), pltpu.VMEM((tm, d), F32)],
        compiler_params=_cparams(("parallel", "parallel", "arbitrary")),
    )(x, gate_mix, *ys, w_out, g, shift, scale1, gate1, w1, w2, final_g)


def _cmul(ar, ai, br, bi):
    return ar * br - ai * bi, ar * bi + ai * br


def _s5_params_kernel(lr_ref, li_ref, ldt_ref, brt_ref, bit_ref, crt_ref, cit_ref,
                      wb_ref, wc_ref, a_re_ref, a_im_ref, pk_re_ref, pk_im_ref, sg_re_ref, sg_im_ref):
    lr, li = lr_ref[...], li_ref[...]
    dt = jnp.exp(ldt_ref[...])
    mag = jnp.exp(lr * dt)
    ar, ai = mag * jnp.cos(li * dt), mag * jnp.sin(li * dt)
    den = lr * lr + li * li
    zr = ((ar - 1.0) * lr + ai * li) / den
    zi = (ai * lr - (ar - 1.0) * li) / den

    rows, cols = brt_ref.shape
    rgrp = lax.broadcasted_iota(jnp.int32, (rows, cols), 0) // SSM_GROUP
    cgrp = lax.broadcasted_iota(jnp.int32, (rows, cols), 1) // SSM_STATE
    diag = rgrp == cgrp
    br, bi = brt_ref[...], bit_ref[...]
    wb_ref[0] = jnp.where(diag, zr * br - zi * bi, 0.0).astype(wb_ref.dtype)
    wb_ref[1] = jnp.where(diag, zr * bi + zi * br, 0.0).astype(wb_ref.dtype)
    wc_ref[0] = jnp.where(diag, crt_ref[...], 0.0).astype(wc_ref.dtype)
    wc_ref[1] = jnp.where(diag, cit_ref[...], 0.0).astype(wc_ref.dtype)

    n = lr.shape[1]
    rows8 = pk_re_ref.shape[0]
    seg_len = rows8 // SUBLANES
    a_re_ref[...] = jnp.broadcast_to(ar, (SUBLANES, n))
    a_im_ref[...] = jnp.broadcast_to(ai, (SUBLANES, n))
    expo = lax.broadcasted_iota(jnp.int32, (rows8, n), 0) // SUBLANES + 1
    sq_r, sq_i = jnp.broadcast_to(ar, (rows8, n)), jnp.broadcast_to(ai, (rows8, n))
    pr, pi = jnp.ones((rows8, n), F32), jnp.zeros((rows8, n), F32)
    for b in range(seg_len.bit_length()):
        nr, ni = _cmul(pr, pi, sq_r, sq_i)
        take = ((expo >> b) & 1) == 1
        pr, pi = jnp.where(take, nr, pr), jnp.where(take, ni, pi)
        sq_r, sq_i = _cmul(sq_r, sq_i, sq_r, sq_i)
    pk_re_ref[...] = pr
    pk_im_ref[...] = pi
    row = lax.broadcasted_iota(jnp.int32, (SUBLANES, n), 0)
    gr, gi = pr[rows8 - SUBLANES:], pi[rows8 - SUBLANES:]
    for k in range(3):
        keep = row >= (1 << k)
        sg_re_ref[k] = jnp.where(keep, gr, 0.0)
        sg_im_ref[k] = jnp.where(keep, gi, 0.0)
        gr, gi = _cmul(gr, gi, gr, gi)


def _s5_params(lam_re, lam_im, log_dt, b_re, b_im, c_re, c_im, seg_len):
    g, p = lam_re.shape
    c = b_re.shape[2]
    n = g * p
    flat = lambda a: a.reshape(1, n)
    ldt = jnp.broadcast_to(log_dt[:, None], (g, p)).reshape(1, n)
    b_t = lambda a: jnp.tile(jnp.transpose(a, (2, 0, 1)).reshape(c, n), (g, 1))
    c_t = lambda a: jnp.tile(a.reshape(g * c, p), (1, g))
    outs = pl.pallas_call(
        _s5_params_kernel,
        out_shape=[jax.ShapeDtypeStruct((2, g * c, n), BF16), jax.ShapeDtypeStruct((2, g * c, n), BF16),
                   jax.ShapeDtypeStruct((SUBLANES, n), F32), jax.ShapeDtypeStruct((SUBLANES, n), F32),
                   jax.ShapeDtypeStruct((seg_len * SUBLANES, n), F32), jax.ShapeDtypeStruct((seg_len * SUBLANES, n), F32),
                   jax.ShapeDtypeStruct((3, SUBLANES, n), F32), jax.ShapeDtypeStruct((3, SUBLANES, n), F32)],
        compiler_params=pltpu.CompilerParams(vmem_limit_bytes=VMEM_LIMIT),
    )(flat(lam_re), flat(lam_im), ldt, b_t(b_re), b_t(b_im), c_t(c_re), c_t(c_im))
    wb, wct, a_re, a_im, pk_re, pk_im, sg_re, sg_im = outs
    wc = jnp.transpose(wct, (0, 2, 1))
    pk_re, pk_im = (a.reshape(seg_len, SUBLANES, n) for a in (pk_re, pk_im))
    return wb, wc, a_re, a_im, pk_re, pk_im, sg_re, sg_im


def _gelu_tanh(x):
    return 0.5 * x * (1.0 + jnp.tanh(math.sqrt(2.0 / math.pi) * (x + 0.044715 * (x * x * x))))


def _even_kernel(p_ref, band_ref, perm_ref, wpool_ref, pscale_ref, wb_ref, wc_ref, dskip_ref, wglu_ref, bglu_ref,
                 a_re_ref, a_im_ref, pk_re_ref, pk_im_ref, sg_re_ref, sg_im_ref,
                 o_ref, prev_s, xr_s, xi_s, car_re, car_im, *, mix_a, scan_lanes):
    i = pl.program_id(1)
    tt = p_ref.shape[0]

    @pl.when(i == 0)
    def _():
        prev_s[...] = jnp.zeros_like(prev_s)
        car_re[...] = jnp.zeros_like(car_re)
        car_im[...] = jnp.zeros_like(car_im)

    ua = p_ref[:, :mix_a]
    ext = jnp.concatenate([prev_s[...], ua], axis=0)
    t1 = i * tt + lax.broadcasted_iota(jnp.int32, (tt, POOL_CH), 0) + 1
    for g, win in enumerate(POOL_WINDOWS):
        sl = slice(g * POOL_CH, (g + 1) * POOL_CH)
        wsum = _dot(band_ref[g], ext[:, sl])
        cnt = jnp.minimum(t1, win).astype(F32)
        pg = wsum / cnt - ua[:, sl].astype(F32)
        yg = _dot(pg.astype(BF16), wpool_ref[g]) * pscale_ref[:, sl]
        o_ref[:, sl] = yg.astype(o_ref.dtype)
    prev_s[...] = ua

    ub = _dot(perm_ref[0], p_ref[:, mix_a:]).astype(BF16)
    n_tiles = xr_s.shape[0]
    n_diag_blocks = ub.shape[1] // LANES
    tpb = n_tiles // n_diag_blocks
    sw = tpb * LANES
    for j in range(n_diag_blocks):
        cs, ss = slice(j * LANES, (j + 1) * LANES), slice(j * sw, (j + 1) * sw)
        xr, xi = _dot(ub[:, cs], wb_ref[0, cs, ss]), _dot(ub[:, cs], wb_ref[1, cs, ss])
        for q in range(tpb):
            xr_s[j * tpb + q] = xr[:, q * LANES:(q + 1) * LANES]
            xi_s[j * tpb + q] = xi[:, q * LANES:(q + 1) * LANES]

    seg_len = tt // SUBLANES
    seg_rows = lambda k: pl.ds(pl.multiple_of(k * SUBLANES, SUBLANES), SUBLANES)
    first_seg = lax.broadcasted_iota(jnp.int32, (SUBLANES, LANES), 0) == 0
    zeros = jnp.zeros((SUBLANES, LANES), F32)
    tpc = scan_lanes // LANES
    for c in range(n_tiles // tpc):
        tiles = list(range(c * tpc, (c + 1) * tpc))
        lanes = [slice(t * LANES, (t + 1) * LANES) for t in tiles]

        def local(k, carry, tiles=tiles, lanes=lanes):
            out = []
            for t, ls, (hr, hi) in zip(tiles, lanes, carry):
                dr, di = _cmul(a_re_ref[:, ls], a_im_ref[:, ls], hr, hi)
                hr, hi = dr + xr_s[t, seg_rows(k), :], di + xi_s[t, seg_rows(k), :]
                xr_s[t, seg_rows(k), :] = hr
                xi_s[t, seg_rows(k), :] = hi
                out.append((hr, hi))
            return tuple(out)

        ends = lax.fori_loop(0, seg_len, local, tuple((zeros, zeros) for _ in tiles))
        starts = []
        for t, ls, (er, ei) in zip(tiles, lanes, ends):
            fr = jnp.where(first_seg, car_re[:, ls], pltpu.roll(er, 1, 0))
            fi = jnp.where(first_seg, car_im[:, ls], pltpu.roll(ei, 1, 0))
            for k in range(3):
                dr, di = _cmul(sg_re_ref[k, :, ls], sg_im_ref[k, :, ls],
                               pltpu.roll(fr, 1 << k, 0), pltpu.roll(fi, 1 << k, 0))
                fr, fi = fr + dr, fi + di
            starts.append((fr, fi))
            dr, di = _cmul(pk_re_ref[seg_len - 1, :, ls], pk_im_ref[seg_len - 1, :, ls], fr, fi)
            car_re[:, ls] = jnp.broadcast_to((dr + er)[SUBLANES - 1:], fr.shape)
            car_im[:, ls] = jnp.broadcast_to((di + ei)[SUBLANES - 1:], fr.shape)

        def fix(k, carry, tiles=tiles, lanes=lanes, starts=starts):
            for t, ls, (fr, fi) in zip(tiles, lanes, starts):
                dr, di = _cmul(pk_re_ref[k, :, ls], pk_im_ref[k, :, ls], fr, fi)
                xr_s[t, seg_rows(k), :] += dr
                xi_s[t, seg_rows(k), :] += di
            return carry

        lax.fori_loop(0, seg_len, fix, 0)

    state = lambda ref, j: jnp.concatenate([ref[j * tpb + q] for q in range(tpb)], axis=1).astype(BF16)
    y = jnp.concatenate(
        [_dot(state(xr_s, j), wc_ref[0, j * sw:(j + 1) * sw, j * LANES:(j + 1) * LANES])
         - _dot(state(xi_s, j), wc_ref[1, j * sw:(j + 1) * sw, j * LANES:(j + 1) * LANES])
         for j in range(n_diag_blocks)], axis=1)
    y = _gelu_tanh(y + dskip_ref[...] * ub.astype(F32))
    z = _dot(y.astype(BF16), wglu_ref[...]) + bglu_ref[...]
    out = (y * (1.0 / (1.0 + jnp.exp(-z)))).astype(o_ref.dtype)
    o_ref[:, mix_a:] = _dot(perm_ref[1], out).astype(o_ref.dtype)


def _segment_perm(tt):
    seg_len = tt // SUBLANES
    r = jnp.arange(tt)
    src = (r % SUBLANES) * seg_len + r // SUBLANES
    p = (src[:, None] == jnp.arange(tt)[None, :])
    return jnp.stack([p, p.T]).astype(BF16)


def _pool_band(tt):
    t = jnp.arange(tt)[:, None] + tt
    j = jnp.arange(2 * tt)[None, :]
    return jnp.stack([((j <= t) & (j > t - w)) for w in POOL_WINDOWS]).astype(BF16)


def _even_mixer(proj, wpool, pscale, s5, dskip, wglu, bglu, tt, scan_lanes):
    bsz, seq, width = proj.shape
    wb, wc, a_re, a_im, pk_re, pk_im, sg_re, sg_im = s5
    mix_a = len(POOL_WINDOWS) * POOL_CH
    mix_b = width - mix_a
    n_state = wb.shape[2]
    const = lambda a: pl.BlockSpec(a.shape, lambda b, i, nd=a.ndim: (0,) * nd)
    band = _pool_band(tt)
    args = (band, _segment_perm(tt), wpool, pscale, wb, wc, dskip, wglu, bglu, a_re, a_im, pk_re, pk_im, sg_re, sg_im)
    return pl.pallas_call(
        functools.partial(_even_kernel, mix_a=mix_a, scan_lanes=scan_lanes),
        grid=(bsz, seq // tt),
        in_specs=[pl.BlockSpec((None, tt, width), lambda b, i: (b, i, 0))] + [const(a) for a in args],
        out_specs=pl.BlockSpec((None, tt, width), lambda b, i: (b, i, 0)),
        out_shape=jax.ShapeDtypeStruct((bsz, seq, width), BF16),
        scratch_shapes=[pltpu.VMEM((tt, mix_a), BF16),
                        pltpu.VMEM((n_state // LANES, tt, LANES), F32), pltpu.VMEM((n_state // LANES, tt, LANES), F32),
                        pltpu.VMEM((SUBLANES, n_state), F32), pltpu.VMEM((SUBLANES, n_state), F32)],
        compiler_params=_cparams(("parallel", "arbitrary")),
    )(proj, *args)


def _pair_blockdiag(x, first_ref, second_ref):
    return jnp.concatenate([x * first_ref[...], x * second_ref[...]], axis=0)


def _dsa_kernel(q_ref, qi_ref, wq_ref, k_ref, v_ref, kk_ref, tri_ref, ones_ref, first_ref, second_ref, alibi_ref,
                o_ref,
                s_s, st_s, qs_s, mx_s, al_s, *, topk, qb):
    i = pl.program_id(1)
    kb_size = ATT_BLOCK
    n_diag = qb // kb_size
    nkb = (i + 1) * n_diag
    shape = (qb, kb_size)
    row = lax.broadcasted_iota(jnp.int32, shape, 0)
    col = lax.broadcasted_iota(jnp.int32, shape, 1)
    q_chunk = (i * qb + row) // CHUNK
    bcast = lambda a: jnp.broadcast_to(a, shape)
    kf = float(topk)
    pairs = [slice(p * LANES, (p + 1) * LANES) for p in range(N_HEADS // 2)]

    wq = wq_ref[...]
    w_cols = [bcast(wq[:, h:h + 1] * IDX_SCALE) for h in range(IDX_HEADS)]

    def indexer(kb, rows):
        kbd = _pair_blockdiag(kk_ref[kb], first_ref, second_ref)
        acc = jnp.zeros((rows.stop - rows.start, kb_size), F32)
        for p, ps in enumerate(pairs):
            rel = jnp.maximum(_dot_nt(qi_ref[rows, ps], kbd), 0.0)
            acc = acc + rel[:, :kb_size] * w_cols[2 * p][rows] + rel[:, kb_size:] * w_cols[2 * p + 1][rows]
        return acc

    def store_scores(kb, r0, scores):
        s_s[kb, r0:, :] = scores
        for c in range(r0 // kb_size, qb // kb_size):
            st_s[kb, :, c * kb_size:(c + 1) * kb_size] = scores[c * kb_size - r0:(c + 1) * kb_size - r0].T
        if r0:
            s_s[kb, :r0, :] = jnp.full((r0, kb_size), -jnp.inf, F32)
            st_s[kb, :, :r0] = jnp.full((kb_size, r0), -jnp.inf, F32)

    def past_block(kb, carry):
        rmax, rmin = carry
        acc = indexer(kb, slice(0, qb))
        store_scores(kb, 0, acc)
        return jnp.maximum(rmax, acc), jnp.minimum(rmin, acc)

    rmax, rmin = lax.fori_loop(0, i * n_diag, past_block,
                               (jnp.full(shape, -jnp.inf, F32), jnp.full(shape, jnp.inf, F32)))
    for u in range(n_diag):
        kb, r0 = i * n_diag + u, u * kb_size
        rows = slice(r0, qb)
        acc = indexer(kb, rows)
        adm = (((kb * kb_size + col) // CHUNK) <= q_chunk)[rows]
        store_scores(kb, r0, jnp.where(adm, acc, -jnp.inf))
        hi_part = jnp.maximum(rmax[rows], jnp.where(adm, acc, -jnp.inf))
        lo_part = jnp.minimum(rmin[rows], jnp.where(adm, acc, jnp.inf))
        if r0:
            hi_part = jnp.concatenate([rmax[:r0], hi_part], axis=0)
            lo_part = jnp.concatenate([rmin[:r0], lo_part], axis=0)
        rmax, rmin = hi_part, lo_part
    n_chunks = qb // kb_size
    chunk_rows = lambda c: slice(c * kb_size, (c + 1) * kb_size)
    chunk_subl = lambda c: slice(c * SUBLANES, (c + 1) * SUBLANES)
    tile = (kb_size, kb_size)

    def to_dense(rep):
        return rep.T[:SUBLANES]

    def to_rows(dense):
        return jnp.broadcast_to(dense[:1], tile).T

    def dense_of(fn):
        return jnp.concatenate([fn(c) for c in range(n_chunks)], axis=0)

    def fold_keys(c, fn, init):
        def body(g, a):
            for u in range(n_diag):
                a = fn(a, st_s[g * n_diag + u, :, chunk_rows(c)])
            return a
        return lax.fori_loop(0, i + 1, body, init)

    def per_row(dense_c):
        return jnp.broadcast_to(dense_c[:1], tile)

    def over_keys(reduce, folded):
        return jnp.broadcast_to(reduce(folded, axis=0, keepdims=True), (SUBLANES, kb_size))

    def count(x, cmp):
        def one(c):
            xb = per_row(x[chunk_subl(c)])
            return over_keys(jnp.sum, fold_keys(c, lambda a, s: a + jnp.where(cmp(s, xb), 1.0, 0.0),
                                                jnp.zeros(tile, F32)))
        return dense_of(one)

    count_ge = functools.partial(count, cmp=lambda s, xb: s >= xb)
    count_gt = functools.partial(count, cmp=lambda s, xb: s > xb)
    count_eq = functools.partial(count, cmp=lambda s, xb: s == xb)

    smax = dense_of(lambda c: to_dense(jnp.broadcast_to(_rowmax(rmax[chunk_rows(c)]), tile)))
    smin = dense_of(lambda c: to_dense(jnp.broadcast_to(jnp.min(rmin[chunk_rows(c)], axis=1, keepdims=True), tile)))
    dshape = (n_chunks * SUBLANES, kb_size)
    t_dense = (i * qb + (lax.broadcasted_iota(jnp.int32, dshape, 0) // SUBLANES) * kb_size
               + lax.broadcasted_iota(jnp.int32, dshape, 1))
    n_adm = ((t_dense // CHUNK + 1) * CHUNK).astype(F32)
    hi0 = smax + jnp.maximum(jnp.abs(smax) * 1e-6, 1e-30)
    zero = jnp.zeros_like(smin)
    cge0, cgt0 = count_ge(zero), count_gt(zero)
    wide = n_adm > kf
    zero_tie = jnp.logical_and(wide, jnp.logical_and(cgt0 < kf, cge0 >= kf))
    above = jnp.logical_and(wide, cgt0 >= kf)
    below = jnp.logical_and(wide, cge0 < kf)
    lo0 = jnp.where(jnp.logical_or(above, zero_tie), 0.0, smin)
    clo0 = jnp.where(zero_tie, kf, jnp.where(above, cge0, n_adm))
    hi0 = jnp.where(below, 0.0, hi0)
    chi0 = jnp.where(below, cge0, 0.0)
    need0 = jnp.where(zero_tie, kf - cgt0, COUNT_ALL)

    def unresolved(clo, chi):
        return jnp.max(jnp.where(jnp.logical_and(clo > kf, clo - chi > POP_SPAN), 1.0, 0.0))

    def bisect_cond(carry):
        return jnp.logical_and(carry[0] > 0.0, carry[1] < BISECT_ITERS)

    def bisect(carry):
        _, it, lo, hi, clo, chi = carry
        span = clo - chi
        aim = (clo - (kf + 0.5)) / span
        interpolate = jnp.logical_and(span < INTERP_SPAN, it % 2 == 0)
        mid = lo + jnp.where(interpolate, aim, 0.5) * (hi - lo)
        c = count_ge(mid)
        active = clo > kf
        up = jnp.logical_and(active, c >= kf)
        down = jnp.logical_and(active, c < kf)
        clo = jnp.where(up, c, clo)
        chi = jnp.where(down, c, chi)
        return (unresolved(clo, chi), it + 1, jnp.where(up, mid, lo), jnp.where(down, mid, hi), clo, chi)

    _, _, lo, hi, clo, chi = lax.while_loop(
        bisect_cond, bisect, (unresolved(clo0, chi0), jnp.int32(0), lo0, hi0, clo0, chi0))

    done0 = jnp.where(clo <= kf, 1.0, 0.0)

    def pop_cond(carry):
        return jnp.logical_and(carry[0] > 0.0, carry[1] < MAX_POPS)

    def next_below(hi):
        def one(c):
            hib = per_row(hi[chunk_subl(c)])
            return over_keys(jnp.max, fold_keys(c, lambda a, s: jnp.maximum(a, jnp.where(s < hib, s, -jnp.inf)),
                                                jnp.full(tile, -jnp.inf, F32)))
        return dense_of(one)

    def pop(carry):
        _, it, hi, chi, thr, need, done = carry
        v = next_below(hi)
        cv = count_eq(v)
        active = done < 0.5
        fin = jnp.logical_and(active, chi + cv >= kf)
        go = jnp.logical_and(active, chi + cv < kf)
        thr = jnp.where(fin, v, thr)
        need = jnp.where(fin, kf - chi, need)
        done = jnp.where(fin, 1.0, done)
        hi = jnp.where(go, v, hi)
        chi = jnp.where(go, chi + cv, chi)
        return (jnp.max(1.0 - done), it + 1, hi, chi, thr, need, done)

    left0 = jnp.max(1.0 - done0)
    _, _, _, _, thr, need, _ = lax.while_loop(
        pop_cond, pop, (left0, jnp.int32(0), hi, chi, lo, need0, done0))
    thrb = jnp.concatenate([to_rows(thr[chunk_subl(c)]) for c in range(n_chunks)], axis=0)
    needb = jnp.concatenate([to_rows(need[chunk_subl(c)]) for c in range(n_chunks)], axis=0)
    partial_ties = jnp.max(jnp.where(need < COUNT_ALL, 1.0, 0.0))

    @pl.when(partial_ties > 0.0)
    def _():
        def tie_block(kb, run):
            s = s_s[kb]
            eq = jnp.where(s == thrb, 1.0, 0.0).astype(BF16)
            rank = _dot(eq, tri_ref[1]) + run
            take = jnp.where(s == thrb, jnp.where(rank < needb, 0.0, NEG_BIG), NEG_BIG)
            s_s[kb] = jnp.where(s > thrb, 0.0, take)
            return run + _dot(eq, tri_ref[2])
        lax.fori_loop(0, nkb, tie_block, jnp.zeros(shape, F32))

    @pl.when(partial_ties <= 0.0)
    def _():
        def sel_block(kb, carry):
            s_s[kb] = jnp.where(s_s[kb] >= thrb, 0.0, NEG_BIG)
            return carry
        lax.fori_loop(0, nkb, sel_block, 0)

    first_half = col < HEAD_DIM
    qs_s[...] = q_ref[...] * (HEAD_DIM ** -0.5)
    mx_s[...] = jnp.full(mx_s.shape, NEG_BIG, F32)
    al_s[...] = jnp.zeros_like(al_s)
    slopes = [2.0 ** (-8.0 * (h + 1) / N_HEADS) for h in range(N_HEADS)]

    q_alibi = jnp.where(col < 4, 1.0, 0.0).astype(BF16)

    def attend_group(g, past):
        blocks = [g * n_diag + u for u in range(n_diag)]
        if not past:
            t_pos = (i * qb + row).astype(F32)
            hats = []
            for kb in blocks:
                s_pos = (kb * kb_size + col).astype(F32)
                hats.append(jnp.minimum(s_pos, 2.0 * t_pos - s_pos))
        for p, ps in enumerate(pairs):
            tiles = []
            for u, kb in enumerate(blocks):
                kbd = _pair_blockdiag(k_ref[kb, :, ps], first_ref, second_ref)
                if past:
                    mask = s_s[kb]
                    k_alibi = alibi_ref[0, p] + alibi_ref[1, p] * kb.astype(BF16)
                    logits = _dot_nt(jnp.concatenate([qs_s[:, ps], q_alibi], axis=1),
                                     jnp.concatenate([kbd, k_alibi], axis=1))
                    tiles.append((0, logits + jnp.concatenate([mask, mask], axis=1)))
                else:
                    r0 = u * kb_size
                    mask, hat = s_s[kb, r0:, :], hats[u][r0:]
                    bias = jnp.concatenate([mask + slopes[2 * p] * hat, mask + slopes[2 * p + 1] * hat], axis=1)
                    tiles.append((r0, _dot_nt(qs_s[r0:, ps], kbd) + bias))

            def padded(r0, part, fill):
                return part if r0 == 0 else jnp.concatenate(
                    [jnp.full((r0, part.shape[1]), fill, part.dtype), part], axis=0)

            top = functools.reduce(jnp.maximum, [padded(r0, t, NEG_BIG) for r0, t in tiles])
            m_old = mx_s[p]
            m_new = jnp.maximum(m_old, jnp.concatenate(
                [bcast(_rowmax(top[:, :kb_size])), bcast(_rowmax(top[:, kb_size:]))], axis=1))
            mx_s[p] = m_new
            pexp = jnp.concatenate(
                [padded(r0, jnp.exp((t - m_new[r0:]).astype(BF16)), 0.0) for r0, t in tiles], axis=1)
            shrink = m_old - m_new
            alpha = jnp.exp(jnp.where(first_half, shrink[:, :kb_size], shrink[:, kb_size:]))
            values = jnp.concatenate(
                [jnp.concatenate([_pair_blockdiag(v_ref[kb, :, ps], first_ref, second_ref), ones_ref[...]], axis=1)
                 for kb in blocks], axis=0)
            al_s[p] = jnp.concatenate([alpha, alpha], axis=1) * al_s[p] + _dot(pexp, values)

    def past_group(g, carry):
        attend_group(g, True)
        return carry

    lax.fori_loop(0, i, past_group, 0)
    attend_group(i, False)
    for p, ps in enumerate(pairs):
        state = al_s[p]
        o_ref[:, ps] = (state[:, :LANES] / state[:, LANES:]).astype(o_ref.dtype)


def _tri_consts(n):
    j = jnp.arange(n)[:, None]
    s = jnp.arange(n)[None, :]
    return jnp.stack([j > s, j < s, jnp.ones((n, n), bool)]).astype(BF16)


def _pair_consts(n):
    tri = _tri_consts(n)
    zero = jnp.zeros((n, n), BF16)
    tri2 = jnp.block([[tri[0], zero], [zero, tri[0]]])
    first = (jnp.arange(n)[None, :] < n // 2) & jnp.ones((n, 1), bool)
    ones2 = jnp.concatenate([first, ~first], axis=0).astype(BF16)
    return tri2, ones2, first.astype(BF16), (~first).astype(BF16)


def _alibi_key_terms(n):
    r = jnp.arange(2 * n)[:, None]
    lane = jnp.arange(LANES)[None, :]
    head_b = r >= n
    out = []
    for p in range(N_HEADS // 2):
        slope = jnp.where(head_b, 2.0 ** (-8.0 * (2 * p + 2) / N_HEADS), 2.0 ** (-8.0 * (2 * p + 1) / N_HEADS))
        in_block = jnp.where(lane == jnp.where(head_b, 3, 1), slope * (r % n), 0.0)
        per_block = jnp.where(lane == jnp.where(head_b, 2, 0), slope * n, 0.0)
        out.append(jnp.stack([in_block, per_block]))
    return jnp.stack(out, axis=1).astype(BF16)


def _dsa(q, k, v, qi, kk, wq, topk, qb):
    bsz, seq, width = q.shape
    nq, nk = seq // qb, seq // ATT_BLOCK
    qblk = lambda a: a.reshape(bsz, nq, qb, a.shape[-1])
    kblk = lambda a: a.reshape(bsz, nk, ATT_BLOCK, a.shape[-1])
    qspec = lambda wd: pl.BlockSpec((None, None, qb, wd), lambda b, i: (b, i, 0, 0))
    kspec = lambda wd: pl.BlockSpec((None, nk, ATT_BLOCK, wd), lambda b, i: (b, 0, 0, 0))
    _, ones2, first, second = _pair_consts(ATT_BLOCK)
    consts = (_tri_consts(ATT_BLOCK), ones2, first, second, _alibi_key_terms(ATT_BLOCK))
    out = pl.pallas_call(
        functools.partial(_dsa_kernel, topk=topk, qb=qb),
        grid=(bsz, nq),
        in_specs=[qspec(width), qspec(width), qspec(LANES), kspec(width), kspec(width), kspec(LANES)]
                 + [pl.BlockSpec(a.shape, lambda b, i, nd=a.ndim: (0,) * nd) for a in consts],
        out_specs=qspec(width),
        out_shape=jax.ShapeDtypeStruct((bsz, nq, qb, width), BF16),
        scratch_shapes=[pltpu.VMEM((nk, qb, ATT_BLOCK), F32), pltpu.VMEM((nk, ATT_BLOCK, qb), F32),
                        pltpu.VMEM((qb, width), BF16),
                        pltpu.VMEM((N_HEADS // 2, qb, 2 * ATT_BLOCK), F32),
                        pltpu.VMEM((N_HEADS // 2, qb, 2 * LANES), F32)],
        compiler_params=_cparams(("parallel", "arbitrary")),
    )(qblk(q), qblk(qi), qblk(wq), kblk(k), kblk(v), kblk(kk), *consts)
    return out.reshape(bsz, seq, width)


def _sb_kernel(q_ref, k_ref, v_ref, tri_ref, ones_ref, first_ref, second_ref, o_ref, qs_s, rem_s, acc_s, *, qb):
    i = pl.program_id(1)
    kb_size = ATT_BLOCK
    n_diag = qb // kb_size
    n_pairs = N_HEADS // 2
    shape = (qb, 2 * kb_size)
    row = lax.broadcasted_iota(jnp.int32, shape, 0)
    col = lax.broadcasted_iota(jnp.int32, shape, 1) % kb_size
    qs_s[...] = q_ref[...] * (HEAD_DIM ** -0.5)
    rem_s[...] = jnp.ones_like(rem_s)
    acc_s[...] = jnp.zeros_like(acc_s)

    def step(kb, masked, r0=0):
        rows = slice(r0, qb)
        causal = ((kb * kb_size + col) < (i * qb + row))[rows]
        pairs = [slice(p * LANES, (p + 1) * LANES) for p in range(n_pairs)]
        log_sig, log_keep = [], []
        for ps in pairs:
            kbd = _pair_blockdiag(k_ref[kb, :, ps], first_ref, second_ref)
            z = _dot_nt(qs_s[rows, ps], kbd).astype(BF16)
            soft = jnp.log(1.0 + jnp.exp(-jnp.abs(z)))
            keep = -jnp.maximum(z, 0.0) - soft
            log_sig.append(jnp.minimum(z, 0.0) - soft)
            log_keep.append(jnp.where(causal, keep, jnp.zeros_like(keep)) if masked else keep)
        for p, ps in enumerate(pairs):
            after = _dot(log_keep[p], tri_ref[...]).astype(BF16)
            w = jnp.exp(log_sig[p] + after)
            if masked:
                w = jnp.where(causal, w, jnp.zeros_like(w))
            vbd = _pair_blockdiag(v_ref[kb, :, ps], first_ref, second_ref)
            out = _dot(w, jnp.concatenate([vbd, ones_ref[...]], axis=1))
            rem = rem_s[p, rows]
            acc_s[p, rows] += rem * out[:, :LANES]
            rem_s[p, rows] = rem * jnp.maximum(1.0 - out[:, LANES:], 0.0)

    for u in reversed(range(n_diag)):
        step(i * n_diag + u, True, u * kb_size)

    def past_step(j, carry):
        step(i * n_diag - 1 - j, False)
        return carry

    lax.fori_loop(0, i * n_diag, past_step, 0)
    for p in range(n_pairs):
        o_ref[:, p * LANES:(p + 1) * LANES] = acc_s[p].astype(o_ref.dtype)


def _stick_breaking(q, k, v, qb):
    bsz, seq, width = q.shape
    nq, nk = seq // qb, seq // ATT_BLOCK
    qspec = pl.BlockSpec((None, None, qb, width), lambda b, i: (b, i, 0, 0))
    kspec = pl.BlockSpec((None, nk, ATT_BLOCK, width), lambda b, i: (b, 0, 0, 0))
    consts = _pair_consts(ATT_BLOCK)
    out = pl.pallas_call(
        functools.partial(_sb_kernel, qb=qb),
        grid=(bsz, nq),
        in_specs=[qspec, kspec, kspec] + [pl.BlockSpec(a.shape, lambda b, i: (0, 0)) for a in consts],
        out_specs=qspec,
        out_shape=jax.ShapeDtypeStruct((bsz, nq, qb, width), BF16),
        scratch_shapes=[pltpu.VMEM((qb, width), BF16),
                        pltpu.VMEM((N_HEADS // 2, qb, LANES), F32),
                        pltpu.VMEM((N_HEADS // 2, qb, LANES), F32)],
        compiler_params=_cparams(("parallel", "arbitrary")),
    )(q.reshape(bsz, nq, qb, width), k.reshape(bsz, nk, ATT_BLOCK, width),
      v.reshape(bsz, nk, ATT_BLOCK, width), *consts)
    return out.reshape(bsz, seq, width)


def _cd_weight(cd_w_in):
    cw = N_HEADS * HEAD_DIM
    iw = IDX_HEADS * IDX_DIM
    offs = [0, cw, 2 * cw, 3 * cw, 3 * cw + iw, 3 * cw + iw + IDX_DIM, 3 * cw + iw + IDX_DIM + IDX_HEADS]
    qkv_c = cd_w_in[:, :offs[3]]
    q_idx = cd_w_in[:, offs[3]:offs[4]]
    k_idx = cd_w_in[:, offs[4]:offs[5]]
    w_idx = cd_w_in[:, offs[5]:offs[6]]
    qkv_d = cd_w_in[:, offs[6]:]
    pad = jnp.zeros((cd_w_in.shape[0], LANES - w_idx.shape[1]), cd_w_in.dtype)
    return jnp.concatenate([qkv_c, q_idx, qkv_d, k_idx, k_idx, w_idx, pad], axis=1)


def kernel(x, c, norm_g, ada_w, ada_b, mlp_w1, mlp_w2, ab_w_in, ab_w_out, pool_w, pool_scale, ssm_lam_re, ssm_lam_im, ssm_log_dt, ssm_b_re, ssm_b_im, ssm_c_re, ssm_c_im, ssm_d, glu_w, glu_b, cd_w_in, cd_w_out, final_g):
    bsz, seq, d = x.shape
    depth = norm_g.shape[0]
    topk = min(TOPK_MAX, seq // 4)
    tm = min(512, seq)

    c_pad = jnp.zeros((SUBLANES, d), F32).at[:bsz].set(c)
    mod = _modulation(c_pad, ada_w.reshape(depth * 2, d, 3 * d), ada_b.reshape(depth * 2, 1, 3 * d))
    mod = mod[:, :, :bsz, None, :]

    for i in range(depth):
        j = i // 2
        shift, scale1, gate1 = mod[2 * i, 0], mod[2 * i, 1], mod[2 * i, 2]
        g = norm_g[i, 0][None, :]
        if i % 2 == 0:
            width = ab_w_in.shape[2]
            tt = min(256, seq)
            (proj,) = _norm_proj(x, g, shift, scale1, ab_w_in[j].astype(BF16), (width,), (BF16,), tm)
            s5 = _s5_params(ssm_lam_re[j], ssm_lam_im[j], ssm_log_dt[j],
                            ssm_b_re[j], ssm_b_im[j], ssm_c_re[j], ssm_c_im[j], seg_len=tt // SUBLANES)
            y = _even_mixer(proj, pool_w[j].astype(BF16), pool_scale[j][None, :], s5,
                            ssm_d[j][None, :], glu_w[j].astype(BF16), glu_b[j][None, :],
                            tt=tt, scan_lanes=512)
            ys, w_out = (y,), ab_w_out[j]
        else:
            hw = N_HEADS * HEAD_DIM
            widths = (hw,) * 7 + (LANES, LANES)
            dtypes = (BF16,) * 8 + (F32,)
            qc, kc, vc, qi, qd, kd, vd, kk, wq = _norm_proj(
                x, g, shift, scale1, _cd_weight(cd_w_in[j]).astype(BF16), widths, dtypes, tm)
            oc = _dsa(qc, kc, vc, qi, kk, wq, topk, qb=ATT_QUERY_BLOCK)
            od = _stick_breaking(qd, kd, vd, qb=ATT_QUERY_BLOCK)
            ys, w_out = (oc, od), cd_w_out[j]
        shift, scale1, gate2 = mod[2 * i + 1, 0], mod[2 * i + 1, 1], mod[2 * i + 1, 2]
        x = _out_mlp(x, gate1, ys, w_out.astype(BF16), norm_g[i, 1][None, :], shift, scale1, gate2,
                     mlp_w1[i].astype(BF16), mlp_w2[i].astype(BF16), final_g[None, :],
                     final_norm=(i == depth - 1), tm=min(1024, seq), tf=1024)
    return x
```

```python
import functools
import math

import jax
import jax.numpy as jnp
from jax import lax
from jax.experimental import pallas as pl
from jax.experimental.pallas import tpu as pltpu

F32 = jnp.float32
BF16 = jnp.bfloat16

EPS = 1e-6
CHUNK = 64
POOL_WINDOWS = (2, 4, 8, 16)
POOL_CH = 128
SSM_GROUP = 16
SSM_STATE = 64
HEAD_DIM = 64
N_HEADS = 8
IDX_HEADS = 8
IDX_DIM = 64
IDX_SCALE = (IDX_HEADS ** -0.5) * (IDX_DIM ** -0.5)
TOPK_MAX = 256

LANES = 128
SUBLANES = 8
VMEM_LIMIT = 56 * 1024 * 1024

PROJ_ROWS = 512
MIXER_ROWS = 256
SCAN_LANES = 512
MLP_ROWS = 1024
MLP_FF = 1024
ATT_BLOCK = 128
ATT_QUERY_BLOCK = 512

NEG_BIG = -1e30
COUNT_ALL = 1e9
BISECT_ITERS = 28
POP_SPAN = 2.0
INTERP_SPAN = 256.0
MAX_POPS = 4096
BRACKET_MARGIN = 1e-6
ALIBI_LANES = 4


def _cparams(sem):
    return pltpu.CompilerParams(dimension_semantics=sem, vmem_limit_bytes=VMEM_LIMIT)


def _dot(a, b):
    return jnp.dot(a, b, preferred_element_type=F32)


def _dot_nt(a, b):
    return lax.dot_general(a, b, (((1,), (1,)), ((), ())), preferred_element_type=F32)


def _rowmax(x):
    return jnp.max(x, axis=1, keepdims=True)


def _norm_modulate(x, g, shift, scale1):
    ms = jnp.mean(x * x, axis=-1, keepdims=True)
    return (x * lax.rsqrt(ms + EPS)) * g * scale1 + shift


def _mod_kernel(c_ref, w_ref, b_ref, o_ref):
    j = pl.program_id(1)
    m = _dot(c_ref[...], w_ref[...]) + b_ref[...]
    o_ref[...] = m + jnp.where(j > 0, 1.0, 0.0)


def _modulation(c_pad, ada_w, ada_b):
    n_sub, d, _ = ada_w.shape
    rows = c_pad.shape[0]
    return pl.pallas_call(
        _mod_kernel,
        grid=(n_sub, 3),
        in_specs=[
            pl.BlockSpec((rows, d), lambda s, j: (0, 0)),
            pl.BlockSpec((None, d, d), lambda s, j: (s, 0, j)),
            pl.BlockSpec((None, 1, d), lambda s, j: (s, 0, j)),
        ],
        out_specs=pl.BlockSpec((None, None, rows, d), lambda s, j: (s, j, 0, 0)),
        out_shape=jax.ShapeDtypeStruct((n_sub, 3, rows, d), F32),
        compiler_params=_cparams(("arbitrary", "arbitrary")),
    )(c_pad, ada_w, ada_b)


def _norm_proj_kernel(x_ref, g_ref, shift_ref, scale_ref, w_ref, *o_refs, widths):
    h = _norm_modulate(x_ref[...], g_ref[...], shift_ref[...], scale_ref[...]).astype(BF16)
    y = _dot(h, w_ref[...])
    off = 0
    for o_ref, width in zip(o_refs, widths):
        o_ref[...] = y[:, off:off + width].astype(o_ref.dtype)
        off += width


def _norm_proj(x, g, shift, scale1, w, widths, dtypes, tm):
    bsz, seq, d = x.shape
    n = w.shape[1]
    vec = pl.BlockSpec((None, 1, d), lambda b, i: (b, 0, 0))
    return pl.pallas_call(
        functools.partial(_norm_proj_kernel, widths=widths),
        grid=(bsz, seq // tm),
        in_specs=[
            pl.BlockSpec((None, tm, d), lambda b, i: (b, i, 0)),
            pl.BlockSpec((1, d), lambda b, i: (0, 0)),
            vec, vec,
            pl.BlockSpec((d, n), lambda b, i: (0, 0)),
        ],
        out_specs=[pl.BlockSpec((None, tm, wd), lambda b, i: (b, i, 0)) for wd in widths],
        out_shape=[jax.ShapeDtypeStruct((bsz, seq, wd), dt) for wd, dt in zip(widths, dtypes)],
        compiler_params=_cparams(("parallel", "parallel")),
    )(x, g, shift, scale1, w)


def _out_mlp_kernel(x_ref, gate_mix_ref, *refs, n_y, final_norm):
    y_refs = refs[:n_y]
    w_out_ref, g_ref, shift_ref, scale_ref, gate_ref, w1_ref, w2_ref, fg_ref, o_ref, h_s, acc_s = refs[n_y:]
    f = pl.program_id(2)

    @pl.when(f == 0)
    def _():
        y = jnp.concatenate([r[...] for r in y_refs], axis=1) if n_y > 1 else y_refs[0][...]
        x1 = x_ref[...] + gate_mix_ref[...] * _dot(y, w_out_ref[...])
        o_ref[...] = x1
        h_s[...] = _norm_modulate(x1, g_ref[...], shift_ref[...], scale_ref[...]).astype(BF16)
        acc_s[...] = jnp.zeros_like(acc_s)

    a = jnp.maximum(_dot(h_s[...], w1_ref[...]), 0.0)
    acc_s[...] += _dot((a * a).astype(BF16), w2_ref[...])

    @pl.when(f == pl.num_programs(2) - 1)
    def _():
        y = o_ref[...] + gate_ref[...] * acc_s[...]
        if final_norm:
            ms = jnp.mean(y * y, axis=-1, keepdims=True)
            y = (y * lax.rsqrt(ms + EPS)) * fg_ref[...]
        o_ref[...] = y


def _out_mlp(x, gate_mix, ys, w_out, g, shift, scale1, gate1, w1, w2, final_g, final_norm, tm, tf):
    bsz, seq, d = x.shape
    dff = w1.shape[1]
    row = lambda wd: pl.BlockSpec((None, tm, wd), lambda b, i, f: (b, i, 0))
    vec = pl.BlockSpec((None, 1, d), lambda b, i, f: (b, 0, 0))
    one = pl.BlockSpec((1, d), lambda b, i, f: (0, 0))
    return pl.pallas_call(
        functools.partial(_out_mlp_kernel, n_y=len(ys), final_norm=final_norm),
        grid=(bsz, seq // tm, dff // tf),
        in_specs=[
            row(d), vec, *[row(y.shape[2]) for y in ys],
            pl.BlockSpec(w_out.shape, lambda b, i, f: (0, 0)),
            one, vec, vec, vec,
            pl.BlockSpec((d, tf), lambda b, i, f: (0, f)),
            pl.BlockSpec((tf, d), lambda b, i, f: (f, 0)),
            one,
        ],
        out_specs=row(d),
        out_shape=jax.ShapeDtypeStruct((bsz, seq, d), F32),
        scratch_shapes=[pltpu.VMEM((tm, d), BF16), pltpu.VMEM((tm, d), F32)],
        compiler_params=_cparams(("parallel", "parallel", "arbitrary")),
    )(x, gate_mix, *ys, w_out, g, shift, scale1, gate1, w1, w2, final_g)


def _cmul(ar, ai, br, bi):
    return ar * br - ai * bi, ar * bi + ai * br


def _s5_params_kernel(lr_ref, li_ref, ldt_ref, brt_ref, bit_ref, crt_ref, cit_ref,
                      wb_ref, wc_ref, a_re_ref, a_im_ref, pk_re_ref, pk_im_ref, sg_re_ref, sg_im_ref):
    lr, li = lr_ref[...], li_ref[...]
    dt = jnp.exp(ldt_ref[...])
    mag = jnp.exp(lr * dt)
    ar, ai = mag * jnp.cos(li * dt), mag * jnp.sin(li * dt)
    den = lr * lr + li * li
    zr = ((ar - 1.0) * lr + ai * li) / den
    zi = (ai * lr - (ar - 1.0) * li) / den

    rows, cols = brt_ref.shape
    rgrp = lax.broadcasted_iota(jnp.int32, (rows, cols), 0) // SSM_GROUP
    cgrp = lax.broadcasted_iota(jnp.int32, (rows, cols), 1) // SSM_STATE
    diag = rgrp == cgrp
    br, bi = brt_ref[...], bit_ref[...]
    wb_ref[0] = jnp.where(diag, zr * br - zi * bi, 0.0).astype(wb_ref.dtype)
    wb_ref[1] = jnp.where(diag, zr * bi + zi * br, 0.0).astype(wb_ref.dtype)
    wc_ref[0] = jnp.where(diag, crt_ref[...], 0.0).astype(wc_ref.dtype)
    wc_ref[1] = jnp.where(diag, cit_ref[...], 0.0).astype(wc_ref.dtype)

    n = lr.shape[1]
    rows8 = pk_re_ref.shape[0]
    seg_len = rows8 // SUBLANES
    a_re_ref[...] = jnp.broadcast_to(ar, (SUBLANES, n))
    a_im_ref[...] = jnp.broadcast_to(ai, (SUBLANES, n))
    expo = lax.broadcasted_iota(jnp.int32, (rows8, n), 0) // SUBLANES + 1
    sq_r, sq_i = jnp.broadcast_to(ar, (rows8, n)), jnp.broadcast_to(ai, (rows8, n))
    pr, pi = jnp.ones((rows8, n), F32), jnp.zeros((rows8, n), F32)
    for b in range(seg_len.bit_length()):
        nr, ni = _cmul(pr, pi, sq_r, sq_i)
        take = ((expo >> b) & 1) == 1
        pr, pi = jnp.where(take, nr, pr), jnp.where(take, ni, pi)
        sq_r, sq_i = _cmul(sq_r, sq_i, sq_r, sq_i)
    pk_re_ref[...] = pr
    pk_im_ref[...] = pi
    row = lax.broadcasted_iota(jnp.int32, (SUBLANES, n), 0)
    gr, gi = pr[rows8 - SUBLANES:], pi[rows8 - SUBLANES:]
    for k in range(3):
        keep = row >= (1 << k)
        sg_re_ref[k] = jnp.where(keep, gr, 0.0)
        sg_im_ref[k] = jnp.where(keep, gi, 0.0)
        gr, gi = _cmul(gr, gi, gr, gi)


def _s5_params(lam_re, lam_im, log_dt, b_re, b_im, c_re, c_im, seg_len):
    g, p = lam_re.shape
    c = b_re.shape[2]
    n = g * p
    flat = lambda a: a.reshape(1, n)
    ldt = jnp.broadcast_to(log_dt[:, None], (g, p)).reshape(1, n)
    b_t = lambda a: jnp.tile(jnp.transpose(a, (2, 0, 1)).reshape(c, n), (g, 1))
    c_t = lambda a: jnp.tile(a.reshape(g * c, p), (1, g))
    outs = pl.pallas_call(
        _s5_params_kernel,
        out_shape=[jax.ShapeDtypeStruct((2, g * c, n), BF16), jax.ShapeDtypeStruct((2, g * c, n), BF16),
                   jax.ShapeDtypeStruct((SUBLANES, n), F32), jax.ShapeDtypeStruct((SUBLANES, n), F32),
                   jax.ShapeDtypeStruct((seg_len * SUBLANES, n), F32), jax.ShapeDtypeStruct((seg_len * SUBLANES, n), F32),
                   jax.ShapeDtypeStruct((3, SUBLANES, n), F32), jax.ShapeDtypeStruct((3, SUBLANES, n), F32)],
        compiler_params=pltpu.CompilerParams(vmem_limit_bytes=VMEM_LIMIT),
    )(flat(lam_re), flat(lam_im), ldt, b_t(b_re), b_t(b_im), c_t(c_re), c_t(c_im))
    wb, wct, a_re, a_im, pk_re, pk_im, sg_re, sg_im = outs
    wc = jnp.transpose(wct, (0, 2, 1))
    pk_re, pk_im = (a.reshape(seg_len, SUBLANES, n) for a in (pk_re, pk_im))
    return wb, wc, a_re, a_im, pk_re, pk_im, sg_re, sg_im


def _gelu_tanh(x):
    return 0.5 * x * (1.0 + jnp.tanh(math.sqrt(2.0 / math.pi) * (x + 0.044715 * (x * x * x))))


def _even_kernel(p_ref, band_ref, perm_ref, wpool_ref, pscale_ref, wb_ref, wc_ref, dskip_ref, wglu_ref, bglu_ref,
                 a_re_ref, a_im_ref, pk_re_ref, pk_im_ref, sg_re_ref, sg_im_ref,
                 o_ref, prev_s, xr_s, xi_s, car_re, car_im, *, mix_a, scan_lanes):
    i = pl.program_id(1)
    tt = p_ref.shape[0]

    @pl.when(i == 0)
    def _():
        prev_s[...] = jnp.zeros_like(prev_s)
        car_re[...] = jnp.zeros_like(car_re)
        car_im[...] = jnp.zeros_like(car_im)

    ua = p_ref[:, :mix_a]
    ext = jnp.concatenate([prev_s[...], ua], axis=0)
    t1 = i * tt + lax.broadcasted_iota(jnp.int32, (tt, POOL_CH), 0) + 1
    for g, win in enumerate(POOL_WINDOWS):
        sl = slice(g * POOL_CH, (g + 1) * POOL_CH)
        wsum = _dot(band_ref[g], ext[:, sl])
        cnt = jnp.minimum(t1, win).astype(F32)
        pg = wsum / cnt - ua[:, sl].astype(F32)
        yg = _dot(pg.astype(BF16), wpool_ref[g]) * pscale_ref[:, sl]
        o_ref[:, sl] = yg.astype(o_ref.dtype)
    prev_s[...] = ua

    ub = _dot(perm_ref[0], p_ref[:, mix_a:]).astype(BF16)
    n_tiles = xr_s.shape[0]
    n_diag_blocks = ub.shape[1] // LANES
    tpb = n_tiles // n_diag_blocks
    sw = tpb * LANES
    for j in range(n_diag_blocks):
        cs, ss = slice(j * LANES, (j + 1) * LANES), slice(j * sw, (j + 1) * sw)
        xr, xi = _dot(ub[:, cs], wb_ref[0, cs, ss]), _dot(ub[:, cs], wb_ref[1, cs, ss])
        for q in range(tpb):
            xr_s[j * tpb + q] = xr[:, q * LANES:(q + 1) * LANES]
            xi_s[j * tpb + q] = xi[:, q * LANES:(q + 1) * LANES]

    seg_len = tt // SUBLANES
    seg_rows = lambda k: pl.ds(pl.multiple_of(k * SUBLANES, SUBLANES), SUBLANES)
    first_seg = lax.broadcasted_iota(jnp.int32, (SUBLANES, LANES), 0) == 0
    zeros = jnp.zeros((SUBLANES, LANES), F32)
    tpc = scan_lanes // LANES
    for c in range(n_tiles // tpc):
        tiles = list(range(c * tpc, (c + 1) * tpc))
        lanes = [slice(t * LANES, (t + 1) * LANES) for t in tiles]

        def local(k, carry, tiles=tiles, lanes=lanes):
            out = []
            for t, ls, (hr, hi) in zip(tiles, lanes, carry):
                dr, di = _cmul(a_re_ref[:, ls], a_im_ref[:, ls], hr, hi)
                hr, hi = dr + xr_s[t, seg_rows(k), :], di + xi_s[t, seg_rows(k), :]
                xr_s[t, seg_rows(k), :] = hr
                xi_s[t, seg_rows(k), :] = hi
                out.append((hr, hi))
            return tuple(out)

        ends = lax.fori_loop(0, seg_len, local, tuple((zeros, zeros) for _ in tiles))
        starts = []
        for t, ls, (er, ei) in zip(tiles, lanes, ends):
            fr = jnp.where(first_seg, car_re[:, ls], pltpu.roll(er, 1, 0))
            fi = jnp.where(first_seg, car_im[:, ls], pltpu.roll(ei, 1, 0))
            for k in range(3):
                dr, di = _cmul(sg_re_ref[k, :, ls], sg_im_ref[k, :, ls],
                               pltpu.roll(fr, 1 << k, 0), pltpu.roll(fi, 1 << k, 0))
                fr, fi = fr + dr, fi + di
            starts.append((fr, fi))
            dr, di = _cmul(pk_re_ref[seg_len - 1, :, ls], pk_im_ref[seg_len - 1, :, ls], fr, fi)
            car_re[:, ls] = jnp.broadcast_to((dr + er)[SUBLANES - 1:], fr.shape)
            car_im[:, ls] = jnp.broadcast_to((di + ei)[SUBLANES - 1:], fr.shape)

        def fix(k, carry, tiles=tiles, lanes=lanes, starts=starts):
            for t, ls, (fr, fi) in zip(tiles, lanes, starts):
                dr, di = _cmul(pk_re_ref[k, :, ls], pk_im_ref[k, :, ls], fr, fi)
                xr_s[t, seg_rows(k), :] += dr
                xi_s[t, seg_rows(k), :] += di
            return carry

        lax.fori_loop(0, seg_len, fix, 0)

    state = lambda ref, j: jnp.concatenate([ref[j * tpb + q] for q in range(tpb)], axis=1).astype(BF16)
    y = jnp.concatenate(
        [_dot(state(xr_s, j), wc_ref[0, j * sw:(j + 1) * sw, j * LANES:(j + 1) * LANES])
         - _dot(state(xi_s, j), wc_ref[1, j * sw:(j + 1) * sw, j * LANES:(j + 1) * LANES])
         for j in range(n_diag_blocks)], axis=1)
    y = _gelu_tanh(y + dskip_ref[...] * ub.astype(F32))
    z = _dot(y.astype(BF16), wglu_ref[...]) + bglu_ref[...]
    out = (y * (1.0 / (1.0 + jnp.exp(-z)))).astype(o_ref.dtype)
    o_ref[:, mix_a:] = _dot(perm_ref[1], out).astype(o_ref.dtype)


def _segment_perm(tt):
    seg_len = tt // SUBLANES
    r = jnp.arange(tt)
    src = (r % SUBLANES) * seg_len + r // SUBLANES
    p = (src[:, None] == jnp.arange(tt)[None, :])
    return jnp.stack([p, p.T]).astype(BF16)


def _pool_band(tt):
    t = jnp.arange(tt)[:, None] + tt
    j = jnp.arange(2 * tt)[None, :]
    return jnp.stack([((j <= t) & (j > t - w)) for w in POOL_WINDOWS]).astype(BF16)


def _even_mixer(proj, wpool, pscale, s5, dskip, wglu, bglu, tt, scan_lanes):
    bsz, seq, width = proj.shape
    wb, wc, a_re, a_im, pk_re, pk_im, sg_re, sg_im = s5
    mix_a = len(POOL_WINDOWS) * POOL_CH
    mix_b = width - mix_a
    n_state = wb.shape[2]
    const = lambda a: pl.BlockSpec(a.shape, lambda b, i, nd=a.ndim: (0,) * nd)
    band = _pool_band(tt)
    args = (band, _segment_perm(tt), wpool, pscale, wb, wc, dskip, wglu, bglu, a_re, a_im, pk_re, pk_im, sg_re, sg_im)
    return pl.pallas_call(
        functools.partial(_even_kernel, mix_a=mix_a, scan_lanes=scan_lanes),
        grid=(bsz, seq // tt),
        in_specs=[pl.BlockSpec((None, tt, width), lambda b, i: (b, i, 0))] + [const(a) for a in args],
        out_specs=pl.BlockSpec((None, tt, width), lambda b, i: (b, i, 0)),
        out_shape=jax.ShapeDtypeStruct((bsz, seq, width), BF16),
        scratch_shapes=[pltpu.VMEM((tt, mix_a), BF16),
                        pltpu.VMEM((n_state // LANES, tt, LANES), F32), pltpu.VMEM((n_state // LANES, tt, LANES), F32),
                        pltpu.VMEM((SUBLANES, n_state), F32), pltpu.VMEM((SUBLANES, n_state), F32)],
        compiler_params=_cparams(("parallel", "arbitrary")),
    )(proj, *args)


def _pair_blockdiag(x, first_ref, second_ref):
    return jnp.concatenate([x * first_ref[...], x * second_ref[...]], axis=0)


def _dsa_kernel(q_ref, qi_ref, wq_ref, k_ref, v_ref, kk_ref, tri_ref, ones_ref, first_ref, second_ref, alibi_ref,
                o_ref,
                s_s, st_s, qs_s, mx_s, al_s, *, topk, qb):
    i = pl.program_id(1)
    kb_size = ATT_BLOCK
    n_diag = qb // kb_size
    nkb = (i + 1) * n_diag
    shape = (qb, kb_size)
    row = lax.broadcasted_iota(jnp.int32, shape, 0)
    col = lax.broadcasted_iota(jnp.int32, shape, 1)
    q_chunk = (i * qb + row) // CHUNK
    bcast = lambda a: jnp.broadcast_to(a, shape)
    kf = float(topk)
    pairs = [slice(p * LANES, (p + 1) * LANES) for p in range(N_HEADS // 2)]

    wq = wq_ref[...]
    w_cols = [bcast(wq[:, h:h + 1] * IDX_SCALE) for h in range(IDX_HEADS)]

    def indexer(kb, rows):
        kbd = _pair_blockdiag(kk_ref[kb], first_ref, second_ref)
        acc = jnp.zeros((rows.stop - rows.start, kb_size), F32)
        for p, ps in enumerate(pairs):
            rel = jnp.maximum(_dot_nt(qi_ref[rows, ps], kbd), 0.0)
            acc = acc + rel[:, :kb_size] * w_cols[2 * p][rows] + rel[:, kb_size:] * w_cols[2 * p + 1][rows]
        return acc

    def store_scores(kb, r0, scores):
        s_s[kb, r0:, :] = scores
        for c in range(r0 // kb_size, qb // kb_size):
            st_s[kb, :, c * kb_size:(c + 1) * kb_size] = scores[c * kb_size - r0:(c + 1) * kb_size - r0].T
        if r0:
            s_s[kb, :r0, :] = jnp.full((r0, kb_size), -jnp.inf, F32)
            st_s[kb, :, :r0] = jnp.full((kb_size, r0), -jnp.inf, F32)

    def past_block(kb, carry):
        rmax, rmin = carry
        acc = indexer(kb, slice(0, qb))
        store_scores(kb, 0, acc)
        return jnp.maximum(rmax, acc), jnp.minimum(rmin, acc)

    rmax, rmin = lax.fori_loop(0, i * n_diag, past_block,
                               (jnp.full(shape, -jnp.inf, F32), jnp.full(shape, jnp.inf, F32)))
    for u in range(n_diag):
        kb, r0 = i * n_diag + u, u * kb_size
        rows = slice(r0, qb)
        acc = indexer(kb, rows)
        adm = (((kb * kb_size + col) // CHUNK) <= q_chunk)[rows]
        store_scores(kb, r0, jnp.where(adm, acc, -jnp.inf))
        hi_part = jnp.maximum(rmax[rows], jnp.where(adm, acc, -jnp.inf))
        lo_part = jnp.minimum(rmin[rows], jnp.where(adm, acc, jnp.inf))
        if r0:
            hi_part = jnp.concatenate([rmax[:r0], hi_part], axis=0)
            lo_part = jnp.concatenate([rmin[:r0], lo_part], axis=0)
        rmax, rmin = hi_part, lo_part
    n_chunks = qb // kb_size
    chunk_rows = lambda c: slice(c * kb_size, (c + 1) * kb_size)
    chunk_subl = lambda c: slice(c * SUBLANES, (c + 1) * SUBLANES)
    tile = (kb_size, kb_size)

    def to_dense(rep):
        return rep.T[:SUBLANES]

    def to_rows(dense):
        return jnp.broadcast_to(dense[:1], tile).T

    def dense_of(fn):
        return jnp.concatenate([fn(c) for c in range(n_chunks)], axis=0)

    def fold_keys(c, fn, init):
        def body(g, a):
            for u in range(n_diag):
                a = fn(a, st_s[g * n_diag + u, :, chunk_rows(c)])
            return a
        return lax.fori_loop(0, i + 1, body, init)

    def per_row(dense_c):
        return jnp.broadcast_to(dense_c[:1], tile)

    def over_keys(reduce, folded):
        return jnp.broadcast_to(reduce(folded, axis=0, keepdims=True), (SUBLANES, kb_size))

    def count(x, cmp):
        def one(c):
            xb = per_row(x[chunk_subl(c)])
            return over_keys(jnp.sum, fold_keys(c, lambda a, s: a + jnp.where(cmp(s, xb), 1.0, 0.0),
                                                jnp.zeros(tile, F32)))
        return dense_of(one)

    count_ge = functools.partial(count, cmp=lambda s, xb: s >= xb)
    count_gt = functools.partial(count, cmp=lambda s, xb: s > xb)
    count_eq = functools.partial(count, cmp=lambda s, xb: s == xb)

    smax = dense_of(lambda c: to_dense(jnp.broadcast_to(_rowmax(rmax[chunk_rows(c)]), tile)))
    smin = dense_of(lambda c: to_dense(jnp.broadcast_to(jnp.min(rmin[chunk_rows(c)], axis=1, keepdims=True), tile)))
    dshape = (n_chunks * SUBLANES, kb_size)
    t_dense = (i * qb + (lax.broadcasted_iota(jnp.int32, dshape, 0) // SUBLANES) * kb_size
               + lax.broadcasted_iota(jnp.int32, dshape, 1))
    n_adm = ((t_dense // CHUNK + 1) * CHUNK).astype(F32)
    tiny = float(jnp.finfo(F32).tiny)
    hi0 = smax + jnp.maximum(jnp.abs(smax) * BRACKET_MARGIN, tiny)
    zero = jnp.zeros_like(smin)
    cge0, cgt0 = count_ge(zero), count_gt(zero)
    wide = n_adm > kf
    zero_tie = jnp.logical_and(wide, jnp.logical_and(cgt0 < kf, cge0 >= kf))
    above = jnp.logical_and(wide, cgt0 >= kf)
    below = jnp.logical_and(wide, cge0 < kf)
    lo0 = jnp.where(jnp.logical_or(above, zero_tie), 0.0, smin)
    clo0 = jnp.where(zero_tie, kf, jnp.where(above, cge0, n_adm))
    hi0 = jnp.where(below, 0.0, hi0)
    chi0 = jnp.where(below, cge0, 0.0)
    need0 = jnp.where(zero_tie, kf - cgt0, COUNT_ALL)

    def unresolved(clo, chi):
        return jnp.max(jnp.where(jnp.logical_and(clo > kf, clo - chi > POP_SPAN), 1.0, 0.0))

    def bisect_cond(carry):
        return jnp.logical_and(carry[0] > 0.0, carry[1] < BISECT_ITERS)

    def bisect(carry):
        _, it, lo, hi, clo, chi = carry
        span = clo - chi
        aim = (clo - (kf + 0.5)) / span
        interpolate = jnp.logical_and(span < INTERP_SPAN, it % 2 == 0)
        mid = lo + jnp.where(interpolate, aim, 0.5) * (hi - lo)
        c = count_ge(mid)
        active = clo > kf
        up = jnp.logical_and(active, c >= kf)
        down = jnp.logical_and(active, c < kf)
        clo = jnp.where(up, c, clo)
        chi = jnp.where(down, c, chi)
        return (unresolved(clo, chi), it + 1, jnp.where(up, mid, lo), jnp.where(down, mid, hi), clo, chi)

    _, _, lo, hi, clo, chi = lax.while_loop(
        bisect_cond, bisect, (unresolved(clo0, chi0), jnp.int32(0), lo0, hi0, clo0, chi0))

    done0 = jnp.where(clo <= kf, 1.0, 0.0)

    def pop_cond(carry):
        return jnp.logical_and(carry[0] > 0.0, carry[1] < MAX_POPS)

    def next_below(hi):
        def one(c):
            hib = per_row(hi[chunk_subl(c)])
            return over_keys(jnp.max, fold_keys(c, lambda a, s: jnp.maximum(a, jnp.where(s < hib, s, -jnp.inf)),
                                                jnp.full(tile, -jnp.inf, F32)))
        return dense_of(one)

    def pop(carry):
        _, it, hi, chi, thr, need, done = carry
        v = next_below(hi)
        cv = count_eq(v)
        active = done < 0.5
        fin = jnp.logical_and(active, chi + cv >= kf)
        go = jnp.logical_and(active, chi + cv < kf)
        thr = jnp.where(fin, v, thr)
        need = jnp.where(fin, kf - chi, need)
        done = jnp.where(fin, 1.0, done)
        hi = jnp.where(go, v, hi)
        chi = jnp.where(go, chi + cv, chi)
        return (jnp.max(1.0 - done), it + 1, hi, chi, thr, need, done)

    left0 = jnp.max(1.0 - done0)
    _, _, _, _, thr, need, _ = lax.while_loop(
        pop_cond, pop, (left0, jnp.int32(0), hi, chi, lo, need0, done0))
    thrb = jnp.concatenate([to_rows(thr[chunk_subl(c)]) for c in range(n_chunks)], axis=0)
    needb = jnp.concatenate([to_rows(need[chunk_subl(c)]) for c in range(n_chunks)], axis=0)
    partial_ties = jnp.max(jnp.where(need < COUNT_ALL, 1.0, 0.0))

    @pl.when(partial_ties > 0.0)
    def _():
        def tie_block(kb, run):
            s = s_s[kb]
            eq = jnp.where(s == thrb, 1.0, 0.0).astype(BF16)
            rank = _dot(eq, tri_ref[1]) + run
            take = jnp.where(s == thrb, jnp.where(rank < needb, 0.0, NEG_BIG), NEG_BIG)
            s_s[kb] = jnp.where(s > thrb, 0.0, take)
            return run + _dot(eq, tri_ref[2])
        lax.fori_loop(0, nkb, tie_block, jnp.zeros(shape, F32))

    @pl.when(partial_ties <= 0.0)
    def _():
        def sel_block(kb, carry):
            s_s[kb] = jnp.where(s_s[kb] >= thrb, 0.0, NEG_BIG)
            return carry
        lax.fori_loop(0, nkb, sel_block, 0)

    first_half = col < HEAD_DIM
    qs_s[...] = q_ref[...] * (HEAD_DIM ** -0.5)
    mx_s[...] = jnp.full(mx_s.shape, NEG_BIG, F32)
    al_s[...] = jnp.zeros_like(al_s)
    slopes = [2.0 ** (-8.0 * (h + 1) / N_HEADS) for h in range(N_HEADS)]

    q_alibi = jnp.where(col < ALIBI_LANES, 1.0, 0.0).astype(BF16)

    def attend_group(g, past):
        blocks = [g * n_diag + u for u in range(n_diag)]
        if not past:
            t_pos = (i * qb + row).astype(F32)
            hats = []
            for kb in blocks:
                s_pos = (kb * kb_size + col).astype(F32)
                hats.append(jnp.minimum(s_pos, 2.0 * t_pos - s_pos))
        for p, ps in enumerate(pairs):
            tiles = []
            for u, kb in enumerate(blocks):
                kbd = _pair_blockdiag(k_ref[kb, :, ps], first_ref, second_ref)
                if past:
                    mask = s_s[kb]
                    k_alibi = alibi_ref[0, p] + alibi_ref[1, p] * kb.astype(BF16)
                    logits = _dot_nt(jnp.concatenate([qs_s[:, ps], q_alibi], axis=1),
                                     jnp.concatenate([kbd, k_alibi], axis=1))
                    tiles.append((0, logits + jnp.concatenate([mask, mask], axis=1)))
                else:
                    r0 = u * kb_size
                    mask, hat = s_s[kb, r0:, :], hats[u][r0:]
                    bias = jnp.concatenate([mask + slopes[2 * p] * hat, mask + slopes[2 * p + 1] * hat], axis=1)
                    tiles.append((r0, _dot_nt(qs_s[r0:, ps], kbd) + bias))

            def padded(r0, part, fill):
                return part if r0 == 0 else jnp.concatenate(
                    [jnp.full((r0, part.shape[1]), fill, part.dtype), part], axis=0)

            top = functools.reduce(jnp.maximum, [padded(r0, t, NEG_BIG) for r0, t in tiles])
            m_old = mx_s[p]
            m_new = jnp.maximum(m_old, jnp.concatenate(
                [bcast(_rowmax(top[:, :kb_size])), bcast(_rowmax(top[:, kb_size:]))], axis=1))
            mx_s[p] = m_new
            pexp = jnp.concatenate(
                [padded(r0, jnp.exp((t - m_new[r0:]).astype(BF16)), 0.0) for r0, t in tiles], axis=1)
            shrink = m_old - m_new
            alpha = jnp.exp(jnp.where(first_half, shrink[:, :kb_size], shrink[:, kb_size:]))
            values = jnp.concatenate(
                [jnp.concatenate([_pair_blockdiag(v_ref[kb, :, ps], first_ref, second_ref), ones_ref[...]], axis=1)
                 for kb in blocks], axis=0)
            al_s[p] = jnp.concatenate([alpha, alpha], axis=1) * al_s[p] + _dot(pexp, values)

    def past_group(g, carry):
        attend_group(g, True)
        return carry

    lax.fori_loop(0, i, past_group, 0)
    attend_group(i, False)
    for p, ps in enumerate(pairs):
        state = al_s[p]
        o_ref[:, ps] = (state[:, :LANES] / state[:, LANES:]).astype(o_ref.dtype)


def _tri_consts(n):
    j = jnp.arange(n)[:, None]
    s = jnp.arange(n)[None, :]
    return jnp.stack([j > s, j < s, jnp.ones((n, n), bool)]).astype(BF16)


def _pair_consts(n):
    tri = _tri_consts(n)
    zero = jnp.zeros((n, n), BF16)
    tri2 = jnp.block([[tri[0], zero], [zero, tri[0]]])
    first = (jnp.arange(n)[None, :] < n // 2) & jnp.ones((n, 1), bool)
    ones2 = jnp.concatenate([first, ~first], axis=0).astype(BF16)
    return tri2, ones2, first.astype(BF16), (~first).astype(BF16)


def _alibi_key_terms(n):
    r = jnp.arange(2 * n)[:, None]
    lane = jnp.arange(LANES)[None, :]
    head_b = r >= n
    out = []
    for p in range(N_HEADS // 2):
        slope = jnp.where(head_b, 2.0 ** (-8.0 * (2 * p + 2) / N_HEADS), 2.0 ** (-8.0 * (2 * p + 1) / N_HEADS))
        in_block = jnp.where(lane == jnp.where(head_b, 3, 1), slope * (r % n), 0.0)
        per_block = jnp.where(lane == jnp.where(head_b, 2, 0), slope * n, 0.0)
        out.append(jnp.stack([in_block, per_block]))
    return jnp.stack(out, axis=1).astype(BF16)


def _dsa(q, k, v, qi, kk, wq, topk, qb):
    bsz, seq, width = q.shape
    nq, nk = seq // qb, seq // ATT_BLOCK
    qblk = lambda a: a.reshape(bsz, nq, qb, a.shape[-1])
    kblk = lambda a: a.reshape(bsz, nk, ATT_BLOCK, a.shape[-1])
    qspec = lambda wd: pl.BlockSpec((None, None, qb, wd), lambda b, i: (b, i, 0, 0))
    kspec = lambda wd: pl.BlockSpec((None, nk, ATT_BLOCK, wd), lambda b, i: (b, 0, 0, 0))
    _, ones2, first, second = _pair_consts(ATT_BLOCK)
    consts = (_tri_consts(ATT_BLOCK), ones2, first, second, _alibi_key_terms(ATT_BLOCK))
    out = pl.pallas_call(
        functools.partial(_dsa_kernel, topk=topk, qb=qb),
        grid=(bsz, nq),
        in_specs=[qspec(width), qspec(width), qspec(LANES), kspec(width), kspec(width), kspec(LANES)]
                 + [pl.BlockSpec(a.shape, lambda b, i, nd=a.ndim: (0,) * nd) for a in consts],
        out_specs=qspec(width),
        out_shape=jax.ShapeDtypeStruct((bsz, nq, qb, width), BF16),
        scratch_shapes=[pltpu.VMEM((nk, qb, ATT_BLOCK), F32), pltpu.VMEM((nk, ATT_BLOCK, qb), F32),
                        pltpu.VMEM((qb, width), BF16),
                        pltpu.VMEM((N_HEADS // 2, qb, 2 * ATT_BLOCK), F32),
                        pltpu.VMEM((N_HEADS // 2, qb, 2 * LANES), F32)],
        compiler_params=_cparams(("parallel", "arbitrary")),
    )(qblk(q), qblk(qi), qblk(wq), kblk(k), kblk(v), kblk(kk), *consts)
    return out.reshape(bsz, seq, width)


def _dsa_t_kernel(q_ref, qi_ref, wq_ref, k_ref, vt_ref, kk_ref, tri_ref, first_ref, second_ref, alibi_ref,
                  rows_a_ref, ones_t_ref, o_ref, st_s, ms_s, al_s, *, topk, qb):
    i = pl.program_id(1)
    kb_size = ATT_BLOCK
    n_diag = qb // kb_size
    nkb = (i + 1) * n_diag
    n_pairs = N_HEADS // 2
    kf = float(topk)
    tile = (kb_size, kb_size)
    pairs = [slice(p * LANES, (p + 1) * LANES) for p in range(n_pairs)]
    key_r = lax.broadcasted_iota(jnp.int32, (kb_size, qb), 0)
    qry_l = lax.broadcasted_iota(jnp.int32, (kb_size, qb), 1)
    q_chunk = (i * qb + qry_l) // CHUNK
    over = lambda a, rows: jnp.broadcast_to(a, (rows, a.shape[1]))

    w_rows = [wq_ref[h:h + 1, :] * IDX_SCALE for h in range(IDX_HEADS)]

    def indexer(kb, l0):
        kbd = _pair_blockdiag(kk_ref[kb], first_ref, second_ref)
        acc = jnp.zeros((kb_size, qb - l0), F32)
        for p, ps in enumerate(pairs):
            rel = jnp.maximum(_dot_nt(kbd, qi_ref[l0:, ps]), 0.0)
            acc = acc + rel[:kb_size] * w_rows[2 * p][:, l0:] + rel[kb_size:] * w_rows[2 * p + 1][:, l0:]
        return acc

    def past_block(kb, carry):
        rmax, rmin = carry
        acc = indexer(kb, 0)
        st_s[kb] = acc
        return jnp.maximum(rmax, acc), jnp.minimum(rmin, acc)

    rmax, rmin = lax.fori_loop(0, i * n_diag, past_block,
                               (jnp.full((kb_size, qb), -jnp.inf, F32), jnp.full((kb_size, qb), jnp.inf, F32)))
    for u in range(n_diag):
        kb, l0 = i * n_diag + u, u * kb_size
        acc = indexer(kb, l0)
        adm = (((kb * kb_size + key_r) // CHUNK) <= q_chunk)[:, l0:]
        st_s[kb, :, l0:] = jnp.where(adm, acc, -jnp.inf)
        hi_part = jnp.maximum(rmax[:, l0:], jnp.where(adm, acc, -jnp.inf))
        lo_part = jnp.minimum(rmin[:, l0:], jnp.where(adm, acc, jnp.inf))
        if l0:
            st_s[kb, :, :l0] = jnp.full((kb_size, l0), -jnp.inf, F32)
            hi_part = jnp.concatenate([rmax[:, :l0], hi_part], axis=1)
            lo_part = jnp.concatenate([rmin[:, :l0], lo_part], axis=1)
        rmax, rmin = hi_part, lo_part

    n_chunks = qb // kb_size
    chunk = lambda c: slice(c * kb_size, (c + 1) * kb_size)
    dense = lambda a: jnp.broadcast_to(a, (SUBLANES, a.shape[1]))

    def fold_keys(c, fn, init):
        def body(g, a):
            for u in range(n_diag):
                a = fn(a, st_s[g * n_diag + u, :, chunk(c)])
            return a
        return lax.fori_loop(0, i + 1, body, init)

    def per_chunk(fn):
        return jnp.concatenate([fn(c) for c in range(n_chunks)], axis=1)

    def count(x, cmp):
        def one(c):
            xb = over(x[:1, chunk(c)], kb_size)
            cnt = fold_keys(c, lambda a, s: a + jnp.where(cmp(s, xb), 1.0, 0.0), jnp.zeros(tile, F32))
            return dense(jnp.sum(cnt, axis=0, keepdims=True))
        return per_chunk(one)

    count_ge = functools.partial(count, cmp=lambda s, xb: s >= xb)
    count_gt = functools.partial(count, cmp=lambda s, xb: s > xb)
    count_eq = functools.partial(count, cmp=lambda s, xb: s == xb)

    smax = dense(jnp.max(rmax, axis=0, keepdims=True))
    smin = dense(jnp.min(rmin, axis=0, keepdims=True))
    t_dense = i * qb + lax.broadcasted_iota(jnp.int32, (SUBLANES, qb), 1)
    n_adm = ((t_dense // CHUNK + 1) * CHUNK).astype(F32)
    tiny = float(jnp.finfo(F32).tiny)
    hi0 = smax + jnp.maximum(jnp.abs(smax) * BRACKET_MARGIN, tiny)
    zero = jnp.zeros_like(smin)
    cge0, cgt0 = count_ge(zero), count_gt(zero)
    wide = n_adm > kf
    zero_tie = jnp.logical_and(wide, jnp.logical_and(cgt0 < kf, cge0 >= kf))
    above = jnp.logical_and(wide, cgt0 >= kf)
    below = jnp.logical_and(wide, cge0 < kf)
    lo0 = jnp.where(jnp.logical_or(above, zero_tie), 0.0, smin)
    clo0 = jnp.where(zero_tie, kf, jnp.where(above, cge0, n_adm))
    hi0 = jnp.where(below, 0.0, hi0)
    chi0 = jnp.where(below, cge0, 0.0)
    need0 = jnp.where(zero_tie, kf - cgt0, COUNT_ALL)

    def unresolved(clo, chi):
        return jnp.max(jnp.where(jnp.logical_and(clo > kf, clo - chi > POP_SPAN), 1.0, 0.0))

    def bisect_cond(carry):
        return jnp.logical_and(carry[0] > 0.0, carry[1] < BISECT_ITERS)

    def bisect(carry):
        _, it, lo, hi, clo, chi = carry
        span = clo - chi
        aim = (clo - (kf + 0.5)) / span
        interpolate = jnp.logical_and(span < INTERP_SPAN, it % 2 == 0)
        mid = lo + jnp.where(interpolate, aim, 0.5) * (hi - lo)
        c = count_ge(mid)
        active = clo > kf
        up = jnp.logical_and(active, c >= kf)
        down = jnp.logical_and(active, c < kf)
        clo = jnp.where(up, c, clo)
        chi = jnp.where(down, c, chi)
        return (unresolved(clo, chi), it + 1, jnp.where(up, mid, lo), jnp.where(down, mid, hi), clo, chi)

    _, _, lo, hi, clo, chi = lax.while_loop(
        bisect_cond, bisect, (unresolved(clo0, chi0), jnp.int32(0), lo0, hi0, clo0, chi0))

    done0 = jnp.where(clo <= kf, 1.0, 0.0)

    def pop_cond(carry):
        return jnp.logical_and(carry[0] > 0.0, carry[1] < MAX_POPS)

    def next_below(hi):
        def one(c):
            hib = over(hi[:1, chunk(c)], kb_size)
            best = fold_keys(c, lambda a, s: jnp.maximum(a, jnp.where(s < hib, s, -jnp.inf)),
                             jnp.full(tile, -jnp.inf, F32))
            return dense(jnp.max(best, axis=0, keepdims=True))
        return per_chunk(one)

    def pop(carry):
        _, it, hi, chi, thr, need, done = carry
        v = next_below(hi)
        cv = count_eq(v)
        active = done < 0.5
        fin = jnp.logical_and(active, chi + cv >= kf)
        go = jnp.logical_and(active, chi + cv < kf)
        thr = jnp.where(fin, v, thr)
        need = jnp.where(fin, kf - chi, need)
        done = jnp.where(fin, 1.0, done)
        hi = jnp.where(go, v, hi)
        chi = jnp.where(go, chi + cv, chi)
        return (jnp.max(1.0 - done), it + 1, hi, chi, thr, need, done)

    left0 = jnp.max(1.0 - done0)
    _, _, _, _, thr, need, _ = lax.while_loop(
        pop_cond, pop, (left0, jnp.int32(0), hi, chi, lo, need0, done0))
    thrb, needb = over(thr[:1], kb_size), over(need[:1], kb_size)
    partial_ties = jnp.max(jnp.where(need < COUNT_ALL, 1.0, 0.0))

    @pl.when(partial_ties > 0.0)
    def _():
        def tie_block(kb, run):
            s = st_s[kb]
            eq = jnp.where(s == thrb, 1.0, 0.0).astype(BF16)
            rank = _dot(tri_ref[0], eq) + run
            take = jnp.where(s == thrb, jnp.where(rank < needb, 0.0, NEG_BIG), NEG_BIG)
            st_s[kb] = jnp.where(s > thrb, 0.0, take)
            return run + over(jnp.sum(eq.astype(F32), axis=0, keepdims=True), kb_size)
        lax.fori_loop(0, nkb, tie_block, jnp.zeros((kb_size, qb), F32))

    @pl.when(partial_ties <= 0.0)
    def _():
        def sel_block(kb, carry):
            st_s[kb] = jnp.where(st_s[kb] >= thrb, 0.0, NEG_BIG)
            return carry
        lax.fori_loop(0, nkb, sel_block, 0)

    ms_s[...] = jnp.full(ms_s.shape, NEG_BIG, F32)
    al_s[...] = jnp.zeros_like(al_s)
    slopes = [2.0 ** (-8.0 * (h + 1) / N_HEADS) for h in range(N_HEADS)]
    q_alibi = jnp.where(lax.broadcasted_iota(jnp.int32, (qb, LANES), 1) < ALIBI_LANES, 1.0, 0.0).astype(BF16)
    scale = HEAD_DIM ** -0.5

    def attend_group(g, past):
        blocks = [g * n_diag + u for u in range(n_diag)]
        if not past:
            t_pos = (i * qb + qry_l).astype(F32)
            hats = []
            for kb in blocks:
                s_pos = (kb * kb_size + key_r).astype(F32)
                hats.append(jnp.minimum(s_pos, 2.0 * t_pos - s_pos))
        for p, ps in enumerate(pairs):
            tiles = []
            for u, kb in enumerate(blocks):
                kbd = _pair_blockdiag(k_ref[kb, :, ps], first_ref, second_ref)
                if past:
                    mask = st_s[kb]
                    k_alibi = alibi_ref[0, p] + alibi_ref[1, p] * kb.astype(BF16)
                    logits = _dot_nt(jnp.concatenate([kbd, k_alibi], axis=1),
                                     jnp.concatenate([q_ref[:, ps] * scale, q_alibi], axis=1))
                    tiles.append((0, logits + jnp.concatenate([mask, mask], axis=0)))
                else:
                    l0 = u * kb_size
                    mask, hat = st_s[kb, :, l0:], hats[u][:, l0:]
                    bias = jnp.concatenate([mask + slopes[2 * p] * hat, mask + slopes[2 * p + 1] * hat], axis=0)
                    tiles.append((l0, _dot_nt(kbd, q_ref[l0:, ps] * scale) + bias))

            def padded(l0, part, fill):
                return part if l0 == 0 else jnp.concatenate(
                    [jnp.full((part.shape[0], l0), fill, part.dtype), part], axis=1)

            top = functools.reduce(jnp.maximum, [padded(l0, t, NEG_BIG) for l0, t in tiles])
            m_old = [ms_s[2 * p][:1], ms_s[2 * p + 1][:1]]
            m_new = [jnp.maximum(m_old[0], jnp.max(top[:kb_size], axis=0, keepdims=True)),
                     jnp.maximum(m_old[1], jnp.max(top[kb_size:], axis=0, keepdims=True))]
            ms_s[2 * p] = dense(m_new[0])
            ms_s[2 * p + 1] = dense(m_new[1])
            m_rows = jnp.concatenate([over(m_new[0], kb_size), over(m_new[1], kb_size)], axis=0)
            pexp = jnp.concatenate(
                [padded(l0, jnp.exp((t - m_rows[:, l0:]).astype(BF16)), 0.0) for l0, t in tiles], axis=0)
            alpha = [jnp.exp(m_old[0] - m_new[0]), jnp.exp(m_old[1] - m_new[1])]
            alpha_rows = jnp.concatenate([over(alpha[0], HEAD_DIM), over(alpha[1], HEAD_DIM),
                                          over(alpha[0], SUBLANES), over(alpha[1], SUBLANES)], axis=0)
            values = jnp.concatenate(
                [jnp.concatenate([jnp.concatenate([vt_ref[kb, ps, :] * rows_a_ref[...],
                                                   vt_ref[kb, ps, :] * (1.0 - rows_a_ref[...]).astype(BF16)], axis=1),
                                  ones_t_ref[...]], axis=0)
                 for kb in blocks], axis=1)
            al_s[p] = alpha_rows * al_s[p] + _dot(values, pexp)

    def past_group(g, carry):
        attend_group(g, True)
        return carry

    lax.fori_loop(0, i, past_group, 0)
    attend_group(i, False)
    for p, ps in enumerate(pairs):
        state = al_s[p]
        den_a = over(state[2 * HEAD_DIM:2 * HEAD_DIM + 1], HEAD_DIM)
        den_b = over(state[2 * HEAD_DIM + SUBLANES:2 * HEAD_DIM + SUBLANES + 1], HEAD_DIM)
        out_t = state[:2 * HEAD_DIM] / jnp.concatenate([den_a, den_b], axis=0)
        o_ref[:, ps] = out_t.T.astype(o_ref.dtype)


def _dsa_t(q, k, v, qi, kk, wq, topk, qb):
    bsz, seq, width = q.shape
    nq, nk = seq // qb, seq // ATT_BLOCK
    qblk = lambda a: a.reshape(bsz, nq, qb, a.shape[-1])
    kblk = lambda a: a.reshape(bsz, nk, ATT_BLOCK, a.shape[-1])
    v_t = jnp.swapaxes(kblk(v), 2, 3)
    w_t = jnp.swapaxes(qblk(wq[:, :, :SUBLANES]), 2, 3)
    qspec = lambda wd: pl.BlockSpec((None, None, qb, wd), lambda b, i: (b, i, 0, 0))
    kspec = lambda wd: pl.BlockSpec((None, nk, ATT_BLOCK, wd), lambda b, i: (b, 0, 0, 0))
    _, _, first, second = _pair_consts(ATT_BLOCK)
    rows_a = (jnp.arange(LANES)[:, None] < HEAD_DIM) & jnp.ones((1, ATT_BLOCK), bool)
    ones_t = jnp.concatenate([jnp.repeat(jnp.array([[1.0, 0.0]]), ATT_BLOCK, axis=1).reshape(1, 2 * ATT_BLOCK)
                              * jnp.ones((SUBLANES, 1)),
                              jnp.repeat(jnp.array([[0.0, 1.0]]), ATT_BLOCK, axis=1).reshape(1, 2 * ATT_BLOCK)
                              * jnp.ones((SUBLANES, 1))], axis=0)
    consts = (_tri_consts(ATT_BLOCK), first, second, _alibi_key_terms(ATT_BLOCK),
              rows_a.astype(BF16), ones_t.astype(BF16))
    out = pl.pallas_call(
        functools.partial(_dsa_t_kernel, topk=topk, qb=qb),
        grid=(bsz, nq),
        in_specs=[qspec(width), qspec(width),
                  pl.BlockSpec((None, None, SUBLANES, qb), lambda b, i: (b, i, 0, 0)),
                  kspec(width),
                  pl.BlockSpec((None, nk, width, ATT_BLOCK), lambda b, i: (b, 0, 0, 0)),
                  kspec(LANES)]
                 + [pl.BlockSpec(a.shape, lambda b, i, nd=a.ndim: (0,) * nd) for a in consts],
        out_specs=qspec(width),
        out_shape=jax.ShapeDtypeStruct((bsz, nq, qb, width), BF16),
        scratch_shapes=[pltpu.VMEM((nk, ATT_BLOCK, qb), F32),
                        pltpu.VMEM((N_HEADS, SUBLANES, qb), F32),
                        pltpu.VMEM((N_HEADS // 2, 2 * HEAD_DIM + 2 * SUBLANES, qb), F32)],
        compiler_params=_cparams(("parallel", "arbitrary")),
    )(qblk(q), qblk(qi), w_t, kblk(k), v_t, kblk(kk), *consts)
    return out.reshape(bsz, seq, width)


def _sb_kernel(q_ref, k_ref, v_ref, tri_ref, ones_ref, first_ref, second_ref, o_ref, qs_s, rem_s, acc_s, *, qb):
    i = pl.program_id(1)
    kb_size = ATT_BLOCK
    n_diag = qb // kb_size
    n_pairs = N_HEADS // 2
    shape = (qb, 2 * kb_size)
    row = lax.broadcasted_iota(jnp.int32, shape, 0)
    col = lax.broadcasted_iota(jnp.int32, shape, 1) % kb_size
    qs_s[...] = q_ref[...] * (HEAD_DIM ** -0.5)
    rem_s[...] = jnp.ones_like(rem_s)
    acc_s[...] = jnp.zeros_like(acc_s)

    def step(kb, masked, r0=0):
        rows = slice(r0, qb)
        causal = ((kb * kb_size + col) < (i * qb + row))[rows]
        pairs = [slice(p * LANES, (p + 1) * LANES) for p in range(n_pairs)]
        log_sig, log_keep = [], []
        for ps in pairs:
            kbd = _pair_blockdiag(k_ref[kb, :, ps], first_ref, second_ref)
            z = _dot_nt(qs_s[rows, ps], kbd).astype(BF16)
            soft = jnp.log(1.0 + jnp.exp(-jnp.abs(z)))
            keep = -jnp.maximum(z, 0.0) - soft
            log_sig.append(jnp.minimum(z, 0.0) - soft)
            log_keep.append(jnp.where(causal, keep, jnp.zeros_like(keep)) if masked else keep)
        for p, ps in enumerate(pairs):
            after = _dot(log_keep[p], tri_ref[...]).astype(BF16)
            w = jnp.exp(log_sig[p] + after)
            if masked:
                w = jnp.where(causal, w, jnp.zeros_like(w))
            vbd = _pair_blockdiag(v_ref[kb, :, ps], first_ref, second_ref)
            out = _dot(w, jnp.concatenate([vbd, ones_ref[...]], axis=1))
            rem = rem_s[p, rows]
            acc_s[p, rows] += rem * out[:, :LANES]
            rem_s[p, rows] = rem * jnp.maximum(1.0 - out[:, LANES:], 0.0)

    for u in reversed(range(n_diag)):
        step(i * n_diag + u, True, u * kb_size)

    def past_step(j, carry):
        step(i * n_diag - 1 - j, False)
        return carry

    lax.fori_loop(0, i * n_diag, past_step, 0)
    for p in range(n_pairs):
        o_ref[:, p * LANES:(p + 1) * LANES] = acc_s[p].astype(o_ref.dtype)


def _stick_breaking(q, k, v, qb):
    bsz, seq, width = q.shape
    nq, nk = seq // qb, seq // ATT_BLOCK
    qspec = pl.BlockSpec((None, None, qb, width), lambda b, i: (b, i, 0, 0))
    kspec = pl.BlockSpec((None, nk, ATT_BLOCK, width), lambda b, i: (b, 0, 0, 0))
    consts = _pair_consts(ATT_BLOCK)
    out = pl.pallas_call(
        functools.partial(_sb_kernel, qb=qb),
        grid=(bsz, nq),
        in_specs=[qspec, kspec, kspec] + [pl.BlockSpec(a.shape, lambda b, i: (0, 0)) for a in consts],
        out_specs=qspec,
        out_shape=jax.ShapeDtypeStruct((bsz, nq, qb, width), BF16),
        scratch_shapes=[pltpu.VMEM((qb, width), BF16),
                        pltpu.VMEM((N_HEADS // 2, qb, LANES), F32),
                        pltpu.VMEM((N_HEADS // 2, qb, LANES), F32)],
        compiler_params=_cparams(("parallel", "arbitrary")),
    )(q.reshape(bsz, nq, qb, width), k.reshape(bsz, nk, ATT_BLOCK, width),
      v.reshape(bsz, nk, ATT_BLOCK, width), *consts)
    return out.reshape(bsz, seq, width)


def _cd_weight(cd_w_in):
    cw = N_HEADS * HEAD_DIM
    iw = IDX_HEADS * IDX_DIM
    offs = [0, cw, 2 * cw, 3 * cw, 3 * cw + iw, 3 * cw + iw + IDX_DIM, 3 * cw + iw + IDX_DIM + IDX_HEADS]
    qkv_c = cd_w_in[:, :offs[3]]
    q_idx = cd_w_in[:, offs[3]:offs[4]]
    k_idx = cd_w_in[:, offs[4]:offs[5]]
    w_idx = cd_w_in[:, offs[5]:offs[6]]
    qkv_d = cd_w_in[:, offs[6]:]
    pad = jnp.zeros((cd_w_in.shape[0], LANES - w_idx.shape[1]), cd_w_in.dtype)
    return jnp.concatenate([qkv_c, q_idx, qkv_d, k_idx, k_idx, w_idx, pad], axis=1)


def kernel(x, c, norm_g, ada_w, ada_b, mlp_w1, mlp_w2, ab_w_in, ab_w_out, pool_w, pool_scale, ssm_lam_re, ssm_lam_im, ssm_log_dt, ssm_b_re, ssm_b_im, ssm_c_re, ssm_c_im, ssm_d, glu_w, glu_b, cd_w_in, cd_w_out, final_g):
    bsz, seq, d = x.shape
    depth = norm_g.shape[0]
    topk = min(TOPK_MAX, seq // 4)
    tm = min(PROJ_ROWS, seq)

    c_pad = jnp.zeros((SUBLANES, d), F32).at[:bsz].set(c)
    mod = _modulation(c_pad, ada_w.reshape(depth * 2, d, 3 * d), ada_b.reshape(depth * 2, 1, 3 * d))
    mod = mod[:, :, :bsz, None, :]

    for i in range(depth):
        j = i // 2
        shift, scale1, gate1 = mod[2 * i, 0], mod[2 * i, 1], mod[2 * i, 2]
        g = norm_g[i, 0][None, :]
        if i % 2 == 0:
            width = ab_w_in.shape[2]
            tt = min(MIXER_ROWS, seq)
            (proj,) = _norm_proj(x, g, shift, scale1, ab_w_in[j].astype(BF16), (width,), (BF16,), tm)
            s5 = _s5_params(ssm_lam_re[j], ssm_lam_im[j], ssm_log_dt[j],
                            ssm_b_re[j], ssm_b_im[j], ssm_c_re[j], ssm_c_im[j], seg_len=tt // SUBLANES)
            y = _even_mixer(proj, pool_w[j].astype(BF16), pool_scale[j][None, :], s5,
                            ssm_d[j][None, :], glu_w[j].astype(BF16), glu_b[j][None, :],
                            tt=tt, scan_lanes=SCAN_LANES)
            ys, w_out = (y,), ab_w_out[j]
        else:
            hw = N_HEADS * HEAD_DIM
            widths = (hw,) * 7 + (LANES, LANES)
            dtypes = (BF16,) * 8 + (F32,)
            qc, kc, vc, qi, qd, kd, vd, kk, wq = _norm_proj(
                x, g, shift, scale1, _cd_weight(cd_w_in[j]).astype(BF16), widths, dtypes, tm)
            oc = _dsa_t(qc, kc, vc, qi, kk, wq, topk, qb=ATT_QUERY_BLOCK)
            od = _stick_breaking(qd, kd, vd, qb=ATT_QUERY_BLOCK)
            ys, w_out = (oc, od), cd_w_out[j]
        shift, scale1, gate2 = mod[2 * i + 1, 0], mod[2 * i + 1, 1], mod[2 * i + 1, 2]
        x = _out_mlp(x, gate1, ys, w_out.astype(BF16), norm_g[i, 1][None, :], shift, scale1, gate2,
                     mlp_w1[i].astype(BF16), mlp_w2[i].astype(BF16), final_g[None, :],
                     final_norm=(i == depth - 1), tm=min(MLP_ROWS, seq), tf=MLP_FF)
    return x
```
